```python
import math
import jax, jax.numpy as jnp
from jax import lax
import numpy as np

D_MODEL = 2048
BATCH = 2
SEQ = 4096
DEPTH = 1

MLA_HEADS = 8
MLA_Q_RANK = 512
MLA_KV_RANK = 512
MLA_NOPE_DIM = 128
MLA_ROPE_DIM = 64
MLA_V_DIM = 128
DIFF_HEADS = 8
DIFF_QK_DIM = 64
DIFF_V_DIM = 2 * DIFF_QK_DIM
D_FF = -(-8 * D_MODEL // (3 * 256)) * 256
ROPE_THETA = 10000.0
Q_BLOCK = 128
ALPHA = (2 * DEPTH) ** 0.25
BETA = (8 * DEPTH) ** -0.25
RMS_EPS = 1e-6
SUBLN_EPS = 1e-5
LN_EPS = 1e-5

IN_SIZES = (
    MLA_Q_RANK,
    MLA_KV_RANK,
    MLA_ROPE_DIM,
    DIFF_HEADS * 2 * DIFF_QK_DIM,
    DIFF_HEADS * 2 * DIFF_QK_DIM,
    DIFF_HEADS * DIFF_V_DIM,
    D_MODEL,
    D_MODEL,
)
IN_WIDTH = sum(IN_SIZES)

kernel_name = 'hybrid_mla_diffattn_deepnorm'


def _rms_norm(x, g, eps):
    xf = x.astype(jnp.float32)
    y = xf * lax.rsqrt(jnp.mean(xf * xf, axis=-1, keepdims=True) + eps)
    return (y * g.astype(jnp.float32)).astype(x.dtype)


def _layer_norm(x, g, b):
    xf = x.astype(jnp.float32)
    mu = jnp.mean(xf, axis=-1, keepdims=True)
    xc = xf - mu
    var = jnp.mean(xc * xc, axis=-1, keepdims=True)
    y = xc * lax.rsqrt(var + LN_EPS) * g.astype(jnp.float32) + b.astype(jnp.float32)
    return y.astype(x.dtype)


def _rope_tables(seq, dim):
    inv_freq = 1.0 / (ROPE_THETA ** (jnp.arange(0, dim, 2, dtype=jnp.float32) / dim))
    ang = jnp.arange(seq, dtype=jnp.float32)[:, None] * inv_freq[None, :]
    return jnp.cos(ang), jnp.sin(ang)


def _apply_rope(x, cos, sin):
    xf = x.astype(jnp.float32)
    x1, x2 = jnp.split(xf, 2, axis=-1)
    shape = (1, cos.shape[0]) + (1,) * (x.ndim - 3) + (cos.shape[1],)
    c = cos.reshape(shape)
    s = sin.reshape(shape)
    return jnp.concatenate([x1 * c - x2 * s, x2 * c + x1 * s], axis=-1).astype(x.dtype)


def _causal_multimap_attention(q, k, v, coef, scale):
    B, S, H, M, Dk = q.shape
    Dv = v.shape[-1]
    nb = S // Q_BLOCK
    q_blocks = jnp.moveaxis(q.reshape(B, nb, Q_BLOCK, H, M, Dk), 1, 0)
    k_pos = jnp.arange(S)
    coef32 = coef.astype(jnp.float32)

    def block(args):
        qb, i = args
        s = jnp.einsum('bqhmd,bkhmd->bhmqk', qb, k,
                       preferred_element_type=jnp.float32) * scale
        q_pos = i * Q_BLOCK + jnp.arange(Q_BLOCK)
        causal = k_pos[None, :] <= q_pos[:, None]
        s = jnp.where(causal, s, -jnp.inf)
        p = jax.nn.softmax(s, axis=-1)
        w = jnp.einsum('bhmqk,m->bhqk', p, coef32)
        o = jnp.einsum('bhqk,bkhd->bqhd', w.astype(v.dtype), v,
                       preferred_element_type=jnp.float32)
        return o.astype(v.dtype)

    out = lax.map(block, (q_blocks, jnp.arange(nb)))
    return jnp.moveaxis(out, 0, 1).reshape(B, S, H, Dv)


def _mla_branch(c_q, c_kv, k_rope, q_norm_g, w_uq, kv_norm_g, w_ukv, cos, sin):
    B, S, _ = c_q.shape
    q = (_rms_norm(c_q, q_norm_g, RMS_EPS) @ w_uq).reshape(B, S, MLA_HEADS, MLA_NOPE_DIM + MLA_ROPE_DIM)
    q_nope, q_pe = jnp.split(q, [MLA_NOPE_DIM], axis=-1)
    q_pe = _apply_rope(q_pe, cos, sin)
    kv = (_rms_norm(c_kv, kv_norm_g, RMS_EPS) @ w_ukv).reshape(B, S, MLA_HEADS, MLA_NOPE_DIM + MLA_V_DIM)
    k_nope, v = jnp.split(kv, [MLA_NOPE_DIM], axis=-1)
    k_pe = _apply_rope(k_rope, cos, sin)
    k_pe = jnp.broadcast_to(k_pe[:, :, None, :], (B, S, MLA_HEADS, MLA_ROPE_DIM))
    q_full = jnp.concatenate([q_nope, q_pe], axis=-1)[:, :, :, None, :]
    k_full = jnp.concatenate([k_nope, k_pe], axis=-1)[:, :, :, None, :]
    scale = (MLA_NOPE_DIM + MLA_ROPE_DIM) ** -0.5
    out = _causal_multimap_attention(q_full, k_full, v, jnp.ones((1,), jnp.float32), scale)
    return out.reshape(B, S, MLA_HEADS * MLA_V_DIM)


def _diff_branch(dq, dk, dv, lq1, lk1, lq2, lk2, subln_g, lambda_init, cos, sin):
    B, S, _ = dq.shape
    q = _apply_rope(dq.reshape(B, S, DIFF_HEADS, 2, DIFF_QK_DIM), cos, sin)
    k = _apply_rope(dk.reshape(B, S, DIFF_HEADS, 2, DIFF_QK_DIM), cos, sin)
    v = dv.reshape(B, S, DIFF_HEADS, DIFF_V_DIM)
    lam = (jnp.exp(jnp.sum(lq1.astype(jnp.float32) * lk1.astype(jnp.float32)))
           - jnp.exp(jnp.sum(lq2.astype(jnp.float32) * lk2.astype(jnp.float32)))
           + lambda_init)
    coef = jnp.stack([jnp.ones_like(lam), -lam])
    out = _causal_multimap_attention(q, k, v, coef, DIFF_QK_DIM ** -0.5)
    out = _rms_norm(out, subln_g, SUBLN_EPS) * (1.0 - lambda_init)
    return out.reshape(B, S, DIFF_HEADS * DIFF_V_DIM)


def setup_inputs(seed: int = 0) -> dict:
    key = jax.random.key(seed)
    ks = jax.random.split(key, 20)
    f32 = jnp.float32

    def nrm(k, shape, scale):
        return jax.random.normal(k, shape, f32) * scale

    def gain(k, shape):
        return 1.0 + 0.02 * jax.random.normal(k, shape, f32)

    L = DEPTH
    mla_out = MLA_HEADS * MLA_V_DIM
    diff_out = DIFF_HEADS * DIFF_V_DIM
    return {
        'x': jax.random.normal(ks[0], (BATCH, SEQ, D_MODEL), f32),
        'w_in': nrm(ks[1], (L, D_MODEL, IN_WIDTH), D_MODEL ** -0.5),
        'mla_q_norm': gain(ks[2], (L, MLA_Q_RANK)),
        'mla_w_uq': nrm(ks[3], (L, MLA_Q_RANK, MLA_HEADS * (MLA_NOPE_DIM + MLA_ROPE_DIM)), MLA_Q_RANK ** -0.5),
        'mla_kv_norm': gain(ks[4], (L, MLA_KV_RANK)),
        'mla_w_ukv': nrm(ks[5], (L, MLA_KV_RANK, MLA_HEADS * (MLA_NOPE_DIM + MLA_V_DIM)), MLA_KV_RANK ** -0.5),
        'diff_lambda_q1': nrm(ks[6], (L, DIFF_QK_DIM), 0.1),
        'diff_lambda_k1': nrm(ks[7], (L, DIFF_QK_DIM), 0.1),
        'diff_lambda_q2': nrm(ks[8], (L, DIFF_QK_DIM), 0.1),
        'diff_lambda_k2': nrm(ks[9], (L, DIFF_QK_DIM), 0.1),
        'diff_subln': gain(ks[10], (L, DIFF_V_DIM)),
        'w_branch_a': nrm(ks[11], (L, mla_out, D_MODEL), BETA * mla_out ** -0.5),
        'w_branch_b': nrm(ks[12], (L, diff_out, D_MODEL), BETA * diff_out ** -0.5),
        'w_out': nrm(ks[13], (L, D_MODEL, D_MODEL), BETA * D_MODEL ** -0.5),
        'ln1_g': gain(ks[14], (L, D_MODEL)),
        'ln1_b': nrm(ks[15], (L, D_MODEL), 0.02),
        'w_ffn_in': nrm(ks[16], (L, D_MODEL, 2 * D_FF), D_MODEL ** -0.5),
        'w_ffn_down': nrm(ks[17], (L, D_FF, D_MODEL), BETA * D_FF ** -0.5),
        'ln2_g': gain(ks[18], (L, D_MODEL)),
        'ln2_b': nrm(ks[19], (L, D_MODEL), 0.02),
    }


def reference(x, w_in, mla_q_norm, mla_w_uq, mla_kv_norm, mla_w_ukv,
              diff_lambda_q1, diff_lambda_k1, diff_lambda_q2, diff_lambda_k2, diff_subln,
              w_branch_a, w_branch_b, w_out, ln1_g, ln1_b,
              w_ffn_in, w_ffn_down, ln2_g, ln2_b):
    S = x.shape[1]
    cos_a, sin_a = _rope_tables(S, MLA_ROPE_DIM)
    cos_b, sin_b = _rope_tables(S, DIFF_QK_DIM)
    split_at = np.cumsum(IN_SIZES)[:-1].tolist()
    h = x
    for l in range(DEPTH):
        lambda_init = 0.8 - 0.6 * math.exp(-0.3 * l)
        proj = h @ w_in[l]
        c_q, c_kv, k_rope, dq, dk, dv, g_a, g_b = jnp.split(proj, split_at, axis=-1)
        y_a = _mla_branch(c_q, c_kv, k_rope, mla_q_norm[l], mla_w_uq[l],
                          mla_kv_norm[l], mla_w_ukv[l], cos_a, sin_a) @ w_branch_a[l]
        y_b = _diff_branch(dq, dk, dv, diff_lambda_q1[l], diff_lambda_k1[l],
                           diff_lambda_q2[l], diff_lambda_k2[l], diff_subln[l],
                           lambda_init, cos_b, sin_b) @ w_branch_b[l]
        mixed = (jax.nn.sigmoid(g_a) * y_a + jax.nn.sigmoid(g_b) * y_b) @ w_out[l]
        h = _layer_norm(ALPHA * h + mixed, ln1_g[l], ln1_b[l])
        gate, up = jnp.split(h @ w_ffn_in[l], 2, axis=-1)
        ffn = (jax.nn.silu(gate) * up) @ w_ffn_down[l]
        h = _layer_norm(ALPHA * h + ffn, ln2_g[l], ln2_b[l])
    return h
```

```python
import functools
import math

import jax
import jax.numpy as jnp
from jax import lax
from jax.experimental import pallas as pl
from jax.experimental.pallas import tpu as pltpu

MLA_HEADS = 8
MLA_NOPE_DIM = 128
MLA_ROPE_DIM = 64
MLA_V_DIM = 128
MLA_Q_RANK = 512
MLA_KV_RANK = 512
DIFF_HEADS = 8
DIFF_QK_DIM = 64
DIFF_V_DIM = 128
ROPE_THETA = 10000.0
DEPTH = 1
ALPHA = (2 * DEPTH) ** 0.25
RMS_EPS = 1e-6
SUBLN_EPS = 1e-5
LN_EPS = 1e-5

LANES = 128
MLA_QK_PAD = 256
VMEM_LIMIT = 56 * 1024 * 1024

BF16 = jnp.bfloat16
F32 = jnp.float32


def _dot(a, b):
    return jnp.dot(a, b, preferred_element_type=F32)


def _dot_nt(a, b):
    return lax.dot_general(a, b, (((1,), (1,)), ((), ())), preferred_element_type=F32)


def _rms(x, g, eps):
    return x * lax.rsqrt(jnp.mean(x * x, axis=-1, keepdims=True) + eps) * g


def _layer_norm(x, g, b):
    mu = jnp.mean(x, axis=-1, keepdims=True)
    xc = x - mu
    var = jnp.mean(xc * xc, axis=-1, keepdims=True)
    return xc * lax.rsqrt(var + LN_EPS) * g + b


def _rope(x, cos, sin_up, sin_dn):
    up = pltpu.roll(x, LANES - 32, 1)
    dn = pltpu.roll(x, 32, 1)
    return x * cos + up * sin_up + dn * sin_dn


def _sigmoid(x):
    return 1.0 / (1.0 + jnp.exp(-x))


def _const_spec(shape):
    return pl.BlockSpec(shape, lambda *_: (0,) * len(shape), pipeline_mode=pl.Buffered(1))


def _params(*sem):
    return pltpu.CompilerParams(dimension_semantics=sem, vmem_limit_bytes=VMEM_LIMIT)


def _mla_proj_kernel(x_ref, w_lat_ref, gq_ref, gkv_ref, w_qn_ref, w_qp_ref, w_kn_ref, w_v_ref,
                     cos_ref, sup_ref, sdn_ref, q_ref, k_ref, v_ref, *, scale):
    xb = x_ref[...].astype(BF16)
    lat = _dot(xb, w_lat_ref[...])
    cq = _rms(lat[:, :MLA_Q_RANK], gq_ref[...], RMS_EPS).astype(BF16)
    ckv = _rms(lat[:, MLA_Q_RANK:MLA_Q_RANK + MLA_KV_RANK], gkv_ref[...], RMS_EPS).astype(BF16)
    cos, sup, sdn = cos_ref[...], sup_ref[...], sdn_ref[...]
    k_pe = _rope(lat[:, MLA_Q_RANK + MLA_KV_RANK:], cos, sup, sdn).astype(BF16)
    qn = _dot(cq, w_qn_ref[...])
    qp = _dot(cq, w_qp_ref[...])
    kn = _dot(ckv, w_kn_ref[...])
    v_ref[...] = _dot(ckv, w_v_ref[...]).astype(BF16)
    for h in range(MLA_HEADS):
        lo, hi = h * LANES, (h + 1) * LANES
        base = h * MLA_QK_PAD
        q_ref[:, base:base + LANES] = (qn[:, lo:hi] * scale).astype(BF16)
        q_ref[:, base + LANES:base + 2 * LANES] = (_rope(qp[:, lo:hi], cos, sup, sdn) * scale).astype(BF16)
        k_ref[:, base:base + LANES] = kn[:, lo:hi].astype(BF16)
        k_ref[:, base + LANES:base + 2 * LANES] = k_pe


def _diff_proj_kernel(x_ref, w_ref, cos_ref, sup_ref, sdn_ref, q_ref, k_ref, v_ref, *, scale):
    xb = x_ref[...].astype(BF16)
    n = DIFF_HEADS * 2 * DIFF_QK_DIM
    y = _dot(xb, w_ref[...])
    cos, sup, sdn = cos_ref[...], sup_ref[...], sdn_ref[...]
    for h in range(DIFF_HEADS):
        lo, hi = h * LANES, (h + 1) * LANES
        q_ref[:, lo:hi] = (_rope(y[:, lo:hi], cos, sup, sdn) * scale).astype(BF16)
        k_ref[:, lo:hi] = _rope(y[:, n + lo:n + hi], cos, sup, sdn).astype(BF16)
    v_ref[...] = y[:, 2 * n:].astype(BF16)


def _gate_proj_kernel(x_ref, w_ref, o_ref):
    xb = x_ref[...].astype(BF16)
    o_ref[...] = _sigmoid(_dot(xb, w_ref[...])).astype(BF16)


def _flash_rows(q, k_ref, v_ref, q_tile, tq, tk, n_maps):
    rows = q.shape[0]
    dv = v_ref.shape[-1]

    def step(j, carry, masked):
        m, l, acc = carry
        start = pl.multiple_of(j * tk, tk)
        k = k_ref[pl.ds(start, tk), :]
        v = v_ref[pl.ds(start, tk), :]
        s = _dot_nt(q, k)
        if masked:
            q_pos = q_tile * tq + lax.broadcasted_iota(jnp.int32, (rows, tk), 0) % tq
            k_pos = start + lax.broadcasted_iota(jnp.int32, (rows, tk), 1)
            s = jnp.where(k_pos <= q_pos, s, -jnp.inf)
        m_new = jnp.maximum(m, jnp.max(s, axis=-1, keepdims=True))
        a = jnp.exp(m - m_new)
        p = jnp.exp(s - m_new)
        l = a * l + jnp.sum(p, axis=-1, keepdims=True)
        acc = a * acc + _dot(p.astype(BF16), v)
        return m_new, l, acc

    carry = (jnp.full((rows, 1), -jnp.inf, F32), jnp.zeros((rows, 1), F32), jnp.zeros((rows, dv), F32))
    n_full = q_tile * (tq // tk)
    carry = lax.fori_loop(0, n_full, functools.partial(step, masked=False), carry)
    for d in range(tq // tk):
        carry = step(n_full + d, carry, masked=True)
    _, l, acc = carry
    return acc, l


def _mla_attn_kernel(q_ref, k_ref, v_ref, o_ref, *, tq, tk):
    acc, l = _flash_rows(q_ref[...], k_ref, v_ref, pl.program_id(2), tq, tk, 1)
    o_ref[...] = (acc / l).astype(o_ref.dtype)


def _diff_attn_kernel(q_ref, k_ref, v_ref, lq1_ref, lk1_ref, lq2_ref, lk2_ref, g_ref, o_ref,
                      *, tq, tk, lambda_init):
    q = q_ref[...]
    lane = lax.broadcasted_iota(jnp.int32, q.shape, 1)
    zero = jnp.zeros_like(q)
    q2 = jnp.concatenate([jnp.where(lane < DIFF_QK_DIM, q, zero), jnp.where(lane >= DIFF_QK_DIM, q, zero)], axis=0)
    acc, l = _flash_rows(q2, k_ref, v_ref, pl.program_id(2), tq, tk, 2)
    o = acc / l
    lam = (jnp.exp(jnp.sum(lq1_ref[...] * lk1_ref[...], axis=-1, keepdims=True))
           - jnp.exp(jnp.sum(lq2_ref[...] * lk2_ref[...], axis=-1, keepdims=True)) + lambda_init)
    o = o[:tq] - lam * o[tq:]
    o_ref[...] = (_rms(o, g_ref[...], SUBLN_EPS) * (1.0 - lambda_init)).astype(o_ref.dtype)


def _mix_out_kernel(a_ref, b_ref, sa_ref, sb_ref, x_ref, wa_ref, wb_ref, wo_ref, g_ref, beta_ref, o_ref):
    ya = _dot(a_ref[...], wa_ref[...])
    yb = _dot(b_ref[...], wb_ref[...])
    m = (sa_ref[...].astype(F32) * ya + sb_ref[...].astype(F32) * yb).astype(BF16)
    mixed = _dot(m, wo_ref[...])
    o_ref[...] = _layer_norm(ALPHA * x_ref[...] + mixed, g_ref[...], beta_ref[...])


def _ffn_kernel(h_ref, wg_ref, wu_ref, wd_ref, g_ref, beta_ref, o_ref, hb_ref, acc_ref):
    f = pl.program_id(1)

    @pl.when(f == 0)
    def _():
        hb_ref[...] = h_ref[...].astype(BF16)
        acc_ref[...] = jnp.zeros_like(acc_ref)

    hb = hb_ref[...]
    gate = _dot(hb, wg_ref[...])
    up = _dot(hb, wu_ref[...])
    act = (gate * _sigmoid(gate) * up).astype(BF16)
    acc_ref[...] += _dot(act, wd_ref[...])

    @pl.when(f == pl.num_programs(1) - 1)
    def _():
        o_ref[...] = _layer_norm(ALPHA * h_ref[...] + acc_ref[...], g_ref[...], beta_ref[...])


def _rope_tables(seq, groups):
    half = 32
    inv_freq = 1.0 / (ROPE_THETA ** (jnp.arange(0, 2 * half, 2, dtype=F32) / (2 * half)))
    ang = jnp.arange(seq, dtype=F32)[:, None] * inv_freq[None, :]
    cos, sin = jnp.cos(ang), jnp.sin(ang)
    z = jnp.zeros_like(cos)
    pad = jnp.zeros((seq, LANES - 64 * groups), F32)
    cos_t = jnp.concatenate([cos, cos] * groups + [pad], axis=-1)
    sin_up = jnp.concatenate([-sin, z] * groups + [pad], axis=-1)
    sin_dn = jnp.concatenate([z, sin] * groups + [pad], axis=-1)
    return cos_t, sin_up, sin_dn


def kernel(x, w_in, mla_q_norm, mla_w_uq, mla_kv_norm, mla_w_ukv, diff_lambda_q1, diff_lambda_k1,
           diff_lambda_q2, diff_lambda_k2, diff_subln, w_branch_a, w_branch_b, w_out, ln1_g, ln1_b,
           w_ffn_in, w_ffn_down, ln2_g, ln2_b):
    B, S, D = x.shape
    T = B * S
    H = MLA_HEADS
    d_ff = w_ffn_down.shape[1]
    lambda_init = 0.8 - 0.6 * math.exp(-0.3 * 0)
    x2 = x.reshape(T, D)

    w = w_in[0]
    n_diff = DIFF_HEADS * 2 * DIFF_QK_DIM
    o_kr = MLA_Q_RANK + MLA_KV_RANK
    o_dq = o_kr + MLA_ROPE_DIM
    o_g = o_dq + 3 * n_diff
    w_lat = jnp.concatenate(
        [w[:, :o_kr], w[:, o_kr:o_dq], jnp.zeros((D, LANES - MLA_ROPE_DIM), w.dtype)], axis=1).astype(BF16)
    w_diff = w[:, o_dq:o_g].astype(BF16)
    w_gate = w[:, o_g:].astype(BF16)
    uq = mla_w_uq[0].reshape(MLA_Q_RANK, H, MLA_NOPE_DIM + MLA_ROPE_DIM)
    w_qn = uq[:, :, :MLA_NOPE_DIM].reshape(MLA_Q_RANK, H * LANES).astype(BF16)
    w_qp = jnp.pad(uq[:, :, MLA_NOPE_DIM:], ((0, 0), (0, 0), (0, LANES - MLA_ROPE_DIM))
                   ).reshape(MLA_Q_RANK, H * LANES).astype(BF16)
    ukv = mla_w_ukv[0].reshape(MLA_KV_RANK, H, MLA_NOPE_DIM + MLA_V_DIM)
    w_kn = ukv[:, :, :MLA_NOPE_DIM].reshape(MLA_KV_RANK, H * LANES).astype(BF16)
    w_v = ukv[:, :, MLA_NOPE_DIM:].reshape(MLA_KV_RANK, H * MLA_V_DIM).astype(BF16)
    w_a = w_branch_a[0].astype(BF16)
    w_b = w_branch_b[0].astype(BF16)
    w_o = w_out[0].astype(BF16)
    w_fi = w_ffn_in[0].astype(BF16)
    w_fd = w_ffn_down[0].astype(BF16)
    row = lambda v: v.reshape(1, -1).astype(F32)

    cos_a, sup_a, sdn_a = _rope_tables(S, 1)
    cos_b, sup_b, sdn_b = _rope_tables(S, 2)

    bm = 512
    n_pos = S // bm
    tok = lambda n: pl.BlockSpec((bm, n), lambda i: (i, 0))
    pos = pl.BlockSpec((bm, LANES), lambda i: (i % n_pos, 0))
    q_a, k_a, v_a = pl.pallas_call(
        functools.partial(_mla_proj_kernel, scale=(MLA_NOPE_DIM + MLA_ROPE_DIM) ** -0.5),
        grid=(T // bm,),
        in_specs=[tok(D), _const_spec(w_lat.shape), _const_spec((1, MLA_Q_RANK)), _const_spec((1, MLA_KV_RANK)),
                  _const_spec(w_qn.shape), _const_spec(w_qp.shape), _const_spec(w_kn.shape),
                  _const_spec(w_v.shape), pos, pos, pos],
        out_specs=[tok(H * MLA_QK_PAD), tok(H * MLA_QK_PAD), tok(H * MLA_V_DIM)],
        out_shape=[jax.ShapeDtypeStruct((T, H * MLA_QK_PAD), BF16), jax.ShapeDtypeStruct((T, H * MLA_QK_PAD), BF16),
                   jax.ShapeDtypeStruct((T, H * MLA_V_DIM), BF16)],
        compiler_params=_params("parallel"), name="mla_proj",
    )(x2, w_lat, row(mla_q_norm), row(mla_kv_norm), w_qn, w_qp, w_kn, w_v, cos_a, sup_a, sdn_a)

    q_b, k_b, v_b = pl.pallas_call(
        functools.partial(_diff_proj_kernel, scale=DIFF_QK_DIM ** -0.5),
        grid=(T // bm,),
        in_specs=[tok(D), _const_spec(w_diff.shape), pos, pos, pos],
        out_specs=[tok(n_diff)] * 3,
        out_shape=[jax.ShapeDtypeStruct((T, n_diff), BF16)] * 3,
        compiler_params=_params("parallel"), name="diff_proj",
    )(x2, w_diff, cos_b, sup_b, sdn_b)

    bmg = 256
    sig = pl.pallas_call(
        _gate_proj_kernel,
        grid=(T // bmg,),
        in_specs=[pl.BlockSpec((bmg, D), lambda i: (i, 0)), _const_spec(w_gate.shape)],
        out_specs=pl.BlockSpec((bmg, 2 * D), lambda i: (i, 0)),
        out_shape=jax.ShapeDtypeStruct((T, 2 * D), BF16),
        compiler_params=_params("parallel"), name="gate_proj",
    )(x2, w_gate)

    tq = tk = 512
    head_blk = lambda rows, width, full: pl.BlockSpec(
        (None, rows, width), (lambda b, h, i: (b, 0, h)) if full else (lambda b, h, i: (b, i, h)))
    attn_a = pl.pallas_call(
        functools.partial(_mla_attn_kernel, tq=tq, tk=tk),
        grid=(B, H, S // tq),
        in_specs=[head_blk(tq, MLA_QK_PAD, False), head_blk(S, MLA_QK_PAD, True), head_blk(S, MLA_V_DIM, True)],
        out_specs=head_blk(tq, MLA_V_DIM, False),
        out_shape=jax.ShapeDtypeStruct((B, S, H * MLA_V_DIM), BF16),
        compiler_params=_params("parallel", "parallel", "parallel"), name="mla_attn",
    )(q_a.reshape(B, S, -1), k_a.reshape(B, S, -1), v_a.reshape(B, S, -1))

    lam_spec = pl.BlockSpec((1, DIFF_QK_DIM), lambda b, h, i: (0, 0))
    attn_b = pl.pallas_call(
        functools.partial(_diff_attn_kernel, tq=tq, tk=tk, lambda_init=lambda_init),
        grid=(B, DIFF_HEADS, S // tq),
        in_specs=[head_blk(tq, LANES, False), head_blk(S, LANES, True), head_blk(S, DIFF_V_DIM, True),
                  lam_spec, lam_spec, lam_spec, lam_spec, pl.BlockSpec((1, DIFF_V_DIM), lambda b, h, i: (0, 0))],
        out_specs=head_blk(tq, DIFF_V_DIM, False),
        out_shape=jax.ShapeDtypeStruct((B, S, DIFF_HEADS * DIFF_V_DIM), BF16),
        compiler_params=_params("parallel", "parallel", "parallel"), name="diff_attn",
    )(q_b.reshape(B, S, -1), k_b.reshape(B, S, -1), v_b.reshape(B, S, -1),
      row(diff_lambda_q1), row(diff_lambda_k1), row(diff_lambda_q2), row(diff_lambda_k2), row(diff_subln))

    bmo = 256
    tokm = lambda n, j=0: pl.BlockSpec((bmo, n), lambda i: (i, j))
    h1 = pl.pallas_call(
        _mix_out_kernel,
        grid=(T // bmo,),
        in_specs=[tokm(w_a.shape[0]), tokm(w_b.shape[0]), tokm(D, 0), tokm(D, 1), tokm(D),
                  _const_spec(w_a.shape), _const_spec(w_b.shape), _const_spec(w_o.shape),
                  _const_spec((1, D)), _const_spec((1, D))],
        out_specs=tokm(D),
        out_shape=jax.ShapeDtypeStruct((T, D), F32),
        compiler_params=_params("parallel"), name="mix_out",
    )(attn_a.reshape(T, -1), attn_b.reshape(T, -1), sig, sig, x2, w_a, w_b, w_o, row(ln1_g), row(ln1_b))

    bmf, tf = 512, 512
    n_f = d_ff // tf
    out = pl.pallas_call(
        _ffn_kernel,
        grid=(T // bmf, n_f),
        in_specs=[pl.BlockSpec((bmf, D), lambda i, f: (i, 0)),
                  pl.BlockSpec((D, tf), lambda i, f: (0, f)),
                  pl.BlockSpec((D, tf), lambda i, f: (0, n_f + f)),
                  pl.BlockSpec((tf, D), lambda i, f: (f, 0)),
                  pl.BlockSpec((1, D), lambda i, f: (0, 0)), pl.BlockSpec((1, D), lambda i, f: (0, 0))],
        out_specs=pl.BlockSpec((bmf, D), lambda i, f: (i, 0)),
        out_shape=jax.ShapeDtypeStruct((T, D), F32),
        scratch_shapes=[pltpu.VMEM((bmf, D), BF16), pltpu.VMEM((bmf, D), F32)],
        compiler_params=_params("parallel", "arbitrary"), name="ffn",
    )(h1, w_fi, w_fi, w_fd, row(ln2_g), row(ln2_b))

    return out.reshape(B, S, D)
```

```python
import functools
import math

import jax
import jax.numpy as jnp
from jax import lax
from jax.experimental import pallas as pl
from jax.experimental.pallas import tpu as pltpu

MLA_HEADS = 8
MLA_NOPE_DIM = 128
MLA_ROPE_DIM = 64
MLA_V_DIM = 128
MLA_Q_RANK = 512
MLA_KV_RANK = 512
DIFF_HEADS = 8
DIFF_QK_DIM = 64
DIFF_V_DIM = 128
ROPE_THETA = 10000.0
DEPTH = 1
ALPHA = (2 * DEPTH) ** 0.25
RMS_EPS = 1e-6
SUBLN_EPS = 1e-5
LN_EPS = 1e-5

LOG2_E = math.log2(math.e)
LANES = 128
MLA_QK_PAD = 256
VMEM_LIMIT = 56 * 1024 * 1024

BF16 = jnp.bfloat16
F32 = jnp.float32


def _dot(a, b):
    return jnp.dot(a, b, preferred_element_type=F32)


def _dot_nt(a, b):
    return lax.dot_general(a, b, (((1,), (1,)), ((), ())), preferred_element_type=F32)


def _rms(x, g, eps):
    return x * lax.rsqrt(jnp.mean(x * x, axis=-1, keepdims=True) + eps) * g


def _layer_norm(x, g, b):
    mu = jnp.mean(x, axis=-1, keepdims=True)
    xc = x - mu
    var = jnp.mean(xc * xc, axis=-1, keepdims=True)
    return xc * lax.rsqrt(var + LN_EPS) * g + b


def _rope(x, cos, sin_up, sin_dn):
    up = pltpu.roll(x, LANES - 32, 1)
    dn = pltpu.roll(x, 32, 1)
    return x * cos + up * sin_up + dn * sin_dn


def _sigmoid(x):
    return 1.0 / (1.0 + jnp.exp(-x))


def _const_spec(shape):
    return pl.BlockSpec(shape, lambda *_: (0,) * len(shape), pipeline_mode=pl.Buffered(1))


def _params(*sem):
    return pltpu.CompilerParams(dimension_semantics=sem, vmem_limit_bytes=VMEM_LIMIT)


def _mla_proj_kernel(x_ref, w_lat_ref, gq_ref, gkv_ref, w_qn_ref, w_qp_ref, w_kn_ref, w_vt_ref,
                     cos_ref, sup_ref, sdn_ref, q_ref, k_ref, vt_ref, *, scale):
    xb = x_ref[...].astype(BF16)
    lat = _dot(xb, w_lat_ref[...])
    cq = _rms(lat[:, :MLA_Q_RANK], gq_ref[...], RMS_EPS).astype(BF16)
    ckv = _rms(lat[:, MLA_Q_RANK:MLA_Q_RANK + MLA_KV_RANK], gkv_ref[...], RMS_EPS).astype(BF16)
    cos, sup, sdn = cos_ref[...], sup_ref[...], sdn_ref[...]
    k_pe = _rope(lat[:, MLA_Q_RANK + MLA_KV_RANK:], cos, sup, sdn).astype(BF16)
    qn = _dot(cq, w_qn_ref[...])
    qp = _dot(cq, w_qp_ref[...])
    kn = _dot(ckv, w_kn_ref[...])
    vt_ref[...] = _dot_nt(w_vt_ref[...], ckv).astype(BF16)
    for h in range(MLA_HEADS):
        lo, hi = h * LANES, (h + 1) * LANES
        base = h * MLA_QK_PAD
        q_ref[:, base:base + LANES] = (qn[:, lo:hi] * scale).astype(BF16)
        q_ref[:, base + LANES:base + 2 * LANES] = (_rope(qp[:, lo:hi], cos, sup, sdn) * scale).astype(BF16)
        k_ref[:, base:base + LANES] = kn[:, lo:hi].astype(BF16)
        k_ref[:, base + LANES:base + 2 * LANES] = k_pe


def _diff_proj_kernel(x_ref, w_ref, w_vt_ref, cos_ref, sup_ref, sdn_ref, q_ref, k_ref, vt_ref, *, scale):
    xb = x_ref[...].astype(BF16)
    n = DIFF_HEADS * 2 * DIFF_QK_DIM
    y = _dot(xb, w_ref[...])
    vt_ref[...] = _dot_nt(w_vt_ref[...], xb).astype(BF16)
    cos, sup, sdn = cos_ref[...], sup_ref[...], sdn_ref[...]
    for h in range(DIFF_HEADS):
        lo, hi = h * LANES, (h + 1) * LANES
        q_ref[:, lo:hi] = (_rope(y[:, lo:hi], cos, sup, sdn) * scale).astype(BF16)
        k_ref[:, lo:hi] = _rope(y[:, n + lo:n + hi], cos, sup, sdn).astype(BF16)


def _gate_proj_kernel(x_ref, w_ref, o_ref):
    xb = x_ref[...].astype(BF16)
    o_ref[...] = _sigmoid(_dot(xb, w_ref[...])).astype(BF16)


def _flash_cols(q, k_ref, vt_ref, scratch, q_tile, tq):
    qt_ref, s0_ref, s1_ref, p0_ref, p1_ref, m_ref, l_ref, a0_ref, a1_ref, acc_ref = scratch
    s_refs, p_refs, a_refs = (s0_ref, s1_ref), (p0_ref, p1_ref), (a0_ref, a1_ref)
    tk = tq // 2
    cols = q.shape[0]

    qt_ref[...] = q.astype(F32).T.astype(BF16)
    m_ref[...] = jnp.full(m_ref.shape, -jnp.inf, F32)
    l_ref[...] = jnp.zeros(l_ref.shape, F32)
    acc_ref[...] = jnp.zeros(acc_ref.shape, F32)
    p1_ref[...] = jnp.zeros(p1_ref.shape, BF16)
    a1_ref[...] = jnp.ones(a1_ref.shape, F32)

    def scores(c, half):
        s_refs[half][...] = _dot(k_ref[pl.ds(pl.multiple_of(c * tk, tk), tk), :], qt_ref[...])

    def softmax(half, diagonal):
        for g in range(cols // LANES):
            sl = slice(g * LANES, (g + 1) * LANES)
            s = s_refs[half][:, sl]
            if diagonal:
                k_pos = half * tk + lax.broadcasted_iota(jnp.int32, (tk, LANES), 0)
                q_pos = (g * LANES) % tq + lax.broadcasted_iota(jnp.int32, (tk, LANES), 1)
                s = jnp.where(k_pos <= q_pos, s, -jnp.inf)
            m_old = m_ref[:, sl]
            m_new = jnp.maximum(m_old, jnp.max(s, axis=0, keepdims=True))
            a = jnp.exp2(m_old - m_new)
            p = jnp.exp2(s - m_new)
            l_ref[:, sl] = a * l_ref[:, sl] + jnp.sum(p, axis=0, keepdims=True)
            m_ref[:, sl] = m_new
            a_refs[half][:, sl] = a
            p_refs[half][:, sl] = p.astype(BF16)

    def weighted_values(pair, half):
        vt = vt_ref[pair][:, half * tk:(half + 1) * tk]
        acc_ref[...] = a_refs[half][...] * acc_ref[...] + _dot(vt, p_refs[half][...])

    def pair_step(i, diagonal):
        weighted_values(jnp.maximum(i - 1, 0), 1)
        scores(2 * i + 1, 1)
        softmax(0, diagonal)
        if not diagonal:
            scores(2 * i + 2, 0)
        weighted_values(i, 0)
        softmax(1, diagonal)

    scores(0, 0)
    lax.fori_loop(0, q_tile, lambda i, c: (pair_step(i, False), c)[1], 0)
    pair_step(q_tile, True)
    weighted_values(q_tile, 1)
    return acc_ref[...], l_ref[...]


def _flash_scratch(dk, dv, cols, tq):
    tk = tq // 2
    stat = pltpu.VMEM((1, cols), F32)
    return [pltpu.VMEM((dk, cols), BF16),
            pltpu.VMEM((tk, cols), F32), pltpu.VMEM((tk, cols), F32),
            pltpu.VMEM((tk, cols), BF16), pltpu.VMEM((tk, cols), BF16),
            stat, stat, stat, stat, pltpu.VMEM((dv, cols), F32)]


def _mla_attn_kernel(q_ref, k_ref, vt_ref, o_ref, *scratch, tq):
    acc, l = _flash_cols(q_ref[...], k_ref, vt_ref, scratch, pl.program_id(2), tq)
    o_ref[...] = (acc / l).T.astype(o_ref.dtype)


def _diff_attn_kernel(q_ref, k_ref, vt_ref, lq1_ref, lk1_ref, lq2_ref, lk2_ref, g_ref, o_ref, *scratch,
                      tq, lambda_init):
    q = q_ref[...]
    lane = lax.broadcasted_iota(jnp.int32, q.shape, 1)
    zero = jnp.zeros_like(q)
    q2 = jnp.concatenate([jnp.where(lane < DIFF_QK_DIM, q, zero), jnp.where(lane >= DIFF_QK_DIM, q, zero)], axis=0)
    acc, l = _flash_cols(q2, k_ref, vt_ref, scratch, pl.program_id(2), tq)
    o = acc / l
    lam = (jnp.exp(jnp.sum(lq1_ref[...] * lk1_ref[...], axis=-1, keepdims=True))
           - jnp.exp(jnp.sum(lq2_ref[...] * lk2_ref[...], axis=-1, keepdims=True)) + lambda_init)
    o = (o[:, :tq] - lam * o[:, tq:]).T
    o_ref[...] = (_rms(o, g_ref[...], SUBLN_EPS) * (1.0 - lambda_init)).astype(o_ref.dtype)


def _mix_out_kernel(a_ref, b_ref, sa_ref, sb_ref, x_ref, wa_ref, wb_ref, wo_ref, g_ref, beta_ref, o_ref):
    ya = _dot(a_ref[...], wa_ref[...])
    yb = _dot(b_ref[...], wb_ref[...])
    m = (sa_ref[...].astype(F32) * ya + sb_ref[...].astype(F32) * yb).astype(BF16)
    mixed = _dot(m, wo_ref[...])
    o_ref[...] = _layer_norm(ALPHA * x_ref[...] + mixed, g_ref[...], beta_ref[...])


def _ffn_kernel(h_ref, wg_ref, wu_ref, wd_ref, g_ref, beta_ref, o_ref, hb_ref, acc_ref):
    f = pl.program_id(1)

    @pl.when(f == 0)
    def _():
        hb_ref[...] = h_ref[...].astype(BF16)
        acc_ref[...] = jnp.zeros_like(acc_ref)

    hb = hb_ref[...]
    gate = _dot(hb, wg_ref[...])
    up = _dot(hb, wu_ref[...])
    act = (gate * _sigmoid(gate) * up).astype(BF16)
    acc_ref[...] += _dot(act, wd_ref[...])

    @pl.when(f == pl.num_programs(1) - 1)
    def _():
        o_ref[...] = _layer_norm(ALPHA * h_ref[...] + acc_ref[...], g_ref[...], beta_ref[...])


def _rope_tables(seq, groups):
    half = 32
    inv_freq = 1.0 / (ROPE_THETA ** (jnp.arange(0, 2 * half, 2, dtype=F32) / (2 * half)))
    ang = jnp.arange(seq, dtype=F32)[:, None] * inv_freq[None, :]
    cos, sin = jnp.cos(ang), jnp.sin(ang)
    z = jnp.zeros_like(cos)
    pad = jnp.zeros((seq, LANES - 64 * groups), F32)
    cos_t = jnp.concatenate([cos, cos] * groups + [pad], axis=-1)
    sin_up = jnp.concatenate([-sin, z] * groups + [pad], axis=-1)
    sin_dn = jnp.concatenate([z, sin] * groups + [pad], axis=-1)
    return cos_t, sin_up, sin_dn


def kernel(x, w_in, mla_q_norm, mla_w_uq, mla_kv_norm, mla_w_ukv, diff_lambda_q1, diff_lambda_k1,
           diff_lambda_q2, diff_lambda_k2, diff_subln, w_branch_a, w_branch_b, w_out, ln1_g, ln1_b,
           w_ffn_in, w_ffn_down, ln2_g, ln2_b):
    B, S, D = x.shape
    T = B * S
    H = MLA_HEADS
    d_ff = w_ffn_down.shape[1]
    lambda_init = 0.8 - 0.6 * math.exp(-0.3 * 0)
    x2 = x.reshape(T, D)

    w = w_in[0]
    n_diff = DIFF_HEADS * 2 * DIFF_QK_DIM
    o_kr = MLA_Q_RANK + MLA_KV_RANK
    o_dq = o_kr + MLA_ROPE_DIM
    o_g = o_dq + 3 * n_diff
    w_lat = jnp.concatenate(
        [w[:, :o_kr], w[:, o_kr:o_dq], jnp.zeros((D, LANES - MLA_ROPE_DIM), w.dtype)], axis=1).astype(BF16)
    w_diff = w[:, o_dq:o_dq + 2 * n_diff].astype(BF16)
    w_dvt = w[:, o_dq + 2 * n_diff:o_g].T.astype(BF16)
    w_gate = w[:, o_g:].astype(BF16)
    uq = mla_w_uq[0].reshape(MLA_Q_RANK, H, MLA_NOPE_DIM + MLA_ROPE_DIM)
    w_qn = uq[:, :, :MLA_NOPE_DIM].reshape(MLA_Q_RANK, H * LANES).astype(BF16)
    w_qp = jnp.pad(uq[:, :, MLA_NOPE_DIM:], ((0, 0), (0, 0), (0, LANES - MLA_ROPE_DIM))
                   ).reshape(MLA_Q_RANK, H * LANES).astype(BF16)
    ukv = mla_w_ukv[0].reshape(MLA_KV_RANK, H, MLA_NOPE_DIM + MLA_V_DIM)
    w_kn = ukv[:, :, :MLA_NOPE_DIM].reshape(MLA_KV_RANK, H * LANES).astype(BF16)
    w_vt = ukv[:, :, MLA_NOPE_DIM:].reshape(MLA_KV_RANK, H * MLA_V_DIM).T.astype(BF16)
    w_a = w_branch_a[0].astype(BF16)
    w_b = w_branch_b[0].astype(BF16)
    w_o = w_out[0].astype(BF16)
    w_fi = w_ffn_in[0].astype(BF16)
    w_fd = w_ffn_down[0].astype(BF16)
    row = lambda v: v.reshape(1, -1).astype(F32)

    cos_a, sup_a, sdn_a = _rope_tables(S, 1)
    cos_b, sup_b, sdn_b = _rope_tables(S, 2)

    bm = 512
    n_pos = S // bm
    tok = lambda n: pl.BlockSpec((bm, n), lambda i: (i, 0))
    pos = pl.BlockSpec((bm, LANES), lambda i: (i % n_pos, 0))
    vt_spec = lambda n: pl.BlockSpec((None, n, bm), lambda i: (i, 0, 0))
    vt_shape = lambda n: jax.ShapeDtypeStruct((T // bm, n, bm), BF16)
    q_a, k_a, vt_a = pl.pallas_call(
        functools.partial(_mla_proj_kernel, scale=LOG2_E * (MLA_NOPE_DIM + MLA_ROPE_DIM) ** -0.5),
        grid=(T // bm,),
        in_specs=[tok(D), _const_spec(w_lat.shape), _const_spec((1, MLA_Q_RANK)), _const_spec((1, MLA_KV_RANK)),
                  _const_spec(w_qn.shape), _const_spec(w_qp.shape), _const_spec(w_kn.shape),
                  _const_spec(w_vt.shape), pos, pos, pos],
        out_specs=[tok(H * MLA_QK_PAD), tok(H * MLA_QK_PAD), vt_spec(H * MLA_V_DIM)],
        out_shape=[jax.ShapeDtypeStruct((T, H * MLA_QK_PAD), BF16), jax.ShapeDtypeStruct((T, H * MLA_QK_PAD), BF16),
                   vt_shape(H * MLA_V_DIM)],
        compiler_params=_params("parallel"), name="mla_proj",
    )(x2, w_lat, row(mla_q_norm), row(mla_kv_norm), w_qn, w_qp, w_kn, w_vt, cos_a, sup_a, sdn_a)

    q_b, k_b, vt_b = pl.pallas_call(
        functools.partial(_diff_proj_kernel, scale=LOG2_E * DIFF_QK_DIM ** -0.5),
        grid=(T // bm,),
        in_specs=[tok(D), _const_spec(w_diff.shape), _const_spec(w_dvt.shape), pos, pos, pos],
        out_specs=[tok(n_diff), tok(n_diff), vt_spec(n_diff)],
        out_shape=[jax.ShapeDtypeStruct((T, n_diff), BF16)] * 2 + [vt_shape(n_diff)],
        compiler_params=_params("parallel"), name="diff_proj",
    )(x2, w_diff, w_dvt, cos_b, sup_b, sdn_b)

    bmg = 256
    sig = pl.pallas_call(
        _gate_proj_kernel,
        grid=(T // bmg,),
        in_specs=[pl.BlockSpec((bmg, D), lambda i: (i, 0)), _const_spec(w_gate.shape)],
        out_specs=pl.BlockSpec((bmg, 2 * D), lambda i: (i, 0)),
        out_shape=jax.ShapeDtypeStruct((T, 2 * D), BF16),
        compiler_params=_params("parallel"), name="gate_proj",
    )(x2, w_gate)

    tq = bm
    head_blk = lambda rows, width, full: pl.BlockSpec(
        (None, rows, width), (lambda b, h, i: (b, 0, h)) if full else (lambda b, h, i: (b, i, h)))
    vt_blk = lambda dv: pl.BlockSpec((None, S // tq, None, dv, tq), lambda b, h, i: (b, 0, h, 0, 0))
    attn_a = pl.pallas_call(
        functools.partial(_mla_attn_kernel, tq=tq),
        grid=(B, H, S // tq),
        in_specs=[head_blk(tq, MLA_QK_PAD, False), head_blk(S, MLA_QK_PAD, True), vt_blk(MLA_V_DIM)],
        out_specs=head_blk(tq, MLA_V_DIM, False),
        out_shape=jax.ShapeDtypeStruct((B, S, H * MLA_V_DIM), BF16),
        scratch_shapes=_flash_scratch(MLA_QK_PAD, MLA_V_DIM, tq, tq),
        compiler_params=_params("parallel", "parallel", "parallel"), name="mla_attn",
    )(q_a.reshape(B, S, -1), k_a.reshape(B, S, -1), vt_a.reshape(B, S // tq, H, MLA_V_DIM, tq))

    lam_spec = pl.BlockSpec((1, DIFF_QK_DIM), lambda b, h, i: (0, 0))
    attn_b = pl.pallas_call(
        functools.partial(_diff_attn_kernel, tq=tq, lambda_init=lambda_init),
        grid=(B, DIFF_HEADS, S // tq),
        in_specs=[head_blk(tq, LANES, False), head_blk(S, LANES, True), vt_blk(DIFF_V_DIM),
                  lam_spec, lam_spec, lam_spec, lam_spec, pl.BlockSpec((1, DIFF_V_DIM), lambda b, h, i: (0, 0))],
        out_specs=head_blk(tq, DIFF_V_DIM, False),
        out_shape=jax.ShapeDtypeStruct((B, S, DIFF_HEADS * DIFF_V_DIM), BF16),
        scratch_shapes=_flash_scratch(LANES, DIFF_V_DIM, 2 * tq, tq),
        compiler_params=_params("parallel", "parallel", "parallel"), name="diff_attn",
    )(q_b.reshape(B, S, -1), k_b.reshape(B, S, -1), vt_b.reshape(B, S // tq, DIFF_HEADS, DIFF_V_DIM, tq),
      row(diff_lambda_q1), row(diff_lambda_k1), row(diff_lambda_q2), row(diff_lambda_k2), row(diff_subln))

    bmo = 256
    tokm = lambda n, j=0: pl.BlockSpec((bmo, n), lambda i: (i, j))
    h1 = pl.pallas_call(
        _mix_out_kernel,
        grid=(T // bmo,),
        in_specs=[tokm(w_a.shape[0]), tokm(w_b.shape[0]), tokm(D, 0), tokm(D, 1), tokm(D),
                  _const_spec(w_a.shape), _const_spec(w_b.shape), _const_spec(w_o.shape),
                  _const_spec((1, D)), _const_spec((1, D))],
        out_specs=tokm(D),
        out_shape=jax.ShapeDtypeStruct((T, D), F32),
        compiler_params=_params("parallel"), name="mix_out",
    )(attn_a.reshape(T, -1), attn_b.reshape(T, -1), sig, sig, x2, w_a, w_b, w_o, row(ln1_g), row(ln1_b))

    bmf, tf = 512, 512
    n_f = d_ff // tf
    out = pl.pallas_call(
        _ffn_kernel,
        grid=(T // bmf, n_f),
        in_specs=[pl.BlockSpec((bmf, D), lambda i, f: (i, 0)),
                  pl.BlockSpec((D, tf), lambda i, f: (0, f)),
                  pl.BlockSpec((D, tf), lambda i, f: (0, n_f + f)),
                  pl.BlockSpec((tf, D), lambda i, f: (f, 0)),
                  pl.BlockSpec((1, D), lambda i, f: (0, 0)), pl.BlockSpec((1, D), lambda i, f: (0, 0))],
        out_specs=pl.BlockSpec((bmf, D), lambda i, f: (i, 0)),
        out_shape=jax.ShapeDtypeStruct((T, D), F32),
        scratch_shapes=[pltpu.VMEM((bmf, D), BF16), pltpu.VMEM((bmf, D), F32)],
        compiler_params=_params("parallel", "arbitrary"), name="ffn",
    )(h1, w_fi, w_fi, w_fd, row(ln2_g), row(ln2_b))

    return out.reshape(B, S, D)
```

```python
import functools
import math

import jax
import jax.numpy as jnp
from jax import lax
from jax.experimental import pallas as pl
from jax.experimental.pallas import tpu as pltpu

MLA_HEADS = 8
MLA_NOPE_DIM = 128
MLA_ROPE_DIM = 64
MLA_V_DIM = 128
MLA_Q_RANK = 512
MLA_KV_RANK = 512
DIFF_HEADS = 8
DIFF_QK_DIM = 64
DIFF_V_DIM = 128
ROPE_THETA = 10000.0
DEPTH = 1
ALPHA = (2 * DEPTH) ** 0.25
RMS_EPS = 1e-6
SUBLN_EPS = 1e-5
LN_EPS = 1e-5

LOG2_E = math.log2(math.e)
LANES = 128
MLA_QK_PAD = 256
VMEM_LIMIT = 56 * 1024 * 1024

BF16 = jnp.bfloat16
F32 = jnp.float32


def _dot(a, b):
    return jnp.dot(a, b, preferred_element_type=F32)


def _dot_nt(a, b):
    return lax.dot_general(a, b, (((1,), (1,)), ((), ())), preferred_element_type=F32)


def _rms(x, g, eps):
    return x * lax.rsqrt(jnp.mean(x * x, axis=-1, keepdims=True) + eps) * g


def _layer_norm(x, g, b):
    mu = jnp.mean(x, axis=-1, keepdims=True)
    xc = x - mu
    var = jnp.mean(xc * xc, axis=-1, keepdims=True)
    return xc * lax.rsqrt(var + LN_EPS) * g + b


def _rope(x, cos, sin_up, sin_dn):
    up = pltpu.roll(x, LANES - 32, 1)
    dn = pltpu.roll(x, 32, 1)
    return x * cos + up * sin_up + dn * sin_dn


def _sigmoid(x):
    return 1.0 / (1.0 + jnp.exp(-x))


def _const_spec(shape):
    return pl.BlockSpec(shape, lambda *_: (0,) * len(shape), pipeline_mode=pl.Buffered(1))


def _params(*sem):
    return pltpu.CompilerParams(dimension_semantics=sem, vmem_limit_bytes=VMEM_LIMIT)


def _mla_proj_kernel(x_ref, w_lat_ref, gq_ref, gkv_ref, w_qn_ref, w_qp_ref, w_kn_ref, w_vt_ref,
                     cos_ref, sup_ref, sdn_ref, q_ref, k_ref, vt_ref, *, scale):
    xb = x_ref[...].astype(BF16)
    lat = _dot(xb, w_lat_ref[...])
    cq = _rms(lat[:, :MLA_Q_RANK], gq_ref[...], RMS_EPS).astype(BF16)
    ckv = _rms(lat[:, MLA_Q_RANK:MLA_Q_RANK + MLA_KV_RANK], gkv_ref[...], RMS_EPS).astype(BF16)
    cos, sup, sdn = cos_ref[...], sup_ref[...], sdn_ref[...]
    k_pe = _rope(lat[:, MLA_Q_RANK + MLA_KV_RANK:], cos, sup, sdn).astype(BF16)
    qn = _dot(cq, w_qn_ref[...])
    qp = _dot(cq, w_qp_ref[...])
    kn = _dot(ckv, w_kn_ref[...])
    vt_ref[...] = _dot_nt(w_vt_ref[...], ckv).astype(BF16)
    for h in range(MLA_HEADS):
        lo, hi = h * LANES, (h + 1) * LANES
        base = h * MLA_QK_PAD
        q_ref[:, base:base + LANES] = (qn[:, lo:hi] * scale).astype(BF16)
        q_ref[:, base + LANES:base + 2 * LANES] = (_rope(qp[:, lo:hi], cos, sup, sdn) * scale).astype(BF16)
        k_ref[:, base:base + LANES] = kn[:, lo:hi].astype(BF16)
        k_ref[:, base + LANES:base + 2 * LANES] = k_pe


def _diff_proj_kernel(x_ref, w_ref, w_vt_ref, cos_ref, sup_ref, sdn_ref, q_ref, k_ref, vt_ref, *, scale):
    xb = x_ref[...].astype(BF16)
    n = DIFF_HEADS * 2 * DIFF_QK_DIM
    y = _dot(xb, w_ref[...])
    vt_ref[...] = _dot_nt(w_vt_ref[...], xb).astype(BF16)
    cos, sup, sdn = cos_ref[...], sup_ref[...], sdn_ref[...]
    for h in range(DIFF_HEADS):
        lo, hi = h * LANES, (h + 1) * LANES
        q_ref[:, lo:hi] = (_rope(y[:, lo:hi], cos, sup, sdn) * scale).astype(BF16)
        k_ref[:, lo:hi] = _rope(y[:, n + lo:n + hi], cos, sup, sdn).astype(BF16)


def _gate_proj_kernel(x_ref, w_ref, o_ref):
    xb = x_ref[...].astype(BF16)
    o_ref[...] = _sigmoid(_dot(xb, w_ref[...])).astype(BF16)


def _flash_cols(qs, k_ref, vt_ref, scratch, q_tile, tq):
    n_heads = len(qs)
    per_head = len(scratch) // n_heads
    tk = tq // 2
    cols, dk = qs[0].shape

    class Head:
        def __init__(self, h):
            (self.qt, s0, s1, p0, p1, self.m, self.l, a0, a1, self.acc) = scratch[h * per_head:(h + 1) * per_head]
            self.s, self.p, self.a = (s0, s1), (p0, p1), (a0, a1)
            self.lanes = slice(h * dk, (h + 1) * dk)
            self.h = h

    heads = [Head(h) for h in range(n_heads)]
    for hd, q in zip(heads, qs):
        hd.qt[...] = q.astype(F32).T.astype(BF16)
        hd.m[...] = jnp.full(hd.m.shape, -jnp.inf, F32)
        hd.l[...] = jnp.zeros(hd.l.shape, F32)
        hd.acc[...] = jnp.zeros(hd.acc.shape, F32)
        hd.p[1][...] = jnp.zeros(hd.p[1].shape, BF16)
        hd.a[1][...] = jnp.ones(hd.a[1].shape, F32)

    def scores(hd, c, half):
        hd.s[half][...] = _dot(k_ref[pl.ds(pl.multiple_of(c * tk, tk), tk), hd.lanes], hd.qt[...])

    def softmax(hd, half, diagonal):
        for g in range(cols // LANES):
            sl = slice(g * LANES, (g + 1) * LANES)
            s = hd.s[half][:, sl]
            if diagonal:
                k_pos = half * tk + lax.broadcasted_iota(jnp.int32, (tk, LANES), 0)
                q_pos = (g * LANES) % tq + lax.broadcasted_iota(jnp.int32, (tk, LANES), 1)
                s = jnp.where(k_pos <= q_pos, s, -jnp.inf)
            m_old = hd.m[:, sl]
            m_new = jnp.maximum(m_old, jnp.max(s, axis=0, keepdims=True))
            a = jnp.exp2(m_old - m_new)
            p = jnp.exp2(s - m_new)
            hd.l[:, sl] = a * hd.l[:, sl] + jnp.sum(p, axis=0, keepdims=True)
            hd.m[:, sl] = m_new
            hd.a[half][:, sl] = a
            hd.p[half][:, sl] = p.astype(BF16)

    def weighted_values(hd, pair, half):
        vt = vt_ref[pair, hd.h][:, half * tk:(half + 1) * tk]
        hd.acc[...] = hd.a[half][...] * hd.acc[...] + _dot(vt, hd.p[half][...])

    def pair_step(i, diagonal):
        for hd in heads:
            weighted_values(hd, jnp.maximum(i - 1, 0), 1)
        for hd in heads:
            scores(hd, 2 * i + 1, 1)
        for hd in heads:
            softmax(hd, 0, diagonal)
        if not diagonal:
            for hd in heads:
                scores(hd, 2 * i + 2, 0)
        for hd in heads:
            weighted_values(hd, i, 0)
        for hd in heads:
            softmax(hd, 1, diagonal)

    for hd in heads:
        scores(hd, 0, 0)
    lax.fori_loop(0, q_tile, lambda i, c: (pair_step(i, False), c)[1], 0)
    pair_step(q_tile, True)
    for hd in heads:
        weighted_values(hd, q_tile, 1)
    return [(hd.acc[...], hd.l[...]) for hd in heads]


def _flash_scratch(n_heads, dk, dv, cols, tq):
    tk = tq // 2
    stat = pltpu.VMEM((1, cols), F32)
    return n_heads * [pltpu.VMEM((dk, cols), BF16),
                      pltpu.VMEM((tk, cols), F32), pltpu.VMEM((tk, cols), F32),
                      pltpu.VMEM((tk, cols), BF16), pltpu.VMEM((tk, cols), BF16),
                      stat, stat, stat, stat, pltpu.VMEM((dv, cols), F32)]


def _mla_attn_kernel(q_ref, k_ref, vt_ref, o_ref, *scratch, tq, n_heads):
    qs = [q_ref[:, h * MLA_QK_PAD:(h + 1) * MLA_QK_PAD] for h in range(n_heads)]
    for h, (acc, l) in enumerate(_flash_cols(qs, k_ref, vt_ref, scratch, pl.program_id(2), tq)):
        o_ref[:, h * MLA_V_DIM:(h + 1) * MLA_V_DIM] = (acc / l).T.astype(o_ref.dtype)


def _diff_attn_kernel(q_ref, k_ref, vt_ref, lq1_ref, lk1_ref, lq2_ref, lk2_ref, g_ref, o_ref, *scratch,
                      tq, n_heads, lambda_init):
    lane = lax.broadcasted_iota(jnp.int32, (tq, LANES), 1)
    zero = jnp.zeros((tq, LANES), BF16)
    qs = []
    for h in range(n_heads):
        q = q_ref[:, h * LANES:(h + 1) * LANES]
        qs.append(jnp.concatenate(
            [jnp.where(lane < DIFF_QK_DIM, q, zero), jnp.where(lane >= DIFF_QK_DIM, q, zero)], axis=0))
    lam = (jnp.exp(jnp.sum(lq1_ref[...] * lk1_ref[...], axis=-1, keepdims=True))
           - jnp.exp(jnp.sum(lq2_ref[...] * lk2_ref[...], axis=-1, keepdims=True)) + lambda_init)
    for h, (acc, l) in enumerate(_flash_cols(qs, k_ref, vt_ref, scratch, pl.program_id(2), tq)):
        o = acc / l
        o = (o[:, :tq] - lam * o[:, tq:]).T
        o_ref[:, h * DIFF_V_DIM:(h + 1) * DIFF_V_DIM] = (
            _rms(o, g_ref[...], SUBLN_EPS) * (1.0 - lambda_init)).astype(o_ref.dtype)


def _mix_out_kernel(a_ref, b_ref, sa_ref, sb_ref, x_ref, wa_ref, wb_ref, wo_ref, g_ref, beta_ref, o_ref):
    ya = _dot(a_ref[...], wa_ref[...])
    yb = _dot(b_ref[...], wb_ref[...])
    m = (sa_ref[...].astype(F32) * ya + sb_ref[...].astype(F32) * yb).astype(BF16)
    mixed = _dot(m, wo_ref[...])
    o_ref[...] = _layer_norm(ALPHA * x_ref[...] + mixed, g_ref[...], beta_ref[...])


def _ffn_kernel(h_ref, wg_ref, wu_ref, wd_ref, g_ref, beta_ref, o_ref, hb_ref):
    f = pl.program_id(1)

    @pl.when(f == 0)
    def _():
        h = h_ref[...]
        hb_ref[...] = h.astype(BF16)
        o_ref[...] = ALPHA * h

    hb = hb_ref[...]
    gate = _dot(hb, wg_ref[...])
    up = _dot(hb, wu_ref[...])
    act = (gate * _sigmoid(gate) * up).astype(BF16)
    o_ref[...] += _dot(act, wd_ref[...])

    @pl.when(f == pl.num_programs(1) - 1)
    def _():
        o_ref[...] = _layer_norm(o_ref[...], g_ref[...], beta_ref[...])


def _rope_tables(seq, groups):
    half = 32
    inv_freq = 1.0 / (ROPE_THETA ** (jnp.arange(0, 2 * half, 2, dtype=F32) / (2 * half)))
    ang = jnp.arange(seq, dtype=F32)[:, None] * inv_freq[None, :]
    cos, sin = jnp.cos(ang), jnp.sin(ang)
    z = jnp.zeros_like(cos)
    pad = jnp.zeros((seq, LANES - 64 * groups), F32)
    cos_t = jnp.concatenate([cos, cos] * groups + [pad], axis=-1)
    sin_up = jnp.concatenate([-sin, z] * groups + [pad], axis=-1)
    sin_dn = jnp.concatenate([z, sin] * groups + [pad], axis=-1)
    return cos_t, sin_up, sin_dn


def kernel(x, w_in, mla_q_norm, mla_w_uq, mla_kv_norm, mla_w_ukv, diff_lambda_q1, diff_lambda_k1,
           diff_lambda_q2, diff_lambda_k2, diff_subln, w_branch_a, w_branch_b, w_out, ln1_g, ln1_b,
           w_ffn_in, w_ffn_down, ln2_g, ln2_b):
    B, S, D = x.shape
    T = B * S
    H = MLA_HEADS
    d_ff = w_ffn_down.shape[1]
    lambda_init = 0.8 - 0.6 * math.exp(-0.3 * 0)
    x2 = x.reshape(T, D)

    w = w_in[0]
    n_diff = DIFF_HEADS * 2 * DIFF_QK_DIM
    o_kr = MLA_Q_RANK + MLA_KV_RANK
    o_dq = o_kr + MLA_ROPE_DIM
    o_g = o_dq + 3 * n_diff
    w_lat = jnp.concatenate(
        [w[:, :o_kr], w[:, o_kr:o_dq], jnp.zeros((D, LANES - MLA_ROPE_DIM), w.dtype)], axis=1).astype(BF16)
    w_diff = w[:, o_dq:o_dq + 2 * n_diff].astype(BF16)
    w_dvt = w[:, o_dq + 2 * n_diff:o_g].T.astype(BF16)
    w_gate = w[:, o_g:].astype(BF16)
    uq = mla_w_uq[0].reshape(MLA_Q_RANK, H, MLA_NOPE_DIM + MLA_ROPE_DIM)
    w_qn = uq[:, :, :MLA_NOPE_DIM].reshape(MLA_Q_RANK, H * LANES).astype(BF16)
    w_qp = jnp.pad(uq[:, :, MLA_NOPE_DIM:], ((0, 0), (0, 0), (0, LANES - MLA_ROPE_DIM))
                   ).reshape(MLA_Q_RANK, H * LANES).astype(BF16)
    ukv = mla_w_ukv[0].reshape(MLA_KV_RANK, H, MLA_NOPE_DIM + MLA_V_DIM)
    w_kn = ukv[:, :, :MLA_NOPE_DIM].reshape(MLA_KV_RANK, H * LANES).astype(BF16)
    w_vt = ukv[:, :, MLA_NOPE_DIM:].reshape(MLA_KV_RANK, H * MLA_V_DIM).T.astype(BF16)
    w_a = w_branch_a[0].astype(BF16)
    w_b = w_branch_b[0].astype(BF16)
    w_o = w_out[0].astype(BF16)
    w_fi = w_ffn_in[0].astype(BF16)
    w_fd = w_ffn_down[0].astype(BF16)
    row = lambda v: v.reshape(1, -1).astype(F32)

    cos_a, sup_a, sdn_a = _rope_tables(S, 1)
    cos_b, sup_b, sdn_b = _rope_tables(S, 2)

    bm = 512
    n_pos = S // bm
    tok = lambda n: pl.BlockSpec((bm, n), lambda i: (i, 0))
    pos = pl.BlockSpec((bm, LANES), lambda i: (i % n_pos, 0))
    vt_spec = lambda n: pl.BlockSpec((None, n, bm), lambda i: (i, 0, 0))
    vt_shape = lambda n: jax.ShapeDtypeStruct((T // bm, n, bm), BF16)
    q_a, k_a, vt_a = pl.pallas_call(
        functools.partial(_mla_proj_kernel, scale=LOG2_E * (MLA_NOPE_DIM + MLA_ROPE_DIM) ** -0.5),
        grid=(T // bm,),
        in_specs=[tok(D), _const_spec(w_lat.shape), _const_spec((1, MLA_Q_RANK)), _const_spec((1, MLA_KV_RANK)),
                  _const_spec(w_qn.shape), _const_spec(w_qp.shape), _const_spec(w_kn.shape),
                  _const_spec(w_vt.shape), pos, pos, pos],
        out_specs=[tok(H * MLA_QK_PAD), tok(H * MLA_QK_PAD), vt_spec(H * MLA_V_DIM)],
        out_shape=[jax.ShapeDtypeStruct((T, H * MLA_QK_PAD), BF16), jax.ShapeDtypeStruct((T, H * MLA_QK_PAD), BF16),
                   vt_shape(H * MLA_V_DIM)],
        compiler_params=_params("parallel"), name="mla_proj",
    )(x2, w_lat, row(mla_q_norm), row(mla_kv_norm), w_qn, w_qp, w_kn, w_vt, cos_a, sup_a, sdn_a)

    q_b, k_b, vt_b = pl.pallas_call(
        functools.partial(_diff_proj_kernel, scale=LOG2_E * DIFF_QK_DIM ** -0.5),
        grid=(T // bm,),
        in_specs=[tok(D), _const_spec(w_diff.shape), _const_spec(w_dvt.shape), pos, pos, pos],
        out_specs=[tok(n_diff), tok(n_diff), vt_spec(n_diff)],
        out_shape=[jax.ShapeDtypeStruct((T, n_diff), BF16)] * 2 + [vt_shape(n_diff)],
        compiler_params=_params("parallel"), name="diff_proj",
    )(x2, w_diff, w_dvt, cos_b, sup_b, sdn_b)

    bmg = 256
    sig = pl.pallas_call(
        _gate_proj_kernel,
        grid=(T // bmg,),
        in_specs=[pl.BlockSpec((bmg, D), lambda i: (i, 0)), _const_spec(w_gate.shape)],
        out_specs=pl.BlockSpec((bmg, 2 * D), lambda i: (i, 0)),
        out_shape=jax.ShapeDtypeStruct((T, 2 * D), BF16),
        compiler_params=_params("parallel"), name="gate_proj",
    )(x2, w_gate)

    tq = bm
    hp = 2
    head_blk = lambda rows, width, full: pl.BlockSpec(
        (None, rows, hp * width), (lambda b, h, i: (b, 0, h)) if full else (lambda b, h, i: (b, i, h)))
    vt_blk = lambda dv: pl.BlockSpec((None, S // tq, hp, dv, tq), lambda b, h, i: (b, 0, h, 0, 0))
    attn_a = pl.pallas_call(
        functools.partial(_mla_attn_kernel, tq=tq, n_heads=hp),
        grid=(B, H // hp, S // tq),
        in_specs=[head_blk(tq, MLA_QK_PAD, False), head_blk(S, MLA_QK_PAD, True), vt_blk(MLA_V_DIM)],
        out_specs=head_blk(tq, MLA_V_DIM, False),
        out_shape=jax.ShapeDtypeStruct((B, S, H * MLA_V_DIM), BF16),
        scratch_shapes=_flash_scratch(hp, MLA_QK_PAD, MLA_V_DIM, tq, tq),
        compiler_params=_params("parallel", "parallel", "parallel"), name="mla_attn",
    )(q_a.reshape(B, S, -1), k_a.reshape(B, S, -1), vt_a.reshape(B, S // tq, H, MLA_V_DIM, tq))

    lam_spec = pl.BlockSpec((1, DIFF_QK_DIM), lambda b, h, i: (0, 0))
    attn_b = pl.pallas_call(
        functools.partial(_diff_attn_kernel, tq=tq, n_heads=hp, lambda_init=lambda_init),
        grid=(B, DIFF_HEADS // hp, S // tq),
        in_specs=[head_blk(tq, LANES, False), head_blk(S, LANES, True), vt_blk(DIFF_V_DIM),
                  lam_spec, lam_spec, lam_spec, lam_spec, pl.BlockSpec((1, DIFF_V_DIM), lambda b, h, i: (0, 0))],
        out_specs=head_blk(tq, DIFF_V_DIM, False),
        out_shape=jax.ShapeDtypeStruct((B, S, DIFF_HEADS * DIFF_V_DIM), BF16),
        scratch_shapes=_flash_scratch(hp, LANES, DIFF_V_DIM, 2 * tq, tq),
        compiler_params=_params("parallel", "parallel", "parallel"), name="diff_attn",
    )(q_b.reshape(B, S, -1), k_b.reshape(B, S, -1), vt_b.reshape(B, S // tq, DIFF_HEADS, DIFF_V_DIM, tq),
      row(diff_lambda_q1), row(diff_lambda_k1), row(diff_lambda_q2), row(diff_lambda_k2), row(diff_subln))

    bmo = 256
    tokm = lambda n, j=0: pl.BlockSpec((bmo, n), lambda i: (i, j))
    h1 = pl.pallas_call(
        _mix_out_kernel,
        grid=(T // bmo,),
        in_specs=[tokm(w_a.shape[0]), tokm(w_b.shape[0]), tokm(D, 0), tokm(D, 1), tokm(D),
                  _const_spec(w_a.shape), _const_spec(w_b.shape), _const_spec(w_o.shape),
                  _const_spec((1, D)), _const_spec((1, D))],
        out_specs=tokm(D),
        out_shape=jax.ShapeDtypeStruct((T, D), F32),
        compiler_params=_params("parallel"), name="mix_out",
    )(attn_a.reshape(T, -1), attn_b.reshape(T, -1), sig, sig, x2, w_a, w_b, w_o, row(ln1_g), row(ln1_b))

    bmf, tf = 1024, 512
    n_f = d_ff // tf
    out = pl.pallas_call(
        _ffn_kernel,
        grid=(T // bmf, n_f),
        in_specs=[pl.BlockSpec((bmf, D), lambda i, f: (i, 0), pipeline_mode=pl.Buffered(1)),
                  pl.BlockSpec((D, tf), lambda i, f: (0, f)),
                  pl.BlockSpec((D, tf), lambda i, f: (0, n_f + f)),
                  pl.BlockSpec((tf, D), lambda i, f: (f, 0)),
                  pl.BlockSpec((1, D), lambda i, f: (0, 0)), pl.BlockSpec((1, D), lambda i, f: (0, 0))],
        out_specs=pl.BlockSpec((bmf, D), lambda i, f: (i, 0)),
        out_shape=jax.ShapeDtypeStruct((T, D), F32),
        scratch_shapes=[pltpu.VMEM((bmf, D), BF16)],
        compiler_params=_params("parallel", "arbitrary"), name="ffn",
    )(h1, w_fi, w_fi, w_fd, row(ln2_g), row(ln2_b))

    return out.reshape(B, S, D)
```

```python
import functools
import math

import jax
import jax.numpy as jnp
from jax import lax
from jax.experimental import pallas as pl
from jax.experimental.pallas import tpu as pltpu

MLA_HEADS = 8
MLA_NOPE_DIM = 128
MLA_ROPE_DIM = 64
MLA_V_DIM = 128
MLA_Q_RANK = 512
MLA_KV_RANK = 512
DIFF_HEADS = 8
DIFF_QK_DIM = 64
DIFF_V_DIM = 128
ROPE_THETA = 10000.0
DEPTH = 1
ALPHA = (2 * DEPTH) ** 0.25
RMS_EPS = 1e-6
SUBLN_EPS = 1e-5
LN_EPS = 1e-5

LOG2_E = math.log2(math.e)
LANES = 128
MLA_QK_PAD = 256
SUM_ROWS = 16
VMEM_LIMIT = 56 * 1024 * 1024

BF16 = jnp.bfloat16
F32 = jnp.float32


def _dot(a, b):
    return jnp.dot(a, b, preferred_element_type=F32)


def _dot_nt(a, b):
    return lax.dot_general(a, b, (((1,), (1,)), ((), ())), preferred_element_type=F32)


def _rms(x, g, eps):
    return x * lax.rsqrt(jnp.mean(x * x, axis=-1, keepdims=True) + eps) * g


def _layer_norm(x, g, b):
    mu = jnp.mean(x, axis=-1, keepdims=True)
    xc = x - mu
    var = jnp.mean(xc * xc, axis=-1, keepdims=True)
    return xc * lax.rsqrt(var + LN_EPS) * g + b


def _rope(x, cos, sin_up, sin_dn):
    up = pltpu.roll(x, LANES - 32, 1)
    dn = pltpu.roll(x, 32, 1)
    return x * cos + up * sin_up + dn * sin_dn


def _sigmoid(x):
    return 1.0 / (1.0 + jnp.exp(-x))


def _const_spec(shape):
    return pl.BlockSpec(shape, lambda *_: (0,) * len(shape), pipeline_mode=pl.Buffered(1))


def _params(*sem):
    return pltpu.CompilerParams(dimension_semantics=sem, vmem_limit_bytes=VMEM_LIMIT)


def _mla_proj_kernel(x_ref, w_lat_ref, gq_ref, gkv_ref, w_qn_ref, w_qp_ref, w_kn_ref, w_vt_ref,
                     cos_ref, sup_ref, sdn_ref, q_ref, k_ref, vt_ref, *, scale):
    xb = x_ref[...].astype(BF16)
    lat = _dot(xb, w_lat_ref[...])
    cq = _rms(lat[:, :MLA_Q_RANK], gq_ref[...], RMS_EPS).astype(BF16)
    ckv = _rms(lat[:, MLA_Q_RANK:MLA_Q_RANK + MLA_KV_RANK], gkv_ref[...], RMS_EPS).astype(BF16)
    cos, sup, sdn = cos_ref[...], sup_ref[...], sdn_ref[...]
    k_pe = _rope(lat[:, MLA_Q_RANK + MLA_KV_RANK:], cos, sup, sdn).astype(BF16)
    qn = _dot(cq, w_qn_ref[...])
    qp = _dot(cq, w_qp_ref[...])
    kn = _dot(ckv, w_kn_ref[...])
    vt_ref[...] = _dot_nt(w_vt_ref[...], ckv).astype(BF16)
    for h in range(MLA_HEADS):
        lo, hi = h * LANES, (h + 1) * LANES
        base = h * MLA_QK_PAD
        q_ref[:, base:base + LANES] = (qn[:, lo:hi] * scale).astype(BF16)
        q_ref[:, base + LANES:base + 2 * LANES] = (_rope(qp[:, lo:hi], cos, sup, sdn) * scale).astype(BF16)
        k_ref[:, base:base + LANES] = kn[:, lo:hi].astype(BF16)
        k_ref[:, base + LANES:base + 2 * LANES] = k_pe


def _diff_proj_kernel(x_ref, w_ref, w_vt_ref, cos_ref, sup_ref, sdn_ref, q_ref, k_ref, vt_ref, *, scale):
    xb = x_ref[...].astype(BF16)
    n = DIFF_HEADS * 2 * DIFF_QK_DIM
    y = _dot(xb, w_ref[...])
    vt_ref[...] = _dot_nt(w_vt_ref[...], xb).astype(BF16)
    cos, sup, sdn = cos_ref[...], sup_ref[...], sdn_ref[...]
    for h in range(DIFF_HEADS):
        lo, hi = h * LANES, (h + 1) * LANES
        q_ref[:, lo:hi] = (_rope(y[:, lo:hi], cos, sup, sdn) * scale).astype(BF16)
        k_ref[:, lo:hi] = _rope(y[:, n + lo:n + hi], cos, sup, sdn).astype(BF16)


def _gate_proj_kernel(x_ref, w_ref, o_ref):
    xb = x_ref[...].astype(BF16)
    o_ref[...] = _sigmoid(_dot(xb, w_ref[...])).astype(BF16)


def _flash_cols(qs, k_ref, vt_ref, scratch, q_tile, tq):
    n_heads = len(qs)
    per_head = len(scratch) // n_heads
    tk = tq // 2
    cols, dk = qs[0].shape
    dv = vt_ref.shape[2]
    ones = jnp.ones((SUM_ROWS, tk), BF16)

    class Head:
        def __init__(self, h):
            (self.qt, s0, s1, p0, p1, self.m, a0, a1, self.acc) = scratch[h * per_head:(h + 1) * per_head]
            self.s, self.p, self.a = (s0, s1), (p0, p1), (a0, a1)
            self.lanes = slice(h * dk, (h + 1) * dk)
            self.h = h

    heads = [Head(h) for h in range(n_heads)]
    for hd, q in zip(heads, qs):
        hd.qt[...] = q.astype(F32).T.astype(BF16)
        hd.m[...] = jnp.full(hd.m.shape, -jnp.inf, F32)
        hd.acc[...] = jnp.zeros(hd.acc.shape, F32)
        hd.p[1][...] = jnp.zeros(hd.p[1].shape, BF16)
        hd.a[1][...] = jnp.ones(hd.a[1].shape, F32)

    def scores(hd, c, half):
        hd.s[half][...] = _dot(k_ref[pl.ds(pl.multiple_of(c * tk, tk), tk), hd.lanes], hd.qt[...])

    def softmax(hd, half, diagonal):
        for g in range(cols // LANES):
            sl = slice(g * LANES, (g + 1) * LANES)
            s = hd.s[half][:, sl]
            if diagonal:
                k_pos = half * tk + lax.broadcasted_iota(jnp.int32, (tk, LANES), 0)
                q_pos = (g * LANES) % tq + lax.broadcasted_iota(jnp.int32, (tk, LANES), 1)
                s = jnp.where(k_pos <= q_pos, s, -jnp.inf)
            m_old = hd.m[:, sl]
            m_new = jnp.maximum(m_old, jnp.max(s, axis=0, keepdims=True))
            a = jnp.exp2(m_old - m_new)
            p = jnp.exp2(s - m_new)
            hd.m[:, sl] = m_new
            hd.a[half][:, sl] = a
            hd.p[half][:, sl] = p.astype(BF16)

    def weighted_values(hd, pair, half):
        vt = jnp.concatenate([vt_ref[pair, hd.h][:, half * tk:(half + 1) * tk], ones], axis=0)
        hd.acc[...] = hd.a[half][...] * hd.acc[...] + _dot(vt, hd.p[half][...])

    def pair_step(i, diagonal):
        for hd in heads:
            weighted_values(hd, jnp.maximum(i - 1, 0), 1)
        for hd in heads:
            scores(hd, 2 * i + 1, 1)
        for hd in heads:
            softmax(hd, 0, diagonal)
        if not diagonal:
            for hd in heads:
                scores(hd, 2 * i + 2, 0)
        for hd in heads:
            weighted_values(hd, i, 0)
        for hd in heads:
            softmax(hd, 1, diagonal)

    for hd in heads:
        scores(hd, 0, 0)
    lax.fori_loop(0, q_tile, lambda i, c: (pair_step(i, False), c)[1], 0)
    pair_step(q_tile, True)
    for hd in heads:
        weighted_values(hd, q_tile, 1)
    return [(hd.acc[:dv, :], hd.acc[dv:dv + 1, :]) for hd in heads]


def _flash_scratch(n_heads, dk, dv, cols, tq):
    tk = tq // 2
    stat = pltpu.VMEM((1, cols), F32)
    return n_heads * [pltpu.VMEM((dk, cols), BF16),
                      pltpu.VMEM((tk, cols), F32), pltpu.VMEM((tk, cols), F32),
                      pltpu.VMEM((tk, cols), BF16), pltpu.VMEM((tk, cols), BF16),
                      stat, stat, stat, pltpu.VMEM((dv + SUM_ROWS, cols), F32)]


def _mla_attn_kernel(q_ref, k_ref, vt_ref, o_ref, *scratch, tq, n_heads):
    qs = [q_ref[:, h * MLA_QK_PAD:(h + 1) * MLA_QK_PAD] for h in range(n_heads)]
    for h, (acc, l) in enumerate(_flash_cols(qs, k_ref, vt_ref, scratch, pl.program_id(2), tq)):
        o_ref[:, h * MLA_V_DIM:(h + 1) * MLA_V_DIM] = (acc / l).T.astype(o_ref.dtype)


def _diff_attn_kernel(q_ref, k_ref, vt_ref, lq1_ref, lk1_ref, lq2_ref, lk2_ref, g_ref, o_ref, *scratch,
                      tq, n_heads, lambda_init):
    lane = lax.broadcasted_iota(jnp.int32, (tq, LANES), 1)
    zero = jnp.zeros((tq, LANES), BF16)
    qs = []
    for h in range(n_heads):
        q = q_ref[:, h * LANES:(h + 1) * LANES]
        qs.append(jnp.concatenate(
            [jnp.where(lane < DIFF_QK_DIM, q, zero), jnp.where(lane >= DIFF_QK_DIM, q, zero)], axis=0))
    lam = (jnp.exp(jnp.sum(lq1_ref[...] * lk1_ref[...], axis=-1, keepdims=True))
           - jnp.exp(jnp.sum(lq2_ref[...] * lk2_ref[...], axis=-1, keepdims=True)) + lambda_init)
    for h, (acc, l) in enumerate(_flash_cols(qs, k_ref, vt_ref, scratch, pl.program_id(2), tq)):
        o = acc / l
        o = (o[:, :tq] - lam * o[:, tq:]).T
        o_ref[:, h * DIFF_V_DIM:(h + 1) * DIFF_V_DIM] = (
            _rms(o, g_ref[...], SUBLN_EPS) * (1.0 - lambda_init)).astype(o_ref.dtype)


def _mix_out_kernel(a_ref, b_ref, sa_ref, sb_ref, x_ref, wa_ref, wb_ref, wo_ref, g_ref, beta_ref, o_ref):
    ya = _dot(a_ref[...], wa_ref[...])
    yb = _dot(b_ref[...], wb_ref[...])
    m = (sa_ref[...].astype(F32) * ya + sb_ref[...].astype(F32) * yb).astype(BF16)
    mixed = _dot(m, wo_ref[...])
    o_ref[...] = _layer_norm(ALPHA * x_ref[...] + mixed, g_ref[...], beta_ref[...])


def _ffn_kernel(h_ref, wg_ref, wu_ref, wd_ref, g_ref, beta_ref, o_ref, hb_ref, acc_ref):
    f = pl.program_id(1)

    @pl.when(f == 0)
    def _():
        hb_ref[...] = h_ref[...].astype(BF16)
        acc_ref[...] = jnp.zeros_like(acc_ref)

    hb = hb_ref[...]
    gate = _dot(hb, wg_ref[...])
    up = _dot(hb, wu_ref[...])
    act = (gate * _sigmoid(gate) * up).astype(BF16)
    acc_ref[...] += _dot(act, wd_ref[...])

    @pl.when(f == pl.num_programs(1) - 1)
    def _():
        o_ref[...] = _layer_norm(ALPHA * h_ref[...] + acc_ref[...], g_ref[...], beta_ref[...])


def _rope_tables(seq, groups):
    half = 32
    inv_freq = 1.0 / (ROPE_THETA ** (jnp.arange(0, 2 * half, 2, dtype=F32) / (2 * half)))
    ang = jnp.arange(seq, dtype=F32)[:, None] * inv_freq[None, :]
    cos, sin = jnp.cos(ang), jnp.sin(ang)
    z = jnp.zeros_like(cos)
    pad = jnp.zeros((seq, LANES - 64 * groups), F32)
    cos_t = jnp.concatenate([cos, cos] * groups + [pad], axis=-1)
    sin_up = jnp.concatenate([-sin, z] * groups + [pad], axis=-1)
    sin_dn = jnp.concatenate([z, sin] * groups + [pad], axis=-1)
    return cos_t, sin_up, sin_dn


def kernel(x, w_in, mla_q_norm, mla_w_uq, mla_kv_norm, mla_w_ukv, diff_lambda_q1, diff_lambda_k1,
           diff_lambda_q2, diff_lambda_k2, diff_subln, w_branch_a, w_branch_b, w_out, ln1_g, ln1_b,
           w_ffn_in, w_ffn_down, ln2_g, ln2_b):
    B, S, D = x.shape
    T = B * S
    H = MLA_HEADS
    d_ff = w_ffn_down.shape[1]
    lambda_init = 0.8 - 0.6 * math.exp(-0.3 * 0)
    x2 = x.reshape(T, D)

    w = w_in[0]
    n_diff = DIFF_HEADS * 2 * DIFF_QK_DIM
    o_kr = MLA_Q_RANK + MLA_KV_RANK
    o_dq = o_kr + MLA_ROPE_DIM
    o_g = o_dq + 3 * n_diff
    w_lat = jnp.concatenate(
        [w[:, :o_kr], w[:, o_kr:o_dq], jnp.zeros((D, LANES - MLA_ROPE_DIM), w.dtype)], axis=1).astype(BF16)
    w_diff = w[:, o_dq:o_dq + 2 * n_diff].astype(BF16)
    w_dvt = w[:, o_dq + 2 * n_diff:o_g].T.astype(BF16)
    w_gate = w[:, o_g:].astype(BF16)
    uq = mla_w_uq[0].reshape(MLA_Q_RANK, H, MLA_NOPE_DIM + MLA_ROPE_DIM)
    w_qn = uq[:, :, :MLA_NOPE_DIM].reshape(MLA_Q_RANK, H * LANES).astype(BF16)
    w_qp = jnp.pad(uq[:, :, MLA_NOPE_DIM:], ((0, 0), (0, 0), (0, LANES - MLA_ROPE_DIM))
                   ).reshape(MLA_Q_RANK, H * LANES).astype(BF16)
    ukv = mla_w_ukv[0].reshape(MLA_KV_RANK, H, MLA_NOPE_DIM + MLA_V_DIM)
    w_kn = ukv[:, :, :MLA_NOPE_DIM].reshape(MLA_KV_RANK, H * LANES).astype(BF16)
    w_vt = ukv[:, :, MLA_NOPE_DIM:].reshape(MLA_KV_RANK, H * MLA_V_DIM).T.astype(BF16)
    w_a = w_branch_a[0].astype(BF16)
    w_b = w_branch_b[0].astype(BF16)
    w_o = w_out[0].astype(BF16)
    w_fi = w_ffn_in[0].astype(BF16)
    w_fd = w_ffn_down[0].astype(BF16)
    row = lambda v: v.reshape(1, -1).astype(F32)

    cos_a, sup_a, sdn_a = _rope_tables(S, 1)
    cos_b, sup_b, sdn_b = _rope_tables(S, 2)

    bm = 512
    n_pos = S // bm
    tok = lambda n: pl.BlockSpec((bm, n), lambda i: (i, 0))
    pos = pl.BlockSpec((bm, LANES), lambda i: (i % n_pos, 0))
    vt_spec = lambda n: pl.BlockSpec((None, n, bm), lambda i: (i, 0, 0))
    vt_shape = lambda n: jax.ShapeDtypeStruct((T // bm, n, bm), BF16)
    q_a, k_a, vt_a = pl.pallas_call(
        functools.partial(_mla_proj_kernel, scale=LOG2_E * (MLA_NOPE_DIM + MLA_ROPE_DIM) ** -0.5),
        grid=(T // bm,),
        in_specs=[tok(D), _const_spec(w_lat.shape), _const_spec((1, MLA_Q_RANK)), _const_spec((1, MLA_KV_RANK)),
                  _const_spec(w_qn.shape), _const_spec(w_qp.shape), _const_spec(w_kn.shape),
                  _const_spec(w_vt.shape), pos, pos, pos],
        out_specs=[tok(H * MLA_QK_PAD), tok(H * MLA_QK_PAD), vt_spec(H * MLA_V_DIM)],
        out_shape=[jax.ShapeDtypeStruct((T, H * MLA_QK_PAD), BF16), jax.ShapeDtypeStruct((T, H * MLA_QK_PAD), BF16),
                   vt_shape(H * MLA_V_DIM)],
        compiler_params=_params("parallel"), name="mla_proj",
    )(x2, w_lat, row(mla_q_norm), row(mla_kv_norm), w_qn, w_qp, w_kn, w_vt, cos_a, sup_a, sdn_a)

    q_b, k_b, vt_b = pl.pallas_call(
        functools.partial(_diff_proj_kernel, scale=LOG2_E * DIFF_QK_DIM ** -0.5),
        grid=(T // bm,),
        in_specs=[tok(D), _const_spec(w_diff.shape), _const_spec(w_dvt.shape), pos, pos, pos],
        out_specs=[tok(n_diff), tok(n_diff), vt_spec(n_diff)],
        out_shape=[jax.ShapeDtypeStruct((T, n_diff), BF16)] * 2 + [vt_shape(n_diff)],
        compiler_params=_params("parallel"), name="diff_proj",
    )(x2, w_diff, w_dvt, cos_b, sup_b, sdn_b)

    bmg = 256
    sig = pl.pallas_call(
        _gate_proj_kernel,
        grid=(T // bmg,),
        in_specs=[pl.BlockSpec((bmg, D), lambda i: (i, 0)), _const_spec(w_gate.shape)],
        out_specs=pl.BlockSpec((bmg, 2 * D), lambda i: (i, 0)),
        out_shape=jax.ShapeDtypeStruct((T, 2 * D), BF16),
        compiler_params=_params("parallel"), name="gate_proj",
    )(x2, w_gate)

    tq = bm
    hp = 4
    head_blk = lambda rows, width, full: pl.BlockSpec(
        (None, rows, hp * width), (lambda b, h, i: (b, 0, h)) if full else (lambda b, h, i: (b, i, h)))
    vt_blk = lambda dv: pl.BlockSpec((None, S // tq, hp, dv, tq), lambda b, h, i: (b, 0, h, 0, 0))
    attn_a = pl.pallas_call(
        functools.partial(_mla_attn_kernel, tq=tq, n_heads=hp),
        grid=(B, H // hp, S // tq),
        in_specs=[head_blk(tq, MLA_QK_PAD, False), head_blk(S, MLA_QK_PAD, True), vt_blk(MLA_V_DIM)],
        out_specs=head_blk(tq, MLA_V_DIM, False),
        out_shape=jax.ShapeDtypeStruct((B, S, H * MLA_V_DIM), BF16),
        scratch_shapes=_flash_scratch(hp, MLA_QK_PAD, MLA_V_DIM, tq, tq),
        compiler_params=_params("parallel", "parallel", "parallel"), name="mla_attn",
    )(q_a.reshape(B, S, -1), k_a.reshape(B, S, -1), vt_a.reshape(B, S // tq, H, MLA_V_DIM, tq))

    lam_spec = pl.BlockSpec((1, DIFF_QK_DIM), lambda b, h, i: (0, 0))
    attn_b = pl.pallas_call(
        functools.partial(_diff_attn_kernel, tq=tq, n_heads=hp, lambda_init=lambda_init),
        grid=(B, DIFF_HEADS // hp, S // tq),
        in_specs=[head_blk(tq, LANES, False), head_blk(S, LANES, True), vt_blk(DIFF_V_DIM),
                  lam_spec, lam_spec, lam_spec, lam_spec, pl.BlockSpec((1, DIFF_V_DIM), lambda b, h, i: (0, 0))],
        out_specs=head_blk(tq, DIFF_V_DIM, False),
        out_shape=jax.ShapeDtypeStruct((B, S, DIFF_HEADS * DIFF_V_DIM), BF16),
        scratch_shapes=_flash_scratch(hp, LANES, DIFF_V_DIM, 2 * tq, tq),
        compiler_params=_params("parallel", "parallel", "parallel"), name="diff_attn",
    )(q_b.reshape(B, S, -1), k_b.reshape(B, S, -1), vt_b.reshape(B, S // tq, DIFF_HEADS, DIFF_V_DIM, tq),
      row(diff_lambda_q1), row(diff_lambda_k1), row(diff_lambda_q2), row(diff_lambda_k2), row(diff_subln))

    bmo = 256
    tokm = lambda n, j=0: pl.BlockSpec((bmo, n), lambda i: (i, j))
    h1 = pl.pallas_call(
        _mix_out_kernel,
        grid=(T // bmo,),
        in_specs=[tokm(w_a.shape[0]), tokm(w_b.shape[0]), tokm(D, 0), tokm(D, 1), tokm(D),
                  _const_spec(w_a.shape), _const_spec(w_b.shape), _const_spec(w_o.shape),
                  _const_spec((1, D)), _const_spec((1, D))],
        out_specs=tokm(D),
        out_shape=jax.ShapeDtypeStruct((T, D), F32),
        compiler_params=_params("parallel"), name="mix_out",
    )(attn_a.reshape(T, -1), attn_b.reshape(T, -1), sig, sig, x2, w_a, w_b, w_o, row(ln1_g), row(ln1_b))

    bmf, tf = 512, 512
    n_f = d_ff // tf
    out = pl.pallas_call(
        _ffn_kernel,
        grid=(T // bmf, n_f),
        in_specs=[pl.BlockSpec((bmf, D), lambda i, f: (i, 0)),
                  pl.BlockSpec((D, tf), lambda i, f: (0, f)),
                  pl.BlockSpec((D, tf), lambda i, f: (0, n_f + f)),
                  pl.BlockSpec((tf, D), lambda i, f: (f, 0)),
                  pl.BlockSpec((1, D), lambda i, f: (0, 0)), pl.BlockSpec((1, D), lambda i, f: (0, 0))],
        out_specs=pl.BlockSpec((bmf, D), lambda i, f: (i, 0)),
        out_shape=jax.ShapeDtypeStruct((T, D), F32),
        scratch_shapes=[pltpu.VMEM((bmf, D), BF16), pltpu.VMEM((bmf, D), F32)],
        compiler_params=_params("parallel", "arbitrary"), name="ffn",
    )(h1, w_fi, w_fi, w_fd, row(ln2_g), row(ln2_b))

    return out.reshape(B, S, D)
```

```python
import functools
import math

import jax
import jax.numpy as jnp
from jax import lax
from jax.experimental import pallas as pl
from jax.experimental.pallas import tpu as pltpu

MLA_HEADS = 8
MLA_NOPE_DIM = 128
MLA_ROPE_DIM = 64
MLA_V_DIM = 128
MLA_Q_RANK = 512
MLA_KV_RANK = 512
DIFF_HEADS = 8
DIFF_QK_DIM = 64
DIFF_V_DIM = 128
ROPE_THETA = 10000.0
DEPTH = 1
ALPHA = (2 * DEPTH) ** 0.25
RMS_EPS = 1e-6
SUBLN_EPS = 1e-5
LN_EPS = 1e-5

LOG2_E = math.log2(math.e)
LANES = 128
MLA_QK_PAD = 256
SUM_ROWS = 16
VMEM_LIMIT = 56 * 1024 * 1024

BF16 = jnp.bfloat16
F32 = jnp.float32


def _dot(a, b):
    return jnp.dot(a, b, preferred_element_type=F32)


def _dot_nt(a, b):
    return lax.dot_general(a, b, (((1,), (1,)), ((), ())), preferred_element_type=F32)


def _rms(x, g, eps):
    return x * lax.rsqrt(jnp.mean(x * x, axis=-1, keepdims=True) + eps) * g


def _layer_norm(x, g, b):
    mu = jnp.mean(x, axis=-1, keepdims=True)
    xc = x - mu
    var = jnp.mean(xc * xc, axis=-1, keepdims=True)
    return xc * lax.rsqrt(var + LN_EPS) * g + b


def _rope(x, cos, sin_up, sin_dn):
    up = pltpu.roll(x, LANES - 32, 1)
    dn = pltpu.roll(x, 32, 1)
    return x * cos + up * sin_up + dn * sin_dn


def _sigmoid(x):
    return 1.0 / (1.0 + jnp.exp(-x))


def _const_spec(shape):
    return pl.BlockSpec(shape, lambda *_: (0,) * len(shape), pipeline_mode=pl.Buffered(1))


def _params(*sem):
    return pltpu.CompilerParams(dimension_semantics=sem, vmem_limit_bytes=VMEM_LIMIT)


def _mla_proj_kernel(x_ref, w_lat_ref, gq_ref, gkv_ref, w_qn_ref, w_qp_ref, w_kn_ref, w_vt_ref,
                     cos_ref, sup_ref, sdn_ref, q_ref, k_ref, vt_ref, *, scale):
    xb = x_ref[...].astype(BF16)
    lat = _dot(xb, w_lat_ref[...])
    cq = _rms(lat[:, :MLA_Q_RANK], gq_ref[...], RMS_EPS).astype(BF16)
    ckv = _rms(lat[:, MLA_Q_RANK:MLA_Q_RANK + MLA_KV_RANK], gkv_ref[...], RMS_EPS).astype(BF16)
    cos, sup, sdn = cos_ref[...], sup_ref[...], sdn_ref[...]
    k_pe = _rope(lat[:, MLA_Q_RANK + MLA_KV_RANK:], cos, sup, sdn).astype(BF16)
    qn = _dot(cq, w_qn_ref[...])
    qp = _dot(cq, w_qp_ref[...])
    kn = _dot(ckv, w_kn_ref[...])
    vt_ref[...] = _dot_nt(w_vt_ref[...], ckv).astype(BF16)
    for h in range(MLA_HEADS):
        lo, hi = h * LANES, (h + 1) * LANES
        base = h * MLA_QK_PAD
        q_ref[:, base:base + LANES] = (qn[:, lo:hi] * scale).astype(BF16)
        q_ref[:, base + LANES:base + 2 * LANES] = (_rope(qp[:, lo:hi], cos, sup, sdn) * scale).astype(BF16)
        k_ref[:, base:base + LANES] = kn[:, lo:hi].astype(BF16)
        k_ref[:, base + LANES:base + 2 * LANES] = k_pe


def _diff_proj_kernel(x_ref, w_ref, w_vt_ref, cos_ref, sup_ref, sdn_ref, q_ref, k_ref, vt_ref, *, scale):
    xb = x_ref[...].astype(BF16)
    n = DIFF_HEADS * 2 * DIFF_QK_DIM
    y = _dot(xb, w_ref[...])
    vt_ref[...] = _dot_nt(w_vt_ref[...], xb).astype(BF16)
    cos, sup, sdn = cos_ref[...], sup_ref[...], sdn_ref[...]
    for h in range(DIFF_HEADS):
        lo, hi = h * LANES, (h + 1) * LANES
        q_ref[:, lo:hi] = (_rope(y[:, lo:hi], cos, sup, sdn) * scale).astype(BF16)
        k_ref[:, lo:hi] = _rope(y[:, n + lo:n + hi], cos, sup, sdn).astype(BF16)


def _gate_proj_kernel(x_ref, w_ref, o_ref):
    xb = x_ref[...].astype(BF16)
    o_ref[...] = _sigmoid(_dot(xb, w_ref[...])).astype(BF16)


def _flash_cols(qs, k_ref, vt_ref, scratch, q_tile, tq):
    n_heads = len(qs)
    per_head = len(scratch) // n_heads
    tk = tq // 2
    cols, dk = qs[0].shape
    dv = vt_ref.shape[2]
    ones = jnp.ones((SUM_ROWS, tk), BF16)

    class Head:
        def __init__(self, h):
            (self.qt, s0, s1, p0, p1, self.m, a0, a1, self.acc) = scratch[h * per_head:(h + 1) * per_head]
            self.s, self.p, self.a = (s0, s1), (p0, p1), (a0, a1)
            self.lanes = slice(h * dk, (h + 1) * dk)
            self.h = h

    heads = [Head(h) for h in range(n_heads)]
    for hd, q in zip(heads, qs):
        hd.qt[...] = q.astype(F32).T.astype(BF16)
        hd.m[...] = jnp.full(hd.m.shape, -jnp.inf, F32)
        hd.acc[...] = jnp.zeros(hd.acc.shape, F32)
        hd.p[1][...] = jnp.zeros(hd.p[1].shape, BF16)
        hd.a[1][...] = jnp.ones(hd.a[1].shape, F32)

    def scores(hd, c, half):
        s = _dot(k_ref[pl.ds(pl.multiple_of(c * tk, tk), tk), hd.lanes], hd.qt[...])
        for g in range(cols // LANES):
            hd.s[half][g] = s[:, g * LANES:(g + 1) * LANES]

    def softmax(hd, half, diagonal):
        for g in range(cols // LANES):
            sl = slice(g * LANES, (g + 1) * LANES)
            s = hd.s[half][g]
            if diagonal:
                k_pos = half * tk + lax.broadcasted_iota(jnp.int32, (tk, LANES), 0)
                q_pos = (g * LANES) % tq + lax.broadcasted_iota(jnp.int32, (tk, LANES), 1)
                s = jnp.where(k_pos <= q_pos, s, -jnp.inf)
            m_old = hd.m[:, sl]
            m_new = jnp.maximum(m_old, jnp.max(s, axis=0, keepdims=True))
            a = jnp.exp2(m_old - m_new)
            p = jnp.exp2(s - m_new)
            hd.m[:, sl] = m_new
            hd.a[half][:, sl] = a
            hd.p[half][g] = p.astype(BF16)

    def weighted_values(hd, pair, half):
        vt = jnp.concatenate([vt_ref[pair, hd.h][:, half * tk:(half + 1) * tk], ones], axis=0)
        p = jnp.concatenate([hd.p[half][g] for g in range(cols // LANES)], axis=1)
        hd.acc[...] = hd.a[half][...] * hd.acc[...] + _dot(vt, p)

    def pair_step(i, diagonal):
        for hd in heads:
            weighted_values(hd, jnp.maximum(i - 1, 0), 1)
        for hd in heads:
            scores(hd, 2 * i + 1, 1)
        for hd in heads:
            softmax(hd, 0, diagonal)
        if not diagonal:
            for hd in heads:
                scores(hd, 2 * i + 2, 0)
        for hd in heads:
            weighted_values(hd, i, 0)
        for hd in heads:
            softmax(hd, 1, diagonal)

    for hd in heads:
        scores(hd, 0, 0)
    lax.fori_loop(0, q_tile, lambda i, c: (pair_step(i, False), c)[1], 0)
    pair_step(q_tile, True)
    for hd in heads:
        weighted_values(hd, q_tile, 1)
    return [(hd.acc[:dv, :], hd.acc[dv:dv + 1, :]) for hd in heads]


def _flash_scratch(n_heads, dk, dv, cols, tq):
    tk = tq // 2
    stat = pltpu.VMEM((1, cols), F32)
    group_major = lambda dtype: pltpu.VMEM((cols // LANES, tk, LANES), dtype)
    return n_heads * [pltpu.VMEM((dk, cols), BF16),
                      group_major(F32), group_major(F32), group_major(BF16), group_major(BF16),
                      stat, stat, stat, pltpu.VMEM((dv + SUM_ROWS, cols), F32)]


def _mla_attn_kernel(q_ref, k_ref, vt_ref, o_ref, *scratch, tq, n_heads):
    qs = [q_ref[:, h * MLA_QK_PAD:(h + 1) * MLA_QK_PAD] for h in range(n_heads)]
    for h, (acc, l) in enumerate(_flash_cols(qs, k_ref, vt_ref, scratch, pl.program_id(2), tq)):
        o_ref[:, h * MLA_V_DIM:(h + 1) * MLA_V_DIM] = (acc / l).T.astype(o_ref.dtype)


def _diff_attn_kernel(q_ref, k_ref, vt_ref, lq1_ref, lk1_ref, lq2_ref, lk2_ref, g_ref, o_ref, *scratch,
                      tq, n_heads, lambda_init):
    lane = lax.broadcasted_iota(jnp.int32, (tq, LANES), 1)
    zero = jnp.zeros((tq, LANES), BF16)
    qs = []
    for h in range(n_heads):
        q = q_ref[:, h * LANES:(h + 1) * LANES]
        qs.append(jnp.concatenate(
            [jnp.where(lane < DIFF_QK_DIM, q, zero), jnp.where(lane >= DIFF_QK_DIM, q, zero)], axis=0))
    lam = (jnp.exp(jnp.sum(lq1_ref[...] * lk1_ref[...], axis=-1, keepdims=True))
           - jnp.exp(jnp.sum(lq2_ref[...] * lk2_ref[...], axis=-1, keepdims=True)) + lambda_init)
    for h, (acc, l) in enumerate(_flash_cols(qs, k_ref, vt_ref, scratch, pl.program_id(2), tq)):
        o = acc / l
        o = (o[:, :tq] - lam * o[:, tq:]).T
        o_ref[:, h * DIFF_V_DIM:(h + 1) * DIFF_V_DIM] = (
            _rms(o, g_ref[...], SUBLN_EPS) * (1.0 - lambda_init)).astype(o_ref.dtype)


def _mix_out_kernel(a_ref, b_ref, sa_ref, sb_ref, x_ref, wa_ref, wb_ref, wo_ref, g_ref, beta_ref, o_ref):
    ya = _dot(a_ref[...], wa_ref[...])
    yb = _dot(b_ref[...], wb_ref[...])
    m = (sa_ref[...].astype(F32) * ya + sb_ref[...].astype(F32) * yb).astype(BF16)
    mixed = _dot(m, wo_ref[...])
    o_ref[...] = _layer_norm(ALPHA * x_ref[...] + mixed, g_ref[...], beta_ref[...])


def _ffn_kernel(h_ref, wg_ref, wu_ref, wd_ref, g_ref, beta_ref, o_ref, hb_ref, acc_ref):
    f = pl.program_id(1)

    @pl.when(f == 0)
    def _():
        hb_ref[...] = h_ref[...].astype(BF16)
        acc_ref[...] = jnp.zeros_like(acc_ref)

    hb = hb_ref[...]
    gate = _dot(hb, wg_ref[...])
    up = _dot(hb, wu_ref[...])
    act = (gate * _sigmoid(gate) * up).astype(BF16)
    acc_ref[...] += _dot(act, wd_ref[...])

    @pl.when(f == pl.num_programs(1) - 1)
    def _():
        o_ref[...] = _layer_norm(ALPHA * h_ref[...] + acc_ref[...], g_ref[...], beta_ref[...])


def _rope_tables(seq, groups):
    half = 32
    inv_freq = 1.0 / (ROPE_THETA ** (jnp.arange(0, 2 * half, 2, dtype=F32) / (2 * half)))
    ang = jnp.arange(seq, dtype=F32)[:, None] * inv_freq[None, :]
    cos, sin = jnp.cos(ang), jnp.sin(ang)
    z = jnp.zeros_like(cos)
    pad = jnp.zeros((seq, LANES - 64 * groups), F32)
    cos_t = jnp.concatenate([cos, cos] * groups + [pad], axis=-1)
    sin_up = jnp.concatenate([-sin, z] * groups + [pad], axis=-1)
    sin_dn = jnp.concatenate([z, sin] * groups + [pad], axis=-1)
    return cos_t, sin_up, sin_dn


def kernel(x, w_in, mla_q_norm, mla_w_uq, mla_kv_norm, mla_w_ukv, diff_lambda_q1, diff_lambda_k1,
           diff_lambda_q2, diff_lambda_k2, diff_subln, w_branch_a, w_branch_b, w_out, ln1_g, ln1_b,
           w_ffn_in, w_ffn_down, ln2_g, ln2_b):
    B, S, D = x.shape
    T = B * S
    H = MLA_HEADS
    d_ff = w_ffn_down.shape[1]
    lambda_init = 0.8 - 0.6 * math.exp(-0.3 * 0)
    x2 = x.reshape(T, D)

    w = w_in[0]
    n_diff = DIFF_HEADS * 2 * DIFF_QK_DIM
    o_kr = MLA_Q_RANK + MLA_KV_RANK
    o_dq = o_kr + MLA_ROPE_DIM
    o_g = o_dq + 3 * n_diff
    w_lat = jnp.concatenate(
        [w[:, :o_kr], w[:, o_kr:o_dq], jnp.zeros((D, LANES - MLA_ROPE_DIM), w.dtype)], axis=1).astype(BF16)
    w_diff = w[:, o_dq:o_dq + 2 * n_diff].astype(BF16)
    w_dvt = w[:, o_dq + 2 * n_diff:o_g].T.astype(BF16)
    w_gate = w[:, o_g:].astype(BF16)
    uq = mla_w_uq[0].reshape(MLA_Q_RANK, H, MLA_NOPE_DIM + MLA_ROPE_DIM)
    w_qn = uq[:, :, :MLA_NOPE_DIM].reshape(MLA_Q_RANK, H * LANES).astype(BF16)
    w_qp = jnp.pad(uq[:, :, MLA_NOPE_DIM:], ((0, 0), (0, 0), (0, LANES - MLA_ROPE_DIM))
                   ).reshape(MLA_Q_RANK, H * LANES).astype(BF16)
    ukv = mla_w_ukv[0].reshape(MLA_KV_RANK, H, MLA_NOPE_DIM + MLA_V_DIM)
    w_kn = ukv[:, :, :MLA_NOPE_DIM].reshape(MLA_KV_RANK, H * LANES).astype(BF16)
    w_vt = ukv[:, :, MLA_NOPE_DIM:].reshape(MLA_KV_RANK, H * MLA_V_DIM).T.astype(BF16)
    w_a = w_branch_a[0].astype(BF16)
    w_b = w_branch_b[0].astype(BF16)
    w_o = w_out[0].astype(BF16)
    w_fi = w_ffn_in[0].astype(BF16)
    w_fd = w_ffn_down[0].astype(BF16)
    row = lambda v: v.reshape(1, -1).astype(F32)

    cos_a, sup_a, sdn_a = _rope_tables(S, 1)
    cos_b, sup_b, sdn_b = _rope_tables(S, 2)

    bm = 512
    n_pos = S // bm
    tok = lambda n: pl.BlockSpec((bm, n), lambda i: (i, 0))
    pos = pl.BlockSpec((bm, LANES), lambda i: (i % n_pos, 0))
    vt_spec = lambda n: pl.BlockSpec((None, n, bm), lambda i: (i, 0, 0))
    vt_shape = lambda n: jax.ShapeDtypeStruct((T // bm, n, bm), BF16)
    q_a, k_a, vt_a = pl.pallas_call(
        functools.partial(_mla_proj_kernel, scale=LOG2_E * (MLA_NOPE_DIM + MLA_ROPE_DIM) ** -0.5),
        grid=(T // bm,),
        in_specs=[tok(D), _const_spec(w_lat.shape), _const_spec((1, MLA_Q_RANK)), _const_spec((1, MLA_KV_RANK)),
                  _const_spec(w_qn.shape), _const_spec(w_qp.shape), _const_spec(w_kn.shape),
                  _const_spec(w_vt.shape), pos, pos, pos],
        out_specs=[tok(H * MLA_QK_PAD), tok(H * MLA_QK_PAD), vt_spec(H * MLA_V_DIM)],
        out_shape=[jax.ShapeDtypeStruct((T, H * MLA_QK_PAD), BF16), jax.ShapeDtypeStruct((T, H * MLA_QK_PAD), BF16),
                   vt_shape(H * MLA_V_DIM)],
        compiler_params=_params("parallel"), name="mla_proj",
    )(x2, w_lat, row(mla_q_norm), row(mla_kv_norm), w_qn, w_qp, w_kn, w_vt, cos_a, sup_a, sdn_a)

    q_b, k_b, vt_b = pl.pallas_call(
        functools.partial(_diff_proj_kernel, scale=LOG2_E * DIFF_QK_DIM ** -0.5),
        grid=(T // bm,),
        in_specs=[tok(D), _const_spec(w_diff.shape), _const_spec(w_dvt.shape), pos, pos, pos],
        out_specs=[tok(n_diff), tok(n_diff), vt_spec(n_diff)],
        out_shape=[jax.ShapeDtypeStruct((T, n_diff), BF16)] * 2 + [vt_shape(n_diff)],
        compiler_params=_params("parallel"), name="diff_proj",
    )(x2, w_diff, w_dvt, cos_b, sup_b, sdn_b)

    bmg = 256
    sig = pl.pallas_call(
        _gate_proj_kernel,
        grid=(T // bmg,),
        in_specs=[pl.BlockSpec((bmg, D), lambda i: (i, 0)), _const_spec(w_gate.shape)],
        out_specs=pl.BlockSpec((bmg, 2 * D), lambda i: (i, 0)),
        out_shape=jax.ShapeDtypeStruct((T, 2 * D), BF16),
        compiler_params=_params("parallel"), name="gate_proj",
    )(x2, w_gate)

    tq = bm
    hp = 4
    head_blk = lambda rows, width, full: pl.BlockSpec(
        (None, rows, hp * width), (lambda b, h, i: (b, 0, h)) if full else (lambda b, h, i: (b, i, h)))
    vt_blk = lambda dv: pl.BlockSpec((None, S // tq, hp, dv, tq), lambda b, h, i: (b, 0, h, 0, 0))
    attn_a = pl.pallas_call(
        functools.partial(_mla_attn_kernel, tq=tq, n_heads=hp),
        grid=(B, H // hp, S // tq),
        in_specs=[head_blk(tq, MLA_QK_PAD, False), head_blk(S, MLA_QK_PAD, True), vt_blk(MLA_V_DIM)],
        out_specs=head_blk(tq, MLA_V_DIM, False),
        out_shape=jax.ShapeDtypeStruct((B, S, H * MLA_V_DIM), BF16),
        scratch_shapes=_flash_scratch(hp, MLA_QK_PAD, MLA_V_DIM, tq, tq),
        compiler_params=_params("parallel", "parallel", "parallel"), name="mla_attn",
    )(q_a.reshape(B, S, -1), k_a.reshape(B, S, -1), vt_a.reshape(B, S // tq, H, MLA_V_DIM, tq))

    lam_spec = pl.BlockSpec((1, DIFF_QK_DIM), lambda b, h, i: (0, 0))
    attn_b = pl.pallas_call(
        functools.partial(_diff_attn_kernel, tq=tq, n_heads=hp, lambda_init=lambda_init),
        grid=(B, DIFF_HEADS // hp, S // tq),
        in_specs=[head_blk(tq, LANES, False), head_blk(S, LANES, True), vt_blk(DIFF_V_DIM),
                  lam_spec, lam_spec, lam_spec, lam_spec, pl.BlockSpec((1, DIFF_V_DIM), lambda b, h, i: (0, 0))],
        out_specs=head_blk(tq, DIFF_V_DIM, False),
        out_shape=jax.ShapeDtypeStruct((B, S, DIFF_HEADS * DIFF_V_DIM), BF16),
        scratch_shapes=_flash_scratch(hp, LANES, DIFF_V_DIM, 2 * tq, tq),
        compiler_params=_params("parallel", "parallel", "parallel"), name="diff_attn",
    )(q_b.reshape(B, S, -1), k_b.reshape(B, S, -1), vt_b.reshape(B, S // tq, DIFF_HEADS, DIFF_V_DIM, tq),
      row(diff_lambda_q1), row(diff_lambda_k1), row(diff_lambda_q2), row(diff_lambda_k2), row(diff_subln))

    bmo = 256
    tokm = lambda n, j=0: pl.BlockSpec((bmo, n), lambda i: (i, j))
    h1 = pl.pallas_call(
        _mix_out_kernel,
        grid=(T // bmo,),
        in_specs=[tokm(w_a.shape[0]), tokm(w_b.shape[0]), tokm(D, 0), tokm(D, 1), tokm(D),
                  _const_spec(w_a.shape), _const_spec(w_b.shape), _const_spec(w_o.shape),
                  _const_spec((1, D)), _const_spec((1, D))],
        out_specs=tokm(D),
        out_shape=jax.ShapeDtypeStruct((T, D), F32),
        compiler_params=_params("parallel"), name="mix_out",
    )(attn_a.reshape(T, -1), attn_b.reshape(T, -1), sig, sig, x2, w_a, w_b, w_o, row(ln1_g), row(ln1_b))

    bmf, tf = 512, 512
    n_f = d_ff // tf
    out = pl.pallas_call(
        _ffn_kernel,
        grid=(T // bmf, n_f),
        in_specs=[pl.BlockSpec((bmf, D), lambda i, f: (i, 0)),
                  pl.BlockSpec((D, tf), lambda i, f: (0, f)),
                  pl.BlockSpec((D, tf), lambda i, f: (0, n_f + f)),
                  pl.BlockSpec((tf, D), lambda i, f: (f, 0)),
                  pl.BlockSpec((1, D), lambda i, f: (0, 0)), pl.BlockSpec((1, D), lambda i, f: (0, 0))],
        out_specs=pl.BlockSpec((bmf, D), lambda i, f: (i, 0)),
        out_shape=jax.ShapeDtypeStruct((T, D), F32),
        scratch_shapes=[pltpu.VMEM((bmf, D), BF16), pltpu.VMEM((bmf, D), F32)],
        compiler_params=_params("parallel", "arbitrary"), name="ffn",
    )(h1, w_fi, w_fi, w_fd, row(ln2_g), row(ln2_b))

    return out.reshape(B, S, D)
```

```python
import functools
import math

import jax
import jax.numpy as jnp
from jax import lax
from jax.experimental import pallas as pl
from jax.experimental.pallas import tpu as pltpu

MLA_HEADS = 8
MLA_NOPE_DIM = 128
MLA_ROPE_DIM = 64
MLA_V_DIM = 128
MLA_Q_RANK = 512
MLA_KV_RANK = 512
DIFF_HEADS = 8
DIFF_QK_DIM = 64
DIFF_V_DIM = 128
ROPE_THETA = 10000.0
DEPTH = 1
ALPHA = (2 * DEPTH) ** 0.25
RMS_EPS = 1e-6
SUBLN_EPS = 1e-5
LN_EPS = 1e-5

LOG2_E = math.log2(math.e)
LANES = 128
MLA_QK_PAD = 256
SUM_ROWS = 16
KV_CHUNK = 256
VMEM_LIMIT = 56 * 1024 * 1024

BF16 = jnp.bfloat16
F32 = jnp.float32


def _dot(a, b):
    return jnp.dot(a, b, preferred_element_type=F32)


def _dot_nt(a, b):
    return lax.dot_general(a, b, (((1,), (1,)), ((), ())), preferred_element_type=F32)


def _rms(x, g, eps):
    return x * lax.rsqrt(jnp.mean(x * x, axis=-1, keepdims=True) + eps) * g


def _layer_norm(x, g, b):
    mu = jnp.mean(x, axis=-1, keepdims=True)
    xc = x - mu
    var = jnp.mean(xc * xc, axis=-1, keepdims=True)
    return xc * lax.rsqrt(var + LN_EPS) * g + b


def _rope(x, cos, sin_up, sin_dn):
    up = pltpu.roll(x, LANES - 32, 1)
    dn = pltpu.roll(x, 32, 1)
    return x * cos + up * sin_up + dn * sin_dn


def _sigmoid(x):
    return 1.0 / (1.0 + jnp.exp(-x))


def _const_spec(shape):
    return pl.BlockSpec(shape, lambda *_: (0,) * len(shape), pipeline_mode=pl.Buffered(1))


def _params(*sem):
    return pltpu.CompilerParams(dimension_semantics=sem, vmem_limit_bytes=VMEM_LIMIT)


def _store_chunks(vt_ref, vt):
    for c in range(vt_ref.shape[0]):
        vt_ref[c] = vt[:, c * KV_CHUNK:(c + 1) * KV_CHUNK]


def _mla_proj_kernel(x_ref, w_lat_ref, gq_ref, gkv_ref, w_qn_ref, w_qp_ref, w_kn_ref, w_vt_ref,
                     cos_ref, sup_ref, sdn_ref, q_ref, k_ref, vt_ref, *, scale):
    xb = x_ref[...].astype(BF16)
    lat = _dot(xb, w_lat_ref[...])
    cq = _rms(lat[:, :MLA_Q_RANK], gq_ref[...], RMS_EPS).astype(BF16)
    ckv = _rms(lat[:, MLA_Q_RANK:MLA_Q_RANK + MLA_KV_RANK], gkv_ref[...], RMS_EPS).astype(BF16)
    cos, sup, sdn = cos_ref[...], sup_ref[...], sdn_ref[...]
    k_pe = _rope(lat[:, MLA_Q_RANK + MLA_KV_RANK:], cos, sup, sdn).astype(BF16)
    qn = _dot(cq, w_qn_ref[...])
    qp = _dot(cq, w_qp_ref[...])
    kn = _dot(ckv, w_kn_ref[...])
    _store_chunks(vt_ref, _dot_nt(w_vt_ref[...], ckv).astype(BF16))
    for h in range(MLA_HEADS):
        lo, hi = h * LANES, (h + 1) * LANES
        base = h * MLA_QK_PAD
        q_ref[:, base:base + LANES] = (qn[:, lo:hi] * scale).astype(BF16)
        q_ref[:, base + LANES:base + 2 * LANES] = (_rope(qp[:, lo:hi], cos, sup, sdn) * scale).astype(BF16)
        k_ref[h, :, :LANES] = kn[:, lo:hi].astype(BF16)
        k_ref[h, :, LANES:] = k_pe


def _diff_proj_kernel(x_ref, w_ref, w_vt_ref, cos_ref, sup_ref, sdn_ref, q_ref, k_ref, vt_ref, *, scale):
    xb = x_ref[...].astype(BF16)
    n = DIFF_HEADS * 2 * DIFF_QK_DIM
    y = _dot(xb, w_ref[...])
    _store_chunks(vt_ref, _dot_nt(w_vt_ref[...], xb).astype(BF16))
    cos, sup, sdn = cos_ref[...], sup_ref[...], sdn_ref[...]
    for h in range(DIFF_HEADS):
        lo, hi = h * LANES, (h + 1) * LANES
        q_ref[:, lo:hi] = (_rope(y[:, lo:hi], cos, sup, sdn) * scale).astype(BF16)
        k_ref[h] = _rope(y[:, n + lo:n + hi], cos, sup, sdn).astype(BF16)


def _gate_proj_kernel(x_ref, w_ref, o_ref):
    xb = x_ref[...].astype(BF16)
    o_ref[...] = _sigmoid(_dot(xb, w_ref[...])).astype(BF16)


def _flash_cols(qs, k_ref, vt_ref, scratch, q_tile, tq):
    n_heads = len(qs)
    per_head = len(scratch) // n_heads
    tk = KV_CHUNK
    assert tq == 2 * tk
    cols, dk = qs[0].shape
    dv = vt_ref.shape[2]
    groups = range(cols // LANES)
    ones = jnp.ones((SUM_ROWS, tk), BF16)

    class Head:
        def __init__(self, h):
            (self.qt, s0, s1, p0, p1, self.m, a0, a1, self.acc) = scratch[h * per_head:(h + 1) * per_head]
            self.s, self.p, self.a = (s0, s1), (p0, p1), (a0, a1)
            self.h = h

    heads = [Head(h) for h in range(n_heads)]
    for hd, q in zip(heads, qs):
        qt = q.astype(F32).T.astype(BF16)
        for g in groups:
            hd.qt[g] = qt[:, g * LANES:(g + 1) * LANES]
        hd.m[...] = jnp.full(hd.m.shape, -jnp.inf, F32)
        hd.acc[...] = jnp.zeros(hd.acc.shape, F32)
        hd.p[1][...] = jnp.zeros(hd.p[1].shape, BF16)
        hd.a[1][...] = jnp.ones(hd.a[1].shape, F32)

    def scores(hd, c, half):
        qt = jnp.concatenate([hd.qt[g] for g in groups], axis=1)
        s = _dot(k_ref[hd.h, pl.ds(pl.multiple_of(c * tk, tk), tk), :], qt)
        for g in groups:
            hd.s[half][g] = s[:, g * LANES:(g + 1) * LANES]

    def softmax(hd, half, diagonal):
        for g in groups:
            sl = slice(g * LANES, (g + 1) * LANES)
            s = hd.s[half][g]
            if diagonal:
                k_pos = half * tk + lax.broadcasted_iota(jnp.int32, (tk, LANES), 0)
                q_pos = (g * LANES) % tq + lax.broadcasted_iota(jnp.int32, (tk, LANES), 1)
                s = jnp.where(k_pos <= q_pos, s, -jnp.inf)
            m_old = hd.m[:, sl]
            m_new = jnp.maximum(m_old, jnp.max(s, axis=0, keepdims=True))
            a = jnp.exp2(m_old - m_new)
            p = jnp.exp2(s - m_new)
            hd.m[:, sl] = m_new
            hd.a[half][:, sl] = a
            hd.p[half][g] = p.astype(BF16)

    def weighted_values(hd, pair, half):
        vt = jnp.concatenate([vt_ref[2 * pair + half, hd.h], ones], axis=0)
        p = jnp.concatenate([hd.p[half][g] for g in groups], axis=1)
        hd.acc[...] = hd.a[half][...] * hd.acc[...] + _dot(vt, p)

    def pair_step(i, diagonal):
        for hd in heads:
            weighted_values(hd, jnp.maximum(i - 1, 0), 1)
        for hd in heads:
            scores(hd, 2 * i + 1, 1)
        for hd in heads:
            softmax(hd, 0, diagonal)
        if not diagonal:
            for hd in heads:
                scores(hd, 2 * i + 2, 0)
        for hd in heads:
            weighted_values(hd, i, 0)
        for hd in heads:
            softmax(hd, 1, diagonal)

    for hd in heads:
        scores(hd, 0, 0)
    lax.fori_loop(0, q_tile, lambda i, c: (pair_step(i, False), c)[1], 0)
    pair_step(q_tile, True)
    for hd in heads:
        weighted_values(hd, q_tile, 1)
    return [(hd.acc[:dv, :], hd.acc[dv:dv + 1, :]) for hd in heads]


def _flash_scratch(n_heads, dk, dv, cols, tq):
    tk = KV_CHUNK
    stat = pltpu.VMEM((1, cols), F32)
    group_major = lambda dtype: pltpu.VMEM((cols // LANES, tk, LANES), dtype)
    return n_heads * [pltpu.VMEM((cols // LANES, dk, LANES), BF16),
                      group_major(F32), group_major(F32), group_major(BF16), group_major(BF16),
                      stat, stat, stat, pltpu.VMEM((dv + SUM_ROWS, cols), F32)]


def _mla_attn_kernel(q_ref, k_ref, vt_ref, o_ref, *scratch, tq, n_heads):
    qs = [q_ref[:, h * MLA_QK_PAD:(h + 1) * MLA_QK_PAD] for h in range(n_heads)]
    for h, (acc, l) in enumerate(_flash_cols(qs, k_ref, vt_ref, scratch, pl.program_id(2), tq)):
        o_ref[:, h * MLA_V_DIM:(h + 1) * MLA_V_DIM] = (acc / l).T.astype(o_ref.dtype)


def _diff_attn_kernel(q_ref, k_ref, vt_ref, lq1_ref, lk1_ref, lq2_ref, lk2_ref, g_ref, o_ref, *scratch,
                      tq, n_heads, lambda_init):
    lane = lax.broadcasted_iota(jnp.int32, (tq, LANES), 1)
    zero = jnp.zeros((tq, LANES), BF16)
    qs = []
    for h in range(n_heads):
        q = q_ref[:, h * LANES:(h + 1) * LANES]
        qs.append(jnp.concatenate(
            [jnp.where(lane < DIFF_QK_DIM, q, zero), jnp.where(lane >= DIFF_QK_DIM, q, zero)], axis=0))
    lam = (jnp.exp(jnp.sum(lq1_ref[...] * lk1_ref[...], axis=-1, keepdims=True))
           - jnp.exp(jnp.sum(lq2_ref[...] * lk2_ref[...], axis=-1, keepdims=True)) + lambda_init)
    for h, (acc, l) in enumerate(_flash_cols(qs, k_ref, vt_ref, scratch, pl.program_id(2), tq)):
        o = acc / l
        o = (o[:, :tq] - lam * o[:, tq:]).T
        o_ref[:, h * DIFF_V_DIM:(h + 1) * DIFF_V_DIM] = (
            _rms(o, g_ref[...], SUBLN_EPS) * (1.0 - lambda_init)).astype(o_ref.dtype)


def _mix_out_kernel(a_ref, b_ref, sa_ref, sb_ref, x_ref, wa_ref, wb_ref, wo_ref, g_ref, beta_ref, o_ref):
    ya = _dot(a_ref[...], wa_ref[...])
    yb = _dot(b_ref[...], wb_ref[...])
    m = (sa_ref[...].astype(F32) * ya + sb_ref[...].astype(F32) * yb).astype(BF16)
    mixed = _dot(m, wo_ref[...])
    o_ref[...] = _layer_norm(ALPHA * x_ref[...] + mixed, g_ref[...], beta_ref[...])


def _ffn_kernel(h_ref, wg_ref, wu_ref, wd_ref, g_ref, beta_ref, o_ref, hb_ref, acc_ref):
    f = pl.program_id(1)

    @pl.when(f == 0)
    def _():
        hb_ref[...] = h_ref[...].astype(BF16)
        acc_ref[...] = jnp.zeros_like(acc_ref)

    hb = hb_ref[...]
    gate = _dot(hb, wg_ref[...])
    up = _dot(hb, wu_ref[...])
    act = (gate * _sigmoid(gate) * up).astype(BF16)
    acc_ref[...] += _dot(act, wd_ref[...])

    @pl.when(f == pl.num_programs(1) - 1)
    def _():
        o_ref[...] = _layer_norm(ALPHA * h_ref[...] + acc_ref[...], g_ref[...], beta_ref[...])


def _rope_tables(seq, groups):
    half = 32
    inv_freq = 1.0 / (ROPE_THETA ** (jnp.arange(0, 2 * half, 2, dtype=F32) / (2 * half)))
    ang = jnp.arange(seq, dtype=F32)[:, None] * inv_freq[None, :]
    cos, sin = jnp.cos(ang), jnp.sin(ang)
    z = jnp.zeros_like(cos)
    pad = jnp.zeros((seq, LANES - 64 * groups), F32)
    cos_t = jnp.concatenate([cos, cos] * groups + [pad], axis=-1)
    sin_up = jnp.concatenate([-sin, z] * groups + [pad], axis=-1)
    sin_dn = jnp.concatenate([z, sin] * groups + [pad], axis=-1)
    return cos_t, sin_up, sin_dn


def kernel(x, w_in, mla_q_norm, mla_w_uq, mla_kv_norm, mla_w_ukv, diff_lambda_q1, diff_lambda_k1,
           diff_lambda_q2, diff_lambda_k2, diff_subln, w_branch_a, w_branch_b, w_out, ln1_g, ln1_b,
           w_ffn_in, w_ffn_down, ln2_g, ln2_b):
    B, S, D = x.shape
    T = B * S
    H = MLA_HEADS
    d_ff = w_ffn_down.shape[1]
    lambda_init = 0.8 - 0.6 * math.exp(-0.3 * 0)
    x2 = x.reshape(T, D)

    w = w_in[0]
    n_diff = DIFF_HEADS * 2 * DIFF_QK_DIM
    o_kr = MLA_Q_RANK + MLA_KV_RANK
    o_dq = o_kr + MLA_ROPE_DIM
    o_g = o_dq + 3 * n_diff
    w_lat = jnp.concatenate(
        [w[:, :o_kr], w[:, o_kr:o_dq], jnp.zeros((D, LANES - MLA_ROPE_DIM), w.dtype)], axis=1).astype(BF16)
    w_diff = w[:, o_dq:o_dq + 2 * n_diff].astype(BF16)
    w_dvt = w[:, o_dq + 2 * n_diff:o_g].T.astype(BF16)
    w_gate = w[:, o_g:].astype(BF16)
    uq = mla_w_uq[0].reshape(MLA_Q_RANK, H, MLA_NOPE_DIM + MLA_ROPE_DIM)
    w_qn = uq[:, :, :MLA_NOPE_DIM].reshape(MLA_Q_RANK, H * LANES).astype(BF16)
    w_qp = jnp.pad(uq[:, :, MLA_NOPE_DIM:], ((0, 0), (0, 0), (0, LANES - MLA_ROPE_DIM))
                   ).reshape(MLA_Q_RANK, H * LANES).astype(BF16)
    ukv = mla_w_ukv[0].reshape(MLA_KV_RANK, H, MLA_NOPE_DIM + MLA_V_DIM)
    w_kn = ukv[:, :, :MLA_NOPE_DIM].reshape(MLA_KV_RANK, H * LANES).astype(BF16)
    w_vt = ukv[:, :, MLA_NOPE_DIM:].reshape(MLA_KV_RANK, H * MLA_V_DIM).T.astype(BF16)
    w_a = w_branch_a[0].astype(BF16)
    w_b = w_branch_b[0].astype(BF16)
    w_o = w_out[0].astype(BF16)
    w_fi = w_ffn_in[0].astype(BF16)
    w_fd = w_ffn_down[0].astype(BF16)
    row = lambda v: v.reshape(1, -1).astype(F32)

    cos_a, sup_a, sdn_a = _rope_tables(S, 1)
    cos_b, sup_b, sdn_b = _rope_tables(S, 2)

    bm = 512
    n_pos = S // bm
    tok = lambda n: pl.BlockSpec((bm, n), lambda i: (i, 0))
    pos = pl.BlockSpec((bm, LANES), lambda i: (i % n_pos, 0))
    vt_spec = lambda n: pl.BlockSpec((bm // KV_CHUNK, n, KV_CHUNK), lambda i: (i, 0, 0))
    vt_shape = lambda n: jax.ShapeDtypeStruct((T // KV_CHUNK, n, KV_CHUNK), BF16)
    k_spec = lambda heads, w: pl.BlockSpec((None, heads, bm, w), lambda i: (i // n_pos, 0, i % n_pos, 0))
    k_shape = lambda heads, w: jax.ShapeDtypeStruct((B, heads, S, w), BF16)
    q_a, k_a, vt_a = pl.pallas_call(
        functools.partial(_mla_proj_kernel, scale=LOG2_E * (MLA_NOPE_DIM + MLA_ROPE_DIM) ** -0.5),
        grid=(T // bm,),
        in_specs=[tok(D), _const_spec(w_lat.shape), _const_spec((1, MLA_Q_RANK)), _const_spec((1, MLA_KV_RANK)),
                  _const_spec(w_qn.shape), _const_spec(w_qp.shape), _const_spec(w_kn.shape),
                  _const_spec(w_vt.shape), pos, pos, pos],
        out_specs=[tok(H * MLA_QK_PAD), k_spec(H, MLA_QK_PAD), vt_spec(H * MLA_V_DIM)],
        out_shape=[jax.ShapeDtypeStruct((T, H * MLA_QK_PAD), BF16), k_shape(H, MLA_QK_PAD),
                   vt_shape(H * MLA_V_DIM)],
        compiler_params=_params("parallel"), name="mla_proj",
    )(x2, w_lat, row(mla_q_norm), row(mla_kv_norm), w_qn, w_qp, w_kn, w_vt, cos_a, sup_a, sdn_a)

    q_b, k_b, vt_b = pl.pallas_call(
        functools.partial(_diff_proj_kernel, scale=LOG2_E * DIFF_QK_DIM ** -0.5),
        grid=(T // bm,),
        in_specs=[tok(D), _const_spec(w_diff.shape), _const_spec(w_dvt.shape), pos, pos, pos],
        out_specs=[tok(n_diff), k_spec(DIFF_HEADS, LANES), vt_spec(n_diff)],
        out_shape=[jax.ShapeDtypeStruct((T, n_diff), BF16), k_shape(DIFF_HEADS, LANES), vt_shape(n_diff)],
        compiler_params=_params("parallel"), name="diff_proj",
    )(x2, w_diff, w_dvt, cos_b, sup_b, sdn_b)

    bmg = 256
    sig = pl.pallas_call(
        _gate_proj_kernel,
        grid=(T // bmg,),
        in_specs=[pl.BlockSpec((bmg, D), lambda i: (i, 0)), _const_spec(w_gate.shape)],
        out_specs=pl.BlockSpec((bmg, 2 * D), lambda i: (i, 0)),
        out_shape=jax.ShapeDtypeStruct((T, 2 * D), BF16),
        compiler_params=_params("parallel"), name="gate_proj",
    )(x2, w_gate)

    tq = 2 * KV_CHUNK
    hp = 4
    n_kv = S // KV_CHUNK
    head_blk = lambda width: pl.BlockSpec((None, tq, hp * width), lambda b, h, i: (b, i, h))
    k_blk = lambda width: pl.BlockSpec((None, hp, S, width), lambda b, h, i: (b, h, 0, 0))
    vt_blk = lambda dv: pl.BlockSpec((None, n_kv, hp, dv, KV_CHUNK), lambda b, h, i: (b, 0, h, 0, 0))
    attn_a = pl.pallas_call(
        functools.partial(_mla_attn_kernel, tq=tq, n_heads=hp),
        grid=(B, H // hp, S // tq),
        in_specs=[head_blk(MLA_QK_PAD), k_blk(MLA_QK_PAD), vt_blk(MLA_V_DIM)],
        out_specs=head_blk(MLA_V_DIM),
        out_shape=jax.ShapeDtypeStruct((B, S, H * MLA_V_DIM), BF16),
        scratch_shapes=_flash_scratch(hp, MLA_QK_PAD, MLA_V_DIM, tq, tq),
        compiler_params=_params("parallel", "parallel", "parallel"), name="mla_attn",
    )(q_a.reshape(B, S, -1), k_a, vt_a.reshape(B, n_kv, H, MLA_V_DIM, KV_CHUNK))

    lam_spec = pl.BlockSpec((1, DIFF_QK_DIM), lambda b, h, i: (0, 0))
    attn_b = pl.pallas_call(
        functools.partial(_diff_attn_kernel, tq=tq, n_heads=hp, lambda_init=lambda_init),
        grid=(B, DIFF_HEADS // hp, S // tq),
        in_specs=[head_blk(LANES), k_blk(LANES), vt_blk(DIFF_V_DIM),
                  lam_spec, lam_spec, lam_spec, lam_spec, pl.BlockSpec((1, DIFF_V_DIM), lambda b, h, i: (0, 0))],
        out_specs=head_blk(DIFF_V_DIM),
        out_shape=jax.ShapeDtypeStruct((B, S, DIFF_HEADS * DIFF_V_DIM), BF16),
        scratch_shapes=_flash_scratch(hp, LANES, DIFF_V_DIM, 2 * tq, tq),
        compiler_params=_params("parallel", "parallel", "parallel"), name="diff_attn",
    )(q_b.reshape(B, S, -1), k_b, vt_b.reshape(B, n_kv, DIFF_HEADS, DIFF_V_DIM, KV_CHUNK),
      row(diff_lambda_q1), row(diff_lambda_k1), row(diff_lambda_q2), row(diff_lambda_k2), row(diff_subln))

    bmo = 256
    tokm = lambda n, j=0: pl.BlockSpec((bmo, n), lambda i: (i, j))
    h1 = pl.pallas_call(
        _mix_out_kernel,
        grid=(T // bmo,),
        in_specs=[tokm(w_a.shape[0]), tokm(w_b.shape[0]), tokm(D, 0), tokm(D, 1), tokm(D),
                  _const_spec(w_a.shape), _const_spec(w_b.shape), _const_spec(w_o.shape),
                  _const_spec((1, D)), _const_spec((1, D))],
        out_specs=tokm(D),
        out_shape=jax.ShapeDtypeStruct((T, D), F32),
        compiler_params=_params("parallel"), name="mix_out",
    )(attn_a.reshape(T, -1), attn_b.reshape(T, -1), sig, sig, x2, w_a, w_b, w_o, row(ln1_g), row(ln1_b))

    bmf, tf = 512, 512
    n_f = d_ff // tf
    out = pl.pallas_call(
        _ffn_kernel,
        grid=(T // bmf, n_f),
        in_specs=[pl.BlockSpec((bmf, D), lambda i, f: (i, 0)),
                  pl.BlockSpec((D, tf), lambda i, f: (0, f)),
                  pl.BlockSpec((D, tf), lambda i, f: (0, n_f + f)),
                  pl.BlockSpec((tf, D), lambda i, f: (f, 0)),
                  pl.BlockSpec((1, D), lambda i, f: (0, 0)), pl.BlockSpec((1, D), lambda i, f: (0, 0))],
        out_specs=pl.BlockSpec((bmf, D), lambda i, f: (i, 0)),
        out_shape=jax.ShapeDtypeStruct((T, D), F32),
        scratch_shapes=[pltpu.VMEM((bmf, D), BF16), pltpu.VMEM((bmf, D), F32)],
        compiler_params=_params("parallel", "arbitrary"), name="ffn",
    )(h1, w_fi, w_fi, w_fd, row(ln2_g), row(ln2_b))

    return out.reshape(B, S, D)
```

```python
import functools
import math

import jax
import jax.numpy as jnp
from jax import lax
from jax.experimental import pallas as pl
from jax.experimental.pallas import tpu as pltpu

MLA_HEADS = 8
MLA_NOPE_DIM = 128
MLA_ROPE_DIM = 64
MLA_V_DIM = 128
MLA_Q_RANK = 512
MLA_KV_RANK = 512
DIFF_HEADS = 8
DIFF_QK_DIM = 64
DIFF_V_DIM = 128
ROPE_THETA = 10000.0
DEPTH = 1
ALPHA = (2 * DEPTH) ** 0.25
RMS_EPS = 1e-6
SUBLN_EPS = 1e-5
LN_EPS = 1e-5

LOG2_E = math.log2(math.e)
LANES = 128
MLA_QK_PAD = 256
SUM_ROWS = 16
VMEM_LIMIT = 56 * 1024 * 1024

BF16 = jnp.bfloat16
F32 = jnp.float32


def _dot(a, b):
    return jnp.dot(a, b, preferred_element_type=F32)


def _dot_nt(a, b):
    return lax.dot_general(a, b, (((1,), (1,)), ((), ())), preferred_element_type=F32)


def _rms(x, g, eps):
    return x * lax.rsqrt(jnp.mean(x * x, axis=-1, keepdims=True) + eps) * g


def _layer_norm(x, g, b):
    mu = jnp.mean(x, axis=-1, keepdims=True)
    xc = x - mu
    var = jnp.mean(xc * xc, axis=-1, keepdims=True)
    return xc * lax.rsqrt(var + LN_EPS) * g + b


def _rope(x, cos, sin_up, sin_dn):
    up = pltpu.roll(x, LANES - 32, 1)
    dn = pltpu.roll(x, 32, 1)
    return x * cos + up * sin_up + dn * sin_dn


def _sigmoid(x):
    return 1.0 / (1.0 + jnp.exp(-x))


def _const_spec(shape):
    return pl.BlockSpec(shape, lambda *_: (0,) * len(shape), pipeline_mode=pl.Buffered(1))


def _params(*sem):
    return pltpu.CompilerParams(dimension_semantics=sem, vmem_limit_bytes=VMEM_LIMIT)


def _split_w_in_kernel(w_ref, lat_ref, rest_ref, *, n_lat, n_skip):
    w = w_ref[...]
    lat = w[:, :lat_ref.shape[1]]
    col = lax.broadcasted_iota(jnp.int32, lat.shape, 1)
    lat_ref[...] = jnp.where(col < n_lat, lat, 0.0).astype(BF16)
    rest_ref[...] = w[:, n_skip:].astype(BF16)


def _mla_proj_kernel(x_ref, w_lat_ref, gq_ref, gkv_ref, w_qn_ref, w_qp_ref, w_kn_ref, w_vt_ref,
                     cos_ref, sup_ref, sdn_ref, q_ref, k_ref, vt_ref, *, scale):
    xb = x_ref[...].astype(BF16)
    lat = _dot(xb, w_lat_ref[...])
    cq = _rms(lat[:, :MLA_Q_RANK], gq_ref[...], RMS_EPS).astype(BF16)
    ckv = _rms(lat[:, MLA_Q_RANK:MLA_Q_RANK + MLA_KV_RANK], gkv_ref[...], RMS_EPS).astype(BF16)
    cos, sup, sdn = cos_ref[...], sup_ref[...], sdn_ref[...]
    k_pe = _rope(lat[:, MLA_Q_RANK + MLA_KV_RANK:], cos, sup, sdn).astype(BF16)
    qn = _dot(cq, w_qn_ref[...])
    qp = _dot(cq, w_qp_ref[...])
    kn = _dot(ckv, w_kn_ref[...])
    vt_ref[...] = _dot_nt(w_vt_ref[...], ckv).astype(BF16)
    for h in range(MLA_HEADS):
        lo, hi = h * LANES, (h + 1) * LANES
        base = h * MLA_QK_PAD
        q_ref[:, base:base + LANES] = (qn[:, lo:hi] * scale).astype(BF16)
        q_ref[:, base + LANES:base + 2 * LANES] = (_rope(qp[:, lo:hi], cos, sup, sdn) * scale).astype(BF16)
        k_ref[:, base:base + LANES] = kn[:, lo:hi].astype(BF16)
        k_ref[:, base + LANES:base + 2 * LANES] = k_pe


def _diff_proj_kernel(x_ref, w_ref, w_vt_ref, cos_ref, sup_ref, sdn_ref, q_ref, k_ref, vt_ref, *, scale):
    xb = x_ref[...].astype(BF16)
    n = DIFF_HEADS * 2 * DIFF_QK_DIM
    y = _dot(xb, w_ref[...])
    vt_ref[...] = _dot_nt(w_vt_ref[...], xb).astype(BF16)
    cos, sup, sdn = cos_ref[...], sup_ref[...], sdn_ref[...]
    for h in range(DIFF_HEADS):
        lo, hi = h * LANES, (h + 1) * LANES
        q_ref[:, lo:hi] = (_rope(y[:, lo:hi], cos, sup, sdn) * scale).astype(BF16)
        k_ref[:, lo:hi] = _rope(y[:, n + lo:n + hi], cos, sup, sdn).astype(BF16)


def _gate_proj_kernel(x_ref, *refs):
    *w_refs, o_ref = refs
    xb = x_ref[...].astype(BF16)
    n = w_refs[0].shape[1]
    for j, w_ref in enumerate(w_refs):
        o_ref[:, j * n:(j + 1) * n] = _sigmoid(_dot(xb, w_ref[...])).astype(BF16)


def _flash_cols(qs, k_ref, vt_ref, scratch, q_tile, tq):
    n_heads = len(qs)
    per_head = len(scratch) // n_heads
    tk = tq // 2
    cols, dk = qs[0].shape
    dv = vt_ref.shape[2]
    ones = jnp.ones((SUM_ROWS, tk), BF16)

    class Head:
        def __init__(self, h):
            (self.qt, s0, s1, p0, p1, self.m, a0, a1, self.acc) = scratch[h * per_head:(h + 1) * per_head]
            self.s, self.p, self.a = (s0, s1), (p0, p1), (a0, a1)
            self.lanes = slice(h * dk, (h + 1) * dk)
            self.h = h

    heads = [Head(h) for h in range(n_heads)]
    for hd, q in zip(heads, qs):
        hd.qt[...] = q.astype(F32).T.astype(BF16)
        hd.m[...] = jnp.full(hd.m.shape, -jnp.inf, F32)
        hd.acc[...] = jnp.zeros(hd.acc.shape, F32)
        hd.p[1][...] = jnp.zeros(hd.p[1].shape, BF16)
        hd.a[1][...] = jnp.ones(hd.a[1].shape, F32)

    def scores(hd, c, half):
        s = _dot(k_ref[pl.ds(pl.multiple_of(c * tk, tk), tk), hd.lanes], hd.qt[...])
        for g in range(cols // LANES):
            hd.s[half][g] = s[:, g * LANES:(g + 1) * LANES]

    def softmax(hd, half, diagonal):
        for g in range(cols // LANES):
            sl = slice(g * LANES, (g + 1) * LANES)
            s = hd.s[half][g]
            if diagonal:
                k_pos = half * tk + lax.broadcasted_iota(jnp.int32, (tk, LANES), 0)
                q_pos = (g * LANES) % tq + lax.broadcasted_iota(jnp.int32, (tk, LANES), 1)
                s = jnp.where(k_pos <= q_pos, s, -jnp.inf)
            m_old = hd.m[:, sl]
            m_new = jnp.maximum(m_old, jnp.max(s, axis=0, keepdims=True))
            a = jnp.exp2(m_old - m_new)
            p = jnp.exp2(s - m_new)
            hd.m[:, sl] = m_new
            hd.a[half][:, sl] = a
            hd.p[half][g] = p.astype(BF16)

    def weighted_values(hd, pair, half):
        vt = jnp.concatenate([vt_ref[pair, hd.h][:, half * tk:(half + 1) * tk], ones], axis=0)
        p = jnp.concatenate([hd.p[half][g] for g in range(cols // LANES)], axis=1)
        hd.acc[...] = hd.a[half][...] * hd.acc[...] + _dot(vt, p)

    def pair_step(i, diagonal):
        for hd in heads:
            weighted_values(hd, jnp.maximum(i - 1, 0), 1)
        for hd in heads:
            scores(hd, 2 * i + 1, 1)
        for hd in heads:
            softmax(hd, 0, diagonal)
        if not diagonal:
            for hd in heads:
                scores(hd, 2 * i + 2, 0)
        for hd in heads:
            weighted_values(hd, i, 0)
        for hd in heads:
            softmax(hd, 1, diagonal)

    for hd in heads:
        scores(hd, 0, 0)
    lax.fori_loop(0, q_tile, lambda i, c: (pair_step(i, False), c)[1], 0)
    pair_step(q_tile, True)
    for hd in heads:
        weighted_values(hd, q_tile, 1)
    return [(hd.acc[:dv, :], hd.acc[dv:dv + 1, :]) for hd in heads]


def _flash_scratch(n_heads, dk, dv, cols, tq):
    tk = tq // 2
    stat = pltpu.VMEM((1, cols), F32)
    group_major = lambda dtype: pltpu.VMEM((cols // LANES, tk, LANES), dtype)
    return n_heads * [pltpu.VMEM((dk, cols), BF16),
                      group_major(F32), group_major(F32), group_major(BF16), group_major(BF16),
                      stat, stat, stat, pltpu.VMEM((dv + SUM_ROWS, cols), F32)]


def _mla_attn_kernel(q_ref, k_ref, vt_ref, o_ref, *scratch, tq, n_heads):
    qs = [q_ref[:, h * MLA_QK_PAD:(h + 1) * MLA_QK_PAD] for h in range(n_heads)]
    for h, (acc, l) in enumerate(_flash_cols(qs, k_ref, vt_ref, scratch, pl.program_id(2), tq)):
        o_ref[:, h * MLA_V_DIM:(h + 1) * MLA_V_DIM] = (acc / l).T.astype(o_ref.dtype)


def _diff_attn_kernel(q_ref, k_ref, vt_ref, lq1_ref, lk1_ref, lq2_ref, lk2_ref, g_ref, o_ref, *scratch,
                      tq, n_heads, lambda_init):
    lane = lax.broadcasted_iota(jnp.int32, (tq, LANES), 1)
    zero = jnp.zeros((tq, LANES), BF16)
    qs = []
    for h in range(n_heads):
        q = q_ref[:, h * LANES:(h + 1) * LANES]
        qs.append(jnp.concatenate(
            [jnp.where(lane < DIFF_QK_DIM, q, zero), jnp.where(lane >= DIFF_QK_DIM, q, zero)], axis=0))
    lam = (jnp.exp(jnp.sum(lq1_ref[...] * lk1_ref[...], axis=-1, keepdims=True))
           - jnp.exp(jnp.sum(lq2_ref[...] * lk2_ref[...], axis=-1, keepdims=True)) + lambda_init)
    for h, (acc, l) in enumerate(_flash_cols(qs, k_ref, vt_ref, scratch, pl.program_id(2), tq)):
        o = acc / l
        o = (o[:, :tq] - lam * o[:, tq:]).T
        o_ref[:, h * DIFF_V_DIM:(h + 1) * DIFF_V_DIM] = (
            _rms(o, g_ref[...], SUBLN_EPS) * (1.0 - lambda_init)).astype(o_ref.dtype)


def _mix_out_kernel(a_ref, b_ref, sa_ref, sb_ref, x_ref, wa_ref, wb_ref, wo_ref, g_ref, beta_ref, o_ref):
    ya = _dot(a_ref[...], wa_ref[...])
    yb = _dot(b_ref[...], wb_ref[...])
    m = (sa_ref[...].astype(F32) * ya + sb_ref[...].astype(F32) * yb).astype(BF16)
    mixed = _dot(m, wo_ref[...])
    o_ref[...] = _layer_norm(ALPHA * x_ref[...] + mixed, g_ref[...], beta_ref[...])


def _ffn_kernel(h_ref, wg_ref, wu_ref, wd_ref, g_ref, beta_ref, o_ref, hb_ref, acc_ref):
    f = pl.program_id(1)

    @pl.when(f == 0)
    def _():
        hb_ref[...] = h_ref[...].astype(BF16)
        acc_ref[...] = jnp.zeros_like(acc_ref)

    hb = hb_ref[...]
    gate = _dot(hb, wg_ref[...])
    up = _dot(hb, wu_ref[...])
    act = (gate * _sigmoid(gate) * up).astype(BF16)
    acc_ref[...] += _dot(act, wd_ref[...])

    @pl.when(f == pl.num_programs(1) - 1)
    def _():
        o_ref[...] = _layer_norm(ALPHA * h_ref[...] + acc_ref[...], g_ref[...], beta_ref[...])


def _rope_angles(seq, dim):
    inv_freq = 1.0 / (ROPE_THETA ** (jnp.arange(0, dim, 2, dtype=F32) / dim))
    ang = jnp.arange(seq, dtype=F32)[:, None] * inv_freq[None, :]
    return jnp.cos(ang), jnp.sin(ang)


def _rope_tables(cos, sin, groups):
    seq = cos.shape[0]
    z = jnp.zeros_like(cos)
    pad = jnp.zeros((seq, LANES - 64 * groups), F32)
    cos_t = jnp.concatenate([cos, cos] * groups + [pad], axis=-1)
    sin_up = jnp.concatenate([-sin, z] * groups + [pad], axis=-1)
    sin_dn = jnp.concatenate([z, sin] * groups + [pad], axis=-1)
    return cos_t, sin_up, sin_dn


def kernel(x, w_in, mla_q_norm, mla_w_uq, mla_kv_norm, mla_w_ukv, diff_lambda_q1, diff_lambda_k1,
           diff_lambda_q2, diff_lambda_k2, diff_subln, w_branch_a, w_branch_b, w_out, ln1_g, ln1_b,
           w_ffn_in, w_ffn_down, ln2_g, ln2_b):
    B, S, D = x.shape
    T = B * S
    H = MLA_HEADS
    d_ff = w_ffn_down.shape[1]
    lambda_init = 0.8 - 0.6 * math.exp(-0.3 * 0)
    x2 = x.reshape(T, D)

    w = w_in[0]
    n_diff = DIFF_HEADS * 2 * DIFF_QK_DIM
    o_kr = MLA_Q_RANK + MLA_KV_RANK
    o_dq = o_kr + MLA_ROPE_DIM
    rows = 256
    w_lat, w_rest = pl.pallas_call(
        functools.partial(_split_w_in_kernel, n_lat=o_dq, n_skip=o_dq),
        grid=(D // rows,),
        in_specs=[pl.BlockSpec((rows, w.shape[1]), lambda i: (i, 0))],
        out_specs=[pl.BlockSpec((rows, o_kr + LANES), lambda i: (i, 0)),
                   pl.BlockSpec((rows, w.shape[1] - o_dq), lambda i: (i, 0))],
        out_shape=[jax.ShapeDtypeStruct((D, o_kr + LANES), BF16),
                   jax.ShapeDtypeStruct((D, w.shape[1] - o_dq), BF16)],
        compiler_params=_params("parallel"), name="split_w_in",
    )(w)
    w_dvt = w_rest[:, 2 * n_diff:3 * n_diff].T
    gate_blk = n_diff
    gate_first = 3 * n_diff // gate_blk
    n_gate_blk = (w_rest.shape[1] - 3 * n_diff) // gate_blk
    uq = mla_w_uq[0].reshape(MLA_Q_RANK, H, MLA_NOPE_DIM + MLA_ROPE_DIM)
    w_qn = uq[:, :, :MLA_NOPE_DIM].reshape(MLA_Q_RANK, H * LANES).astype(BF16)
    w_qp = jnp.pad(uq[:, :, MLA_NOPE_DIM:], ((0, 0), (0, 0), (0, LANES - MLA_ROPE_DIM))
                   ).reshape(MLA_Q_RANK, H * LANES).astype(BF16)
    ukv = mla_w_ukv[0].reshape(MLA_KV_RANK, H, MLA_NOPE_DIM + MLA_V_DIM)
    w_kn = ukv[:, :, :MLA_NOPE_DIM].reshape(MLA_KV_RANK, H * LANES).astype(BF16)
    w_vt = ukv[:, :, MLA_NOPE_DIM:].reshape(MLA_KV_RANK, H * MLA_V_DIM).T.astype(BF16)
    w_a = w_branch_a[0].astype(BF16)
    w_b = w_branch_b[0].astype(BF16)
    w_o = w_out[0].astype(BF16)
    w_fi = w_ffn_in[0].astype(BF16)
    w_fd = w_ffn_down[0].astype(BF16)
    row = lambda v: v.reshape(1, -1).astype(F32)

    assert MLA_ROPE_DIM == DIFF_QK_DIM
    cos, sin = _rope_angles(S, MLA_ROPE_DIM)
    cos_a, sup_a, sdn_a = _rope_tables(cos, sin, 1)
    cos_b, sup_b, sdn_b = _rope_tables(cos, sin, 2)

    bm = 512
    n_pos = S // bm
    tok = lambda n: pl.BlockSpec((bm, n), lambda i: (i, 0))
    pos = pl.BlockSpec((bm, LANES), lambda i: (i % n_pos, 0))
    vt_spec = lambda n: pl.BlockSpec((None, n, bm), lambda i: (i, 0, 0))
    vt_shape = lambda n: jax.ShapeDtypeStruct((T // bm, n, bm), BF16)
    q_a, k_a, vt_a = pl.pallas_call(
        functools.partial(_mla_proj_kernel, scale=LOG2_E * (MLA_NOPE_DIM + MLA_ROPE_DIM) ** -0.5),
        grid=(T // bm,),
        in_specs=[tok(D), _const_spec(w_lat.shape), _const_spec((1, MLA_Q_RANK)), _const_spec((1, MLA_KV_RANK)),
                  _const_spec(w_qn.shape), _const_spec(w_qp.shape), _const_spec(w_kn.shape),
                  _const_spec(w_vt.shape), pos, pos, pos],
        out_specs=[tok(H * MLA_QK_PAD), tok(H * MLA_QK_PAD), vt_spec(H * MLA_V_DIM)],
        out_shape=[jax.ShapeDtypeStruct((T, H * MLA_QK_PAD), BF16), jax.ShapeDtypeStruct((T, H * MLA_QK_PAD), BF16),
                   vt_shape(H * MLA_V_DIM)],
        compiler_params=_params("parallel"), name="mla_proj",
    )(x2, w_lat, row(mla_q_norm), row(mla_kv_norm), w_qn, w_qp, w_kn, w_vt, cos_a, sup_a, sdn_a)

    q_b, k_b, vt_b = pl.pallas_call(
        functools.partial(_diff_proj_kernel, scale=LOG2_E * DIFF_QK_DIM ** -0.5),
        grid=(T // bm,),
        in_specs=[tok(D), _const_spec((D, 2 * n_diff)), _const_spec(w_dvt.shape), pos, pos, pos],
        out_specs=[tok(n_diff), tok(n_diff), vt_spec(n_diff)],
        out_shape=[jax.ShapeDtypeStruct((T, n_diff), BF16)] * 2 + [vt_shape(n_diff)],
        compiler_params=_params("parallel"), name="diff_proj",
    )(x2, w_rest, w_dvt, cos_b, sup_b, sdn_b)

    bmg = 256
    sig = pl.pallas_call(
        _gate_proj_kernel,
        grid=(T // bmg,),
        in_specs=[pl.BlockSpec((bmg, D), lambda i: (i, 0))] + [
            pl.BlockSpec((D, gate_blk), lambda i, j=j: (0, gate_first + j), pipeline_mode=pl.Buffered(1))
            for j in range(n_gate_blk)],
        out_specs=pl.BlockSpec((bmg, 2 * D), lambda i: (i, 0)),
        out_shape=jax.ShapeDtypeStruct((T, 2 * D), BF16),
        compiler_params=_params("parallel"), name="gate_proj",
    )(x2, *([w_rest] * n_gate_blk))

    tq = bm
    hp = 4
    head_blk = lambda rows, width, full: pl.BlockSpec(
        (None, rows, hp * width), (lambda b, h, i: (b, 0, h)) if full else (lambda b, h, i: (b, i, h)))
    vt_blk = lambda dv: pl.BlockSpec((None, S // tq, hp, dv, tq), lambda b, h, i: (b, 0, h, 0, 0))
    attn_a = pl.pallas_call(
        functools.partial(_mla_attn_kernel, tq=tq, n_heads=hp),
        grid=(B, H // hp, S // tq),
        in_specs=[head_blk(tq, MLA_QK_PAD, False), head_blk(S, MLA_QK_PAD, True), vt_blk(MLA_V_DIM)],
        out_specs=head_blk(tq, MLA_V_DIM, False),
        out_shape=jax.ShapeDtypeStruct((B, S, H * MLA_V_DIM), BF16),
        scratch_shapes=_flash_scratch(hp, MLA_QK_PAD, MLA_V_DIM, tq, tq),
        compiler_params=_params("parallel", "parallel", "parallel"), name="mla_attn",
    )(q_a.reshape(B, S, -1), k_a.reshape(B, S, -1), vt_a.reshape(B, S // tq, H, MLA_V_DIM, tq))

    lam_spec = pl.BlockSpec((1, DIFF_QK_DIM), lambda b, h, i: (0, 0))
    attn_b = pl.pallas_call(
        functools.partial(_diff_attn_kernel, tq=tq, n_heads=hp, lambda_init=lambda_init),
        grid=(B, DIFF_HEADS // hp, S // tq),
        in_specs=[head_blk(tq, LANES, False), head_blk(S, LANES, True), vt_blk(DIFF_V_DIM),
                  lam_spec, lam_spec, lam_spec, lam_spec, pl.BlockSpec((1, DIFF_V_DIM), lambda b, h, i: (0, 0))],
        out_specs=head_blk(tq, DIFF_V_DIM, False),
        out_shape=jax.ShapeDtypeStruct((B, S, DIFF_HEADS * DIFF_V_DIM), BF16),
        scratch_shapes=_flash_scratch(hp, LANES, DIFF_V_DIM, 2 * tq, tq),
        compiler_params=_params("parallel", "parallel", "parallel"), name="diff_attn",
    )(q_b.reshape(B, S, -1), k_b.reshape(B, S, -1), vt_b.reshape(B, S // tq, DIFF_HEADS, DIFF_V_DIM, tq),
      row(diff_lambda_q1), row(diff_lambda_k1), row(diff_lambda_q2), row(diff_lambda_k2), row(diff_subln))

    bmo = 256
    tokm = lambda n, j=0: pl.BlockSpec((bmo, n), lambda i: (i, j))
    h1 = pl.pallas_call(
        _mix_out_kernel,
        grid=(T // bmo,),
        in_specs=[tokm(w_a.shape[0]), tokm(w_b.shape[0]), tokm(D, 0), tokm(D, 1), tokm(D),
                  _const_spec(w_a.shape), _const_spec(w_b.shape), _const_spec(w_o.shape),
                  _const_spec((1, D)), _const_spec((1, D))],
        out_specs=tokm(D),
        out_shape=jax.ShapeDtypeStruct((T, D), F32),
        compiler_params=_params("parallel"), name="mix_out",
    )(attn_a.reshape(T, -1), attn_b.reshape(T, -1), sig, sig, x2, w_a, w_b, w_o, row(ln1_g), row(ln1_b))

    bmf, tf = 512, 512
    n_f = d_ff // tf
    out = pl.pallas_call(
        _ffn_kernel,
        grid=(T // bmf, n_f),
        in_specs=[pl.BlockSpec((bmf, D), lambda i, f: (i, 0)),
                  pl.BlockSpec((D, tf), lambda i, f: (0, f)),
                  pl.BlockSpec((D, tf), lambda i, f: (0, n_f + f)),
                  pl.BlockSpec((tf, D), lambda i, f: (f, 0)),
                  pl.BlockSpec((1, D), lambda i, f: (0, 0)), pl.BlockSpec((1, D), lambda i, f: (0, 0))],
        out_specs=pl.BlockSpec((bmf, D), lambda i, f: (i, 0)),
        out_shape=jax.ShapeDtypeStruct((T, D), F32),
        scratch_shapes=[pltpu.VMEM((bmf, D), BF16), pltpu.VMEM((bmf, D), F32)],
        compiler_params=_params("parallel", "arbitrary"), name="ffn",
    )(h1, w_fi, w_fi, w_fd, row(ln2_g), row(ln2_b))

    return out.reshape(B, S, D)
```

```python
import functools
import math

import jax
import jax.numpy as jnp
from jax import lax
from jax.experimental import pallas as pl
from jax.experimental.pallas import tpu as pltpu

MLA_HEADS = 8
MLA_NOPE_DIM = 128
MLA_ROPE_DIM = 64
MLA_V_DIM = 128
MLA_Q_RANK = 512
MLA_KV_RANK = 512
DIFF_HEADS = 8
DIFF_QK_DIM = 64
DIFF_V_DIM = 128
ROPE_THETA = 10000.0
DEPTH = 1
ALPHA = (2 * DEPTH) ** 0.25
RMS_EPS = 1e-6
SUBLN_EPS = 1e-5
LN_EPS = 1e-5

LOG2_E = math.log2(math.e)
LANES = 128
MLA_QK_PAD = 256
SUM_ROWS = 16
VMEM_LIMIT = 56 * 1024 * 1024

BF16 = jnp.bfloat16
F32 = jnp.float32


def _dot(a, b):
    return jnp.dot(a, b, preferred_element_type=F32)


def _dot_nt(a, b):
    return lax.dot_general(a, b, (((1,), (1,)), ((), ())), preferred_element_type=F32)


def _rms(x, g, eps):
    return x * lax.rsqrt(jnp.mean(x * x, axis=-1, keepdims=True) + eps) * g


def _layer_norm(x, g, b):
    mu = jnp.mean(x, axis=-1, keepdims=True)
    xc = x - mu
    var = jnp.mean(xc * xc, axis=-1, keepdims=True)
    return xc * lax.rsqrt(var + LN_EPS) * g + b


def _rope(x, cos, sin_up, sin_dn):
    up = pltpu.roll(x, LANES - 32, 1)
    dn = pltpu.roll(x, 32, 1)
    return x * cos + up * sin_up + dn * sin_dn


def _sigmoid(x):
    return 1.0 / (1.0 + jnp.exp(-x))


def _const_spec(shape):
    return pl.BlockSpec(shape, lambda *_: (0,) * len(shape), pipeline_mode=pl.Buffered(1))


def _params(*sem):
    return pltpu.CompilerParams(dimension_semantics=sem, vmem_limit_bytes=VMEM_LIMIT)


def _mla_proj_kernel(x_ref, w_lat_ref, gq_ref, gkv_ref, w_qn_ref, w_qp_ref, w_kn_ref, w_vt_ref,
                     cos_ref, sup_ref, sdn_ref, q_ref, k_ref, vt_ref, *, scale):
    xb = x_ref[...].astype(BF16)
    lat = _dot_nt(xb, w_lat_ref[...])
    cq = _rms(lat[:, :MLA_Q_RANK], gq_ref[...], RMS_EPS).astype(BF16)
    ckv = _rms(lat[:, MLA_Q_RANK:MLA_Q_RANK + MLA_KV_RANK], gkv_ref[...], RMS_EPS).astype(BF16)
    cos, sup, sdn = cos_ref[...], sup_ref[...], sdn_ref[...]
    k_pe = _rope(lat[:, MLA_Q_RANK + MLA_KV_RANK:], cos, sup, sdn).astype(BF16)
    qn = _dot(cq, w_qn_ref[...])
    qp = _dot(cq, w_qp_ref[...])
    kn = _dot(ckv, w_kn_ref[...])
    vt_ref[...] = _dot_nt(w_vt_ref[...], ckv).astype(BF16)
    for h in range(MLA_HEADS):
        lo, hi = h * LANES, (h + 1) * LANES
        base = h * MLA_QK_PAD
        q_ref[:, base:base + LANES] = (qn[:, lo:hi] * scale).astype(BF16)
        q_ref[:, base + LANES:base + 2 * LANES] = (_rope(qp[:, lo:hi], cos, sup, sdn) * scale).astype(BF16)
        k_ref[:, base:base + LANES] = kn[:, lo:hi].astype(BF16)
        k_ref[:, base + LANES:base + 2 * LANES] = k_pe


def _diff_proj_kernel(x_ref, w_ref, w_vt_ref, cos_ref, sup_ref, sdn_ref, q_ref, k_ref, vt_ref, *, scale):
    xb = x_ref[...].astype(BF16)
    n = DIFF_HEADS * 2 * DIFF_QK_DIM
    y = _dot_nt(xb, w_ref[...])
    vt_ref[...] = _dot_nt(w_vt_ref[...], xb).astype(BF16)
    cos, sup, sdn = cos_ref[...], sup_ref[...], sdn_ref[...]
    for h in range(DIFF_HEADS):
        lo, hi = h * LANES, (h + 1) * LANES
        q_ref[:, lo:hi] = (_rope(y[:, lo:hi], cos, sup, sdn) * scale).astype(BF16)
        k_ref[:, lo:hi] = _rope(y[:, n + lo:n + hi], cos, sup, sdn).astype(BF16)


def _gate_proj_kernel(x_ref, w_ref, o_ref):
    xb = x_ref[...].astype(BF16)
    o_ref[...] = _sigmoid(_dot_nt(xb, w_ref[...])).astype(BF16)


def _flash_cols(qs, k_ref, vt_ref, scratch, q_tile, tq):
    n_heads = len(qs)
    per_head = len(scratch) // n_heads
    tk = tq // 2
    cols, dk = qs[0].shape
    dv = vt_ref.shape[2]
    ones = jnp.ones((SUM_ROWS, tk), BF16)

    class Head:
        def __init__(self, h):
            (self.qt, s0, s1, p0, p1, self.m, a0, a1, self.acc) = scratch[h * per_head:(h + 1) * per_head]
            self.s, self.p, self.a = (s0, s1), (p0, p1), (a0, a1)
            self.lanes = slice(h * dk, (h + 1) * dk)
            self.h = h

    heads = [Head(h) for h in range(n_heads)]
    for hd, q in zip(heads, qs):
        hd.qt[...] = q.astype(F32).T.astype(BF16)
        hd.m[...] = jnp.full(hd.m.shape, -jnp.inf, F32)
        hd.acc[...] = jnp.zeros(hd.acc.shape, F32)
        hd.p[1][...] = jnp.zeros(hd.p[1].shape, BF16)
        hd.a[1][...] = jnp.ones(hd.a[1].shape, F32)

    def scores(hd, c, half):
        s = _dot(k_ref[pl.ds(pl.multiple_of(c * tk, tk), tk), hd.lanes], hd.qt[...])
        for g in range(cols // LANES):
            hd.s[half][g] = s[:, g * LANES:(g + 1) * LANES]

    def softmax(hd, half, diagonal):
        for g in range(cols // LANES):
            sl = slice(g * LANES, (g + 1) * LANES)
            s = hd.s[half][g]
            if diagonal:
                k_pos = half * tk + lax.broadcasted_iota(jnp.int32, (tk, LANES), 0)
                q_pos = (g * LANES) % tq + lax.broadcasted_iota(jnp.int32, (tk, LANES), 1)
                s = jnp.where(k_pos <= q_pos, s, -jnp.inf)
            m_old = hd.m[:, sl]
            m_new = jnp.maximum(m_old, jnp.max(s, axis=0, keepdims=True))
            a = jnp.exp2(m_old - m_new)
            p = jnp.exp2(s - m_new)
            hd.m[:, sl] = m_new
            hd.a[half][:, sl] = a
            hd.p[half][g] = p.astype(BF16)

    def weighted_values(hd, pair, half):
        vt = jnp.concatenate([vt_ref[pair, hd.h][:, half * tk:(half + 1) * tk], ones], axis=0)
        p = jnp.concatenate([hd.p[half][g] for g in range(cols // LANES)], axis=1)
        hd.acc[...] = hd.a[half][...] * hd.acc[...] + _dot(vt, p)

    def pair_step(i, diagonal):
        for hd in heads:
            weighted_values(hd, jnp.maximum(i - 1, 0), 1)
        for hd in heads:
            scores(hd, 2 * i + 1, 1)
        for hd in heads:
            softmax(hd, 0, diagonal)
        if not diagonal:
            for hd in heads:
                scores(hd, 2 * i + 2, 0)
        for hd in heads:
            weighted_values(hd, i, 0)
        for hd in heads:
            softmax(hd, 1, diagonal)

    for hd in heads:
        scores(hd, 0, 0)
    lax.fori_loop(0, q_tile, lambda i, c: (pair_step(i, False), c)[1], 0)
    pair_step(q_tile, True)
    for hd in heads:
        weighted_values(hd, q_tile, 1)
    return [(hd.acc[:dv, :], hd.acc[dv:dv + 1, :]) for hd in heads]


def _flash_scratch(n_heads, dk, dv, cols, tq):
    tk = tq // 2
    stat = pltpu.VMEM((1, cols), F32)
    group_major = lambda dtype: pltpu.VMEM((cols // LANES, tk, LANES), dtype)
    return n_heads * [pltpu.VMEM((dk, cols), BF16),
                      group_major(F32), group_major(F32), group_major(BF16), group_major(BF16),
                      stat, stat, stat, pltpu.VMEM((dv + SUM_ROWS, cols), F32)]


def _mla_attn_kernel(q_ref, k_ref, vt_ref, o_ref, *scratch, tq, n_heads):
    qs = [q_ref[:, h * MLA_QK_PAD:(h + 1) * MLA_QK_PAD] for h in range(n_heads)]
    for h, (acc, l) in enumerate(_flash_cols(qs, k_ref, vt_ref, scratch, pl.program_id(2), tq)):
        o_ref[:, h * MLA_V_DIM:(h + 1) * MLA_V_DIM] = (acc / l).T.astype(o_ref.dtype)


def _diff_attn_kernel(q_ref, k_ref, vt_ref, lq1_ref, lk1_ref, lq2_ref, lk2_ref, g_ref, o_ref, *scratch,
                      tq, n_heads, lambda_init):
    lane = lax.broadcasted_iota(jnp.int32, (tq, LANES), 1)
    zero = jnp.zeros((tq, LANES), BF16)
    qs = []
    for h in range(n_heads):
        q = q_ref[:, h * LANES:(h + 1) * LANES]
        qs.append(jnp.concatenate(
            [jnp.where(lane < DIFF_QK_DIM, q, zero), jnp.where(lane >= DIFF_QK_DIM, q, zero)], axis=0))
    lam = (jnp.exp(jnp.sum(lq1_ref[...] * lk1_ref[...], axis=-1, keepdims=True))
           - jnp.exp(jnp.sum(lq2_ref[...] * lk2_ref[...], axis=-1, keepdims=True)) + lambda_init)
    for h, (acc, l) in enumerate(_flash_cols(qs, k_ref, vt_ref, scratch, pl.program_id(2), tq)):
        o = acc / l
        o = (o[:, :tq] - lam * o[:, tq:]).T
        o_ref[:, h * DIFF_V_DIM:(h + 1) * DIFF_V_DIM] = (
            _rms(o, g_ref[...], SUBLN_EPS) * (1.0 - lambda_init)).astype(o_ref.dtype)


def _mix_out_kernel(a_ref, b_ref, sa_ref, sb_ref, x_ref, wa_ref, wb_ref, wo_ref, g_ref, beta_ref, o_ref):
    ya = _dot(a_ref[...], wa_ref[...])
    yb = _dot(b_ref[...], wb_ref[...])
    m = (sa_ref[...].astype(F32) * ya + sb_ref[...].astype(F32) * yb).astype(BF16)
    mixed = _dot(m, wo_ref[...])
    o_ref[...] = _layer_norm(ALPHA * x_ref[...] + mixed, g_ref[...], beta_ref[...])


def _ffn_kernel(h_ref, wg_ref, wu_ref, wd_ref, g_ref, beta_ref, o_ref, hb_ref, acc_ref):
    f = pl.program_id(1)

    @pl.when(f == 0)
    def _():
        hb_ref[...] = h_ref[...].astype(BF16)
        acc_ref[...] = jnp.zeros_like(acc_ref)

    hb = hb_ref[...]
    gate = _dot(hb, wg_ref[...])
    up = _dot(hb, wu_ref[...])
    act = (gate * _sigmoid(gate) * up).astype(BF16)
    acc_ref[...] += _dot(act, wd_ref[...])

    @pl.when(f == pl.num_programs(1) - 1)
    def _():
        o_ref[...] = _layer_norm(ALPHA * h_ref[...] + acc_ref[...], g_ref[...], beta_ref[...])


def _rope_angles(seq, dim):
    inv_freq = 1.0 / (ROPE_THETA ** (jnp.arange(0, dim, 2, dtype=F32) / dim))
    ang = jnp.arange(seq, dtype=F32)[:, None] * inv_freq[None, :]
    return jnp.cos(ang), jnp.sin(ang)


def _rope_tables(cos, sin, groups):
    seq = cos.shape[0]
    z = jnp.zeros_like(cos)
    pad = jnp.zeros((seq, LANES - 64 * groups), F32)
    cos_t = jnp.concatenate([cos, cos] * groups + [pad], axis=-1)
    sin_up = jnp.concatenate([-sin, z] * groups + [pad], axis=-1)
    sin_dn = jnp.concatenate([z, sin] * groups + [pad], axis=-1)
    return cos_t, sin_up, sin_dn


def kernel(x, w_in, mla_q_norm, mla_w_uq, mla_kv_norm, mla_w_ukv, diff_lambda_q1, diff_lambda_k1,
           diff_lambda_q2, diff_lambda_k2, diff_subln, w_branch_a, w_branch_b, w_out, ln1_g, ln1_b,
           w_ffn_in, w_ffn_down, ln2_g, ln2_b):
    B, S, D = x.shape
    T = B * S
    H = MLA_HEADS
    d_ff = w_ffn_down.shape[1]
    lambda_init = 0.8 - 0.6 * math.exp(-0.3 * 0)
    x2 = x.reshape(T, D)

    w = w_in[0]
    n_diff = DIFF_HEADS * 2 * DIFF_QK_DIM
    o_kr = MLA_Q_RANK + MLA_KV_RANK
    o_dq = o_kr + MLA_ROPE_DIM
    wt = jnp.swapaxes(w, 0, 1)
    w_lat = jnp.concatenate([wt[:o_dq], jnp.zeros((LANES - MLA_ROPE_DIM, D), wt.dtype)]).astype(BF16)
    w_dqk = wt[o_dq:o_dq + 2 * n_diff].astype(BF16)
    w_dvt = wt[o_dq + 2 * n_diff:o_dq + 3 * n_diff].astype(BF16)
    w_gate = wt[o_dq + 3 * n_diff:].astype(BF16)
    uq = mla_w_uq[0].reshape(MLA_Q_RANK, H, MLA_NOPE_DIM + MLA_ROPE_DIM)
    w_qn = uq[:, :, :MLA_NOPE_DIM].reshape(MLA_Q_RANK, H * LANES).astype(BF16)
    w_qp = jnp.pad(uq[:, :, MLA_NOPE_DIM:], ((0, 0), (0, 0), (0, LANES - MLA_ROPE_DIM))
                   ).reshape(MLA_Q_RANK, H * LANES).astype(BF16)
    ukv = mla_w_ukv[0].reshape(MLA_KV_RANK, H, MLA_NOPE_DIM + MLA_V_DIM)
    w_kn = ukv[:, :, :MLA_NOPE_DIM].reshape(MLA_KV_RANK, H * LANES).astype(BF16)
    w_vt = ukv[:, :, MLA_NOPE_DIM:].reshape(MLA_KV_RANK, H * MLA_V_DIM).T.astype(BF16)
    w_a = w_branch_a[0].astype(BF16)
    w_b = w_branch_b[0].astype(BF16)
    w_o = w_out[0].astype(BF16)
    w_fi = w_ffn_in[0].astype(BF16)
    w_fd = w_ffn_down[0].astype(BF16)
    row = lambda v: v.reshape(1, -1).astype(F32)

    assert MLA_ROPE_DIM == DIFF_QK_DIM
    cos, sin = _rope_angles(S, MLA_ROPE_DIM)
    cos_a, sup_a, sdn_a = _rope_tables(cos, sin, 1)
    cos_b, sup_b, sdn_b = _rope_tables(cos, sin, 2)

    bm = 512
    n_pos = S // bm
    tok = lambda n: pl.BlockSpec((bm, n), lambda i: (i, 0))
    pos = pl.BlockSpec((bm, LANES), lambda i: (i % n_pos, 0))
    vt_spec = lambda n: pl.BlockSpec((None, n, bm), lambda i: (i, 0, 0))
    vt_shape = lambda n: jax.ShapeDtypeStruct((T // bm, n, bm), BF16)
    q_a, k_a, vt_a = pl.pallas_call(
        functools.partial(_mla_proj_kernel, scale=LOG2_E * (MLA_NOPE_DIM + MLA_ROPE_DIM) ** -0.5),
        grid=(T // bm,),
        in_specs=[tok(D), _const_spec(w_lat.shape), _const_spec((1, MLA_Q_RANK)), _const_spec((1, MLA_KV_RANK)),
                  _const_spec(w_qn.shape), _const_spec(w_qp.shape), _const_spec(w_kn.shape),
                  _const_spec(w_vt.shape), pos, pos, pos],
        out_specs=[tok(H * MLA_QK_PAD), tok(H * MLA_QK_PAD), vt_spec(H * MLA_V_DIM)],
        out_shape=[jax.ShapeDtypeStruct((T, H * MLA_QK_PAD), BF16), jax.ShapeDtypeStruct((T, H * MLA_QK_PAD), BF16),
                   vt_shape(H * MLA_V_DIM)],
        compiler_params=_params("parallel"), name="mla_proj",
    )(x2, w_lat, row(mla_q_norm), row(mla_kv_norm), w_qn, w_qp, w_kn, w_vt, cos_a, sup_a, sdn_a)

    q_b, k_b, vt_b = pl.pallas_call(
        functools.partial(_diff_proj_kernel, scale=LOG2_E * DIFF_QK_DIM ** -0.5),
        grid=(T // bm,),
        in_specs=[tok(D), _const_spec(w_dqk.shape), _const_spec(w_dvt.shape), pos, pos, pos],
        out_specs=[tok(n_diff), tok(n_diff), vt_spec(n_diff)],
        out_shape=[jax.ShapeDtypeStruct((T, n_diff), BF16)] * 2 + [vt_shape(n_diff)],
        compiler_params=_params("parallel"), name="diff_proj",
    )(x2, w_dqk, w_dvt, cos_b, sup_b, sdn_b)

    bmg = 512
    sig = pl.pallas_call(
        _gate_proj_kernel,
        grid=(T // bmg,),
        in_specs=[pl.BlockSpec((bmg, D), lambda i: (i, 0)), _const_spec(w_gate.shape)],
        out_specs=pl.BlockSpec((bmg, 2 * D), lambda i: (i, 0)),
        out_shape=jax.ShapeDtypeStruct((T, 2 * D), BF16),
        compiler_params=_params("parallel"), name="gate_proj",
    )(x2, w_gate)

    tq = bm
    hp = 4
    head_blk = lambda rows, width, full: pl.BlockSpec(
        (None, rows, hp * width), (lambda b, h, i: (b, 0, h)) if full else (lambda b, h, i: (b, i, h)))
    vt_blk = lambda dv: pl.BlockSpec((None, S // tq, hp, dv, tq), lambda b, h, i: (b, 0, h, 0, 0))
    attn_a = pl.pallas_call(
        functools.partial(_mla_attn_kernel, tq=tq, n_heads=hp),
        grid=(B, H // hp, S // tq),
        in_specs=[head_blk(tq, MLA_QK_PAD, False), head_blk(S, MLA_QK_PAD, True), vt_blk(MLA_V_DIM)],
        out_specs=head_blk(tq, MLA_V_DIM, False),
        out_shape=jax.ShapeDtypeStruct((B, S, H * MLA_V_DIM), BF16),
        scratch_shapes=_flash_scratch(hp, MLA_QK_PAD, MLA_V_DIM, tq, tq),
        compiler_params=_params("parallel", "parallel", "parallel"), name="mla_attn",
    )(q_a.reshape(B, S, -1), k_a.reshape(B, S, -1), vt_a.reshape(B, S // tq, H, MLA_V_DIM, tq))

    lam_spec = pl.BlockSpec((1, DIFF_QK_DIM), lambda b, h, i: (0, 0))
    attn_b = pl.pallas_call(
        functools.partial(_diff_attn_kernel, tq=tq, n_heads=hp, lambda_init=lambda_init),
        grid=(B, DIFF_HEADS // hp, S // tq),
        in_specs=[head_blk(tq, LANES, False), head_blk(S, LANES, True), vt_blk(DIFF_V_DIM),
                  lam_spec, lam_spec, lam_spec, lam_spec, pl.BlockSpec((1, DIFF_V_DIM), lambda b, h, i: (0, 0))],
        out_specs=head_blk(tq, DIFF_V_DIM, False),
        out_shape=jax.ShapeDtypeStruct((B, S, DIFF_HEADS * DIFF_V_DIM), BF16),
        scratch_shapes=_flash_scratch(hp, LANES, DIFF_V_DIM, 2 * tq, tq),
        compiler_params=_params("parallel", "parallel", "parallel"), name="diff_attn",
    )(q_b.reshape(B, S, -1), k_b.reshape(B, S, -1), vt_b.reshape(B, S // tq, DIFF_HEADS, DIFF_V_DIM, tq),
      row(diff_lambda_q1), row(diff_lambda_k1), row(diff_lambda_q2), row(diff_lambda_k2), row(diff_subln))

    bmo = 256
    tokm = lambda n, j=0: pl.BlockSpec((bmo, n), lambda i: (i, j))
    h1 = pl.pallas_call(
        _mix_out_kernel,
        grid=(T // bmo,),
        in_specs=[tokm(w_a.shape[0]), tokm(w_b.shape[0]), tokm(D, 0), tokm(D, 1), tokm(D),
                  _const_spec(w_a.shape), _const_spec(w_b.shape), _const_spec(w_o.shape),
                  _const_spec((1, D)), _const_spec((1, D))],
        out_specs=tokm(D),
        out_shape=jax.ShapeDtypeStruct((T, D), F32),
        compiler_params=_params("parallel"), name="mix_out",
    )(attn_a.reshape(T, -1), attn_b.reshape(T, -1), sig, sig, x2, w_a, w_b, w_o, row(ln1_g), row(ln1_b))

    bmf, tf = 512, 512
    n_f = d_ff // tf
    out = pl.pallas_call(
        _ffn_kernel,
        grid=(T // bmf, n_f),
        in_specs=[pl.BlockSpec((bmf, D), lambda i, f: (i, 0)),
                  pl.BlockSpec((D, tf), lambda i, f: (0, f)),
                  pl.BlockSpec((D, tf), lambda i, f: (0, n_f + f)),
                  pl.BlockSpec((tf, D), lambda i, f: (f, 0)),
                  pl.BlockSpec((1, D), lambda i, f: (0, 0)), pl.BlockSpec((1, D), lambda i, f: (0, 0))],
        out_specs=pl.BlockSpec((bmf, D), lambda i, f: (i, 0)),
        out_shape=jax.ShapeDtypeStruct((T, D), F32),
        scratch_shapes=[pltpu.VMEM((bmf, D), BF16), pltpu.VMEM((bmf, D), F32)],
        compiler_params=_params("parallel", "arbitrary"), name="ffn",
    )(h1, w_fi, w_fi, w_fd, row(ln2_g), row(ln2_b))

    return out.reshape(B, S, D)
```

```python
import functools
import math

import jax
import jax.numpy as jnp
from jax import lax
from jax.experimental import pallas as pl
from jax.experimental.pallas import tpu as pltpu

MLA_HEADS = 8
MLA_NOPE_DIM = 128
MLA_ROPE_DIM = 64
MLA_V_DIM = 128
MLA_Q_RANK = 512
MLA_KV_RANK = 512
DIFF_HEADS = 8
DIFF_QK_DIM = 64
DIFF_V_DIM = 128
ROPE_THETA = 10000.0
DEPTH = 1
ALPHA = (2 * DEPTH) ** 0.25
RMS_EPS = 1e-6
SUBLN_EPS = 1e-5
LN_EPS = 1e-5

LOG2_E = math.log2(math.e)
LANES = 128
MLA_QK_PAD = 256
SUM_ROWS = 16
VMEM_LIMIT = 56 * 1024 * 1024

BF16 = jnp.bfloat16
F32 = jnp.float32


def _dot(a, b):
    return jnp.dot(a, b, preferred_element_type=F32)


def _dot_nt(a, b):
    return lax.dot_general(a, b, (((1,), (1,)), ((), ())), preferred_element_type=F32)


def _rms(x, g, eps):
    return x * lax.rsqrt(jnp.mean(x * x, axis=-1, keepdims=True) + eps) * g


def _layer_norm(x, g, b):
    mu = jnp.mean(x, axis=-1, keepdims=True)
    xc = x - mu
    var = jnp.mean(xc * xc, axis=-1, keepdims=True)
    return xc * lax.rsqrt(var + LN_EPS) * g + b


def _rope(x, cos, sin_up, sin_dn):
    up = pltpu.roll(x, LANES - 32, 1)
    dn = pltpu.roll(x, 32, 1)
    return x * cos + up * sin_up + dn * sin_dn


def _sigmoid(x):
    return 1.0 / (1.0 + jnp.exp(-x))


def _const_spec(shape):
    return pl.BlockSpec(shape, lambda *_: (0,) * len(shape), pipeline_mode=pl.Buffered(1))


def _params(*sem):
    return pltpu.CompilerParams(dimension_semantics=sem, vmem_limit_bytes=VMEM_LIMIT)


def _mla_proj_kernel(x_ref, w_lat_ref, gq_ref, gkv_ref, w_qn_ref, w_qp_ref, w_kn_ref, w_vt_ref,
                     cos_ref, sup_ref, sdn_ref, q_ref, k_ref, vt_ref, *, scale):
    xb = x_ref[...].astype(BF16)
    lat = _dot_nt(xb, w_lat_ref[...])
    cq = _rms(lat[:, :MLA_Q_RANK], gq_ref[...], RMS_EPS).astype(BF16)
    ckv = _rms(lat[:, MLA_Q_RANK:MLA_Q_RANK + MLA_KV_RANK], gkv_ref[...], RMS_EPS).astype(BF16)
    cos, sup, sdn = cos_ref[...], sup_ref[...], sdn_ref[...]
    kr = lat[:, MLA_Q_RANK + MLA_KV_RANK:]
    kr = jnp.where(lax.broadcasted_iota(jnp.int32, kr.shape, 1) < MLA_ROPE_DIM, kr, 0.0)
    k_pe = _rope(kr, cos, sup, sdn).astype(BF16)
    qn = _dot(cq, w_qn_ref[...])
    qp = _dot(cq, w_qp_ref[...])
    kn = _dot(ckv, w_kn_ref[...])
    vt_ref[...] = _dot_nt(w_vt_ref[...], ckv).astype(BF16)
    for h in range(MLA_HEADS):
        lo, hi = h * LANES, (h + 1) * LANES
        base = h * MLA_QK_PAD
        q_ref[:, base:base + LANES] = (qn[:, lo:hi] * scale).astype(BF16)
        q_ref[:, base + LANES:base + 2 * LANES] = (_rope(qp[:, lo:hi], cos, sup, sdn) * scale).astype(BF16)
        k_ref[:, base:base + LANES] = kn[:, lo:hi].astype(BF16)
        k_ref[:, base + LANES:base + 2 * LANES] = k_pe


def _diff_proj_kernel(x_ref, w_ref, w_vt_ref, cos_ref, sup_ref, sdn_ref, q_ref, k_ref, vt_ref, *, scale):
    xb = x_ref[...].astype(BF16)
    n = DIFF_HEADS * 2 * DIFF_QK_DIM
    y = _dot_nt(xb, w_ref[...])
    vt_ref[...] = _dot_nt(w_vt_ref[...], xb).astype(BF16)
    cos, sup, sdn = cos_ref[...], sup_ref[...], sdn_ref[...]
    for h in range(DIFF_HEADS):
        lo, hi = h * LANES, (h + 1) * LANES
        q_ref[:, lo:hi] = (_rope(y[:, lo:hi], cos, sup, sdn) * scale).astype(BF16)
        k_ref[:, lo:hi] = _rope(y[:, n + lo:n + hi], cos, sup, sdn).astype(BF16)


def _gate_proj_kernel(x_ref, w_ref, o_ref):
    xb = x_ref[...].astype(BF16)
    o_ref[...] = _sigmoid(_dot_nt(xb, w_ref[...])).astype(BF16)


def _flash_cols(qs, k_ref, vt_ref, scratch, q_tile, tq):
    n_heads = len(qs)
    per_head = len(scratch) // n_heads
    tk = tq // 2
    cols, dk = qs[0].shape
    dv = vt_ref.shape[2]
    ones = jnp.ones((SUM_ROWS, tk), BF16)

    class Head:
        def __init__(self, h):
            (self.qt, s0, s1, p0, p1, self.m, a0, a1, self.acc) = scratch[h * per_head:(h + 1) * per_head]
            self.s, self.p, self.a = (s0, s1), (p0, p1), (a0, a1)
            self.lanes = slice(h * dk, (h + 1) * dk)
            self.h = h

    heads = [Head(h) for h in range(n_heads)]
    for hd, q in zip(heads, qs):
        hd.qt[...] = q.astype(F32).T.astype(BF16)
        hd.m[...] = jnp.full(hd.m.shape, -jnp.inf, F32)
        hd.acc[...] = jnp.zeros(hd.acc.shape, F32)
        hd.p[1][...] = jnp.zeros(hd.p[1].shape, BF16)
        hd.a[1][...] = jnp.ones(hd.a[1].shape, F32)

    def scores(hd, c, half):
        s = _dot(k_ref[pl.ds(pl.multiple_of(c * tk, tk), tk), hd.lanes], hd.qt[...])
        for g in range(cols // LANES):
            hd.s[half][g] = s[:, g * LANES:(g + 1) * LANES]

    def softmax(hd, half, diagonal):
        for g in range(cols // LANES):
            sl = slice(g * LANES, (g + 1) * LANES)
            s = hd.s[half][g]
            if diagonal:
                k_pos = half * tk + lax.broadcasted_iota(jnp.int32, (tk, LANES), 0)
                q_pos = (g * LANES) % tq + lax.broadcasted_iota(jnp.int32, (tk, LANES), 1)
                s = jnp.where(k_pos <= q_pos, s, -jnp.inf)
            m_old = hd.m[:, sl]
            m_new = jnp.maximum(m_old, jnp.max(s, axis=0, keepdims=True))
            a = jnp.exp2(m_old - m_new)
            p = jnp.exp2(s - m_new)
            hd.m[:, sl] = m_new
            hd.a[half][:, sl] = a
            hd.p[half][g] = p.astype(BF16)

    def weighted_values(hd, pair, half):
        vt = jnp.concatenate([vt_ref[pair, hd.h][:, half * tk:(half + 1) * tk], ones], axis=0)
        p = jnp.concatenate([hd.p[half][g] for g in range(cols // LANES)], axis=1)
        hd.acc[...] = hd.a[half][...] * hd.acc[...] + _dot(vt, p)

    def pair_step(i, diagonal):
        for hd in heads:
            weighted_values(hd, jnp.maximum(i - 1, 0), 1)
        for hd in heads:
            scores(hd, 2 * i + 1, 1)
        for hd in heads:
            softmax(hd, 0, diagonal)
        if not diagonal:
            for hd in heads:
                scores(hd, 2 * i + 2, 0)
        for hd in heads:
            weighted_values(hd, i, 0)
        for hd in heads:
            softmax(hd, 1, diagonal)

    for hd in heads:
        scores(hd, 0, 0)
    lax.fori_loop(0, q_tile, lambda i, c: (pair_step(i, False), c)[1], 0)
    pair_step(q_tile, True)
    for hd in heads:
        weighted_values(hd, q_tile, 1)
    return [(hd.acc[:dv, :], hd.acc[dv:dv + 1, :]) for hd in heads]


def _flash_scratch(n_heads, dk, dv, cols, tq):
    tk = tq // 2
    stat = pltpu.VMEM((1, cols), F32)
    group_major = lambda dtype: pltpu.VMEM((cols // LANES, tk, LANES), dtype)
    return n_heads * [pltpu.VMEM((dk, cols), BF16),
                      group_major(F32), group_major(F32), group_major(BF16), group_major(BF16),
                      stat, stat, stat, pltpu.VMEM((dv + SUM_ROWS, cols), F32)]


def _mla_attn_kernel(q_ref, k_ref, vt_ref, o_ref, *scratch, tq, n_heads):
    qs = [q_ref[:, h * MLA_QK_PAD:(h + 1) * MLA_QK_PAD] for h in range(n_heads)]
    for h, (acc, l) in enumerate(_flash_cols(qs, k_ref, vt_ref, scratch, pl.program_id(2), tq)):
        o_ref[:, h * MLA_V_DIM:(h + 1) * MLA_V_DIM] = (acc / l).T.astype(o_ref.dtype)


def _diff_attn_kernel(q_ref, k_ref, vt_ref, lq1_ref, lk1_ref, lq2_ref, lk2_ref, g_ref, o_ref, *scratch,
                      tq, n_heads, lambda_init):
    lane = lax.broadcasted_iota(jnp.int32, (tq, LANES), 1)
    zero = jnp.zeros((tq, LANES), BF16)
    qs = []
    for h in range(n_heads):
        q = q_ref[:, h * LANES:(h + 1) * LANES]
        qs.append(jnp.concatenate(
            [jnp.where(lane < DIFF_QK_DIM, q, zero), jnp.where(lane >= DIFF_QK_DIM, q, zero)], axis=0))
    lam = (jnp.exp(jnp.sum(lq1_ref[...] * lk1_ref[...], axis=-1, keepdims=True))
           - jnp.exp(jnp.sum(lq2_ref[...] * lk2_ref[...], axis=-1, keepdims=True)) + lambda_init)
    for h, (acc, l) in enumerate(_flash_cols(qs, k_ref, vt_ref, scratch, pl.program_id(2), tq)):
        o = acc / l
        o = (o[:, :tq] - lam * o[:, tq:]).T
        o_ref[:, h * DIFF_V_DIM:(h + 1) * DIFF_V_DIM] = (
            _rms(o, g_ref[...], SUBLN_EPS) * (1.0 - lambda_init)).astype(o_ref.dtype)


def _mix_out_kernel(a_ref, b_ref, sa_ref, sb_ref, x_ref, wa_ref, wb_ref, wo_ref, g_ref, beta_ref, o_ref):
    ya = _dot(a_ref[...], wa_ref[...])
    yb = _dot(b_ref[...], wb_ref[...])
    m = (sa_ref[...].astype(F32) * ya + sb_ref[...].astype(F32) * yb).astype(BF16)
    mixed = _dot(m, wo_ref[...])
    o_ref[...] = _layer_norm(ALPHA * x_ref[...] + mixed, g_ref[...], beta_ref[...])


def _ffn_kernel(h_ref, wg_ref, wu_ref, wd_ref, g_ref, beta_ref, o_ref, hb_ref, acc_ref):
    f = pl.program_id(1)

    @pl.when(f == 0)
    def _():
        hb_ref[...] = h_ref[...].astype(BF16)
        acc_ref[...] = jnp.zeros_like(acc_ref)

    hb = hb_ref[...]
    gate = _dot(hb, wg_ref[...])
    up = _dot(hb, wu_ref[...])
    act = (gate * _sigmoid(gate) * up).astype(BF16)
    acc_ref[...] += _dot(act, wd_ref[...])

    @pl.when(f == pl.num_programs(1) - 1)
    def _():
        o_ref[...] = _layer_norm(ALPHA * h_ref[...] + acc_ref[...], g_ref[...], beta_ref[...])


def _rope_angles(seq, dim):
    inv_freq = 1.0 / (ROPE_THETA ** (jnp.arange(0, dim, 2, dtype=F32) / dim))
    ang = jnp.arange(seq, dtype=F32)[:, None] * inv_freq[None, :]
    return jnp.cos(ang), jnp.sin(ang)


def _rope_tables(cos, sin, groups):
    seq = cos.shape[0]
    z = jnp.zeros_like(cos)
    pad = jnp.zeros((seq, LANES - 64 * groups), F32)
    cos_t = jnp.concatenate([cos, cos] * groups + [pad], axis=-1)
    sin_up = jnp.concatenate([-sin, z] * groups + [pad], axis=-1)
    sin_dn = jnp.concatenate([z, sin] * groups + [pad], axis=-1)
    return cos_t, sin_up, sin_dn


def kernel(x, w_in, mla_q_norm, mla_w_uq, mla_kv_norm, mla_w_ukv, diff_lambda_q1, diff_lambda_k1,
           diff_lambda_q2, diff_lambda_k2, diff_subln, w_branch_a, w_branch_b, w_out, ln1_g, ln1_b,
           w_ffn_in, w_ffn_down, ln2_g, ln2_b):
    B, S, D = x.shape
    T = B * S
    H = MLA_HEADS
    d_ff = w_ffn_down.shape[1]
    lambda_init = 0.8 - 0.6 * math.exp(-0.3 * 0)
    x2 = x.reshape(T, D)

    w = w_in[0]
    n_diff = DIFF_HEADS * 2 * DIFF_QK_DIM
    o_kr = MLA_Q_RANK + MLA_KV_RANK
    o_dq = o_kr + MLA_ROPE_DIM
    wt = jnp.swapaxes(w, 0, 1)
    wt = wt.astype(BF16)
    w_rows = lambda start, n: pl.BlockSpec((pl.Element(n), pl.Element(D)), lambda *_: (start, 0),
                                           pipeline_mode=pl.Buffered(1))
    uq = mla_w_uq[0].reshape(MLA_Q_RANK, H, MLA_NOPE_DIM + MLA_ROPE_DIM)
    w_qn = uq[:, :, :MLA_NOPE_DIM].reshape(MLA_Q_RANK, H * LANES).astype(BF16)
    w_qp = jnp.pad(uq[:, :, MLA_NOPE_DIM:], ((0, 0), (0, 0), (0, LANES - MLA_ROPE_DIM))
                   ).reshape(MLA_Q_RANK, H * LANES).astype(BF16)
    ukv = mla_w_ukv[0].reshape(MLA_KV_RANK, H, MLA_NOPE_DIM + MLA_V_DIM)
    w_kn = ukv[:, :, :MLA_NOPE_DIM].reshape(MLA_KV_RANK, H * LANES).astype(BF16)
    w_vt = ukv[:, :, MLA_NOPE_DIM:].reshape(MLA_KV_RANK, H * MLA_V_DIM).T.astype(BF16)
    w_a = w_branch_a[0].astype(BF16)
    w_b = w_branch_b[0].astype(BF16)
    w_o = w_out[0].astype(BF16)
    w_fi = w_ffn_in[0].astype(BF16)
    w_fd = w_ffn_down[0].astype(BF16)
    row = lambda v: v.reshape(1, -1).astype(F32)

    assert MLA_ROPE_DIM == DIFF_QK_DIM
    cos, sin = _rope_angles(S, MLA_ROPE_DIM)
    cos_a, sup_a, sdn_a = _rope_tables(cos, sin, 1)
    cos_b, sup_b, sdn_b = _rope_tables(cos, sin, 2)

    bm = 512
    n_pos = S // bm
    tok = lambda n: pl.BlockSpec((bm, n), lambda i: (i, 0))
    pos = pl.BlockSpec((bm, LANES), lambda i: (i % n_pos, 0))
    vt_spec = lambda n: pl.BlockSpec((None, n, bm), lambda i: (i, 0, 0))
    vt_shape = lambda n: jax.ShapeDtypeStruct((T // bm, n, bm), BF16)
    q_a, k_a, vt_a = pl.pallas_call(
        functools.partial(_mla_proj_kernel, scale=LOG2_E * (MLA_NOPE_DIM + MLA_ROPE_DIM) ** -0.5),
        grid=(T // bm,),
        in_specs=[tok(D), w_rows(0, o_kr + LANES), _const_spec((1, MLA_Q_RANK)), _const_spec((1, MLA_KV_RANK)),
                  _const_spec(w_qn.shape), _const_spec(w_qp.shape), _const_spec(w_kn.shape),
                  _const_spec(w_vt.shape), pos, pos, pos],
        out_specs=[tok(H * MLA_QK_PAD), tok(H * MLA_QK_PAD), vt_spec(H * MLA_V_DIM)],
        out_shape=[jax.ShapeDtypeStruct((T, H * MLA_QK_PAD), BF16), jax.ShapeDtypeStruct((T, H * MLA_QK_PAD), BF16),
                   vt_shape(H * MLA_V_DIM)],
        compiler_params=_params("parallel"), name="mla_proj",
    )(x2, wt, row(mla_q_norm), row(mla_kv_norm), w_qn, w_qp, w_kn, w_vt, cos_a, sup_a, sdn_a)

    q_b, k_b, vt_b = pl.pallas_call(
        functools.partial(_diff_proj_kernel, scale=LOG2_E * DIFF_QK_DIM ** -0.5),
        grid=(T // bm,),
        in_specs=[tok(D), w_rows(o_dq, 2 * n_diff), w_rows(o_dq + 2 * n_diff, n_diff), pos, pos, pos],
        out_specs=[tok(n_diff), tok(n_diff), vt_spec(n_diff)],
        out_shape=[jax.ShapeDtypeStruct((T, n_diff), BF16)] * 2 + [vt_shape(n_diff)],
        compiler_params=_params("parallel"), name="diff_proj",
    )(x2, wt, wt, cos_b, sup_b, sdn_b)

    bmg = 512
    sig = pl.pallas_call(
        _gate_proj_kernel,
        grid=(T // bmg,),
        in_specs=[pl.BlockSpec((bmg, D), lambda i: (i, 0)), w_rows(o_dq + 3 * n_diff, 2 * D)],
        out_specs=pl.BlockSpec((bmg, 2 * D), lambda i: (i, 0)),
        out_shape=jax.ShapeDtypeStruct((T, 2 * D), BF16),
        compiler_params=_params("parallel"), name="gate_proj",
    )(x2, wt)

    tq = bm
    hp = 4
    head_blk = lambda rows, width, full: pl.BlockSpec(
        (None, rows, hp * width), (lambda b, h, i: (b, 0, h)) if full else (lambda b, h, i: (b, i, h)))
    vt_blk = lambda dv: pl.BlockSpec((None, S // tq, hp, dv, tq), lambda b, h, i: (b, 0, h, 0, 0))
    attn_a = pl.pallas_call(
        functools.partial(_mla_attn_kernel, tq=tq, n_heads=hp),
        grid=(B, H // hp, S // tq),
        in_specs=[head_blk(tq, MLA_QK_PAD, False), head_blk(S, MLA_QK_PAD, True), vt_blk(MLA_V_DIM)],
        out_specs=head_blk(tq, MLA_V_DIM, False),
        out_shape=jax.ShapeDtypeStruct((B, S, H * MLA_V_DIM), BF16),
        scratch_shapes=_flash_scratch(hp, MLA_QK_PAD, MLA_V_DIM, tq, tq),
        compiler_params=_params("parallel", "parallel", "parallel"), name="mla_attn",
    )(q_a.reshape(B, S, -1), k_a.reshape(B, S, -1), vt_a.reshape(B, S // tq, H, MLA_V_DIM, tq))

    lam_spec = pl.BlockSpec((1, DIFF_QK_DIM), lambda b, h, i: (0, 0))
    attn_b = pl.pallas_call(
        functools.partial(_diff_attn_kernel, tq=tq, n_heads=hp, lambda_init=lambda_init),
        grid=(B, DIFF_HEADS // hp, S // tq),
        in_specs=[head_blk(tq, LANES, False), head_blk(S, LANES, True), vt_blk(DIFF_V_DIM),
                  lam_spec, lam_spec, lam_spec, lam_spec, pl.BlockSpec((1, DIFF_V_DIM), lambda b, h, i: (0, 0))],
        out_specs=head_blk(tq, DIFF_V_DIM, False),
        out_shape=jax.ShapeDtypeStruct((B, S, DIFF_HEADS * DIFF_V_DIM), BF16),
        scratch_shapes=_flash_scratch(hp, LANES, DIFF_V_DIM, 2 * tq, tq),
        compiler_params=_params("parallel", "parallel", "parallel"), name="diff_attn",
    )(q_b.reshape(B, S, -1), k_b.reshape(B, S, -1), vt_b.reshape(B, S // tq, DIFF_HEADS, DIFF_V_DIM, tq),
      row(diff_lambda_q1), row(diff_lambda_k1), row(diff_lambda_q2), row(diff_lambda_k2), row(diff_subln))

    bmo = 256
    tokm = lambda n, j=0: pl.BlockSpec((bmo, n), lambda i: (i, j))
    h1 = pl.pallas_call(
        _mix_out_kernel,
        grid=(T // bmo,),
        in_specs=[tokm(w_a.shape[0]), tokm(w_b.shape[0]), tokm(D, 0), tokm(D, 1), tokm(D),
                  _const_spec(w_a.shape), _const_spec(w_b.shape), _const_spec(w_o.shape),
                  _const_spec((1, D)), _const_spec((1, D))],
        out_specs=tokm(D),
        out_shape=jax.ShapeDtypeStruct((T, D), F32),
        compiler_params=_params("parallel"), name="mix_out",
    )(attn_a.reshape(T, -1), attn_b.reshape(T, -1), sig, sig, x2, w_a, w_b, w_o, row(ln1_g), row(ln1_b))

    bmf, tf = 512, 512
    n_f = d_ff // tf
    out = pl.pallas_call(
        _ffn_kernel,
        grid=(T // bmf, n_f),
        in_specs=[pl.BlockSpec((bmf, D), lambda i, f: (i, 0)),
                  pl.BlockSpec((D, tf), lambda i, f: (0, f)),
                  pl.BlockSpec((D, tf), lambda i, f: (0, n_f + f)),
                  pl.BlockSpec((tf, D), lambda i, f: (f, 0)),
                  pl.BlockSpec((1, D), lambda i, f: (0, 0)), pl.BlockSpec((1, D), lambda i, f: (0, 0))],
        out_specs=pl.BlockSpec((bmf, D), lambda i, f: (i, 0)),
        out_shape=jax.ShapeDtypeStruct((T, D), F32),
        scratch_shapes=[pltpu.VMEM((bmf, D), BF16), pltpu.VMEM((bmf, D), F32)],
        compiler_params=_params("parallel", "arbitrary"), name="ffn",
    )(h1, w_fi, w_fi, w_fd, row(ln2_g), row(ln2_b))

    return out.reshape(B, S, D)
```

```python
import functools
import math

import jax
import jax.numpy as jnp
from jax import lax
from jax.experimental import pallas as pl
from jax.experimental.pallas import tpu as pltpu

MLA_HEADS = 8
MLA_NOPE_DIM = 128
MLA_ROPE_DIM = 64
MLA_V_DIM = 128
MLA_Q_RANK = 512
MLA_KV_RANK = 512
DIFF_HEADS = 8
DIFF_QK_DIM = 64
DIFF_V_DIM = 128
ROPE_THETA = 10000.0
DEPTH = 1
ALPHA = (2 * DEPTH) ** 0.25
RMS_EPS = 1e-6
SUBLN_EPS = 1e-5
LN_EPS = 1e-5

LOG2_E = math.log2(math.e)
LANES = 128
MLA_QK_PAD = 256
SUM_ROWS = 16
VMEM_LIMIT = 56 * 1024 * 1024

BF16 = jnp.bfloat16
F32 = jnp.float32


def _dot(a, b):
    return jnp.dot(a, b, preferred_element_type=F32)


def _dot_nt(a, b):
    return lax.dot_general(a, b, (((1,), (1,)), ((), ())), preferred_element_type=F32)


def _rms(x, g, eps):
    return x * lax.rsqrt(jnp.mean(x * x, axis=-1, keepdims=True) + eps) * g


def _layer_norm(x, g, b):
    mu = jnp.mean(x, axis=-1, keepdims=True)
    xc = x - mu
    var = jnp.mean(xc * xc, axis=-1, keepdims=True)
    return xc * lax.rsqrt(var + LN_EPS) * g + b


def _rope(x, cos, sin_up, sin_dn):
    up = pltpu.roll(x, LANES - 32, 1)
    dn = pltpu.roll(x, 32, 1)
    return x * cos + up * sin_up + dn * sin_dn


def _sigmoid(x):
    return 1.0 / (1.0 + jnp.exp(-x))


def _const_spec(shape):
    return pl.BlockSpec(shape, lambda *_: (0,) * len(shape), pipeline_mode=pl.Buffered(1))


def _params(*sem):
    return pltpu.CompilerParams(dimension_semantics=sem, vmem_limit_bytes=VMEM_LIMIT)


def _mla_proj_kernel(x_ref, w_lat_ref, gq_ref, gkv_ref, w_qn_ref, w_qp_ref, w_kn_ref, w_vt_ref,
                     cos_ref, sup_ref, sdn_ref, q_ref, k_ref, vt_ref, *, scale):
    xb = x_ref[...].astype(BF16)
    lat = _dot_nt(xb, w_lat_ref[...])
    cq = _rms(lat[:, :MLA_Q_RANK], gq_ref[...], RMS_EPS).astype(BF16)
    ckv = _rms(lat[:, MLA_Q_RANK:MLA_Q_RANK + MLA_KV_RANK], gkv_ref[...], RMS_EPS).astype(BF16)
    cos, sup, sdn = cos_ref[...], sup_ref[...], sdn_ref[...]
    kr = lat[:, MLA_Q_RANK + MLA_KV_RANK:]
    kr = jnp.where(lax.broadcasted_iota(jnp.int32, kr.shape, 1) < MLA_ROPE_DIM, kr, 0.0)
    k_pe = _rope(kr, cos, sup, sdn).astype(BF16)
    qn = _dot(cq, w_qn_ref[...])
    qp = _dot(cq, w_qp_ref[...])
    kn = _dot(ckv, w_kn_ref[...])
    vt_ref[...] = _dot_nt(w_vt_ref[...], ckv).astype(BF16)
    for h in range(MLA_HEADS):
        lo, hi = h * LANES, (h + 1) * LANES
        base = h * MLA_QK_PAD
        q_ref[:, base:base + LANES] = (qn[:, lo:hi] * scale).astype(BF16)
        q_ref[:, base + LANES:base + 2 * LANES] = (_rope(qp[:, lo:hi], cos, sup, sdn) * scale).astype(BF16)
        k_ref[:, base:base + LANES] = kn[:, lo:hi].astype(BF16)
        k_ref[:, base + LANES:base + 2 * LANES] = k_pe


def _diff_proj_kernel(x_ref, w_ref, w_vt_ref, cos_ref, sup_ref, sdn_ref, q_ref, k_ref, vt_ref, *, scale):
    xb = x_ref[...].astype(BF16)
    n = DIFF_HEADS * 2 * DIFF_QK_DIM
    y = _dot_nt(xb, w_ref[...])
    vt_ref[...] = _dot_nt(w_vt_ref[...], xb).astype(BF16)
    cos, sup, sdn = cos_ref[...], sup_ref[...], sdn_ref[...]
    for h in range(DIFF_HEADS):
        lo, hi = h * LANES, (h + 1) * LANES
        q_ref[:, lo:hi] = (_rope(y[:, lo:hi], cos, sup, sdn) * scale).astype(BF16)
        k_ref[:, lo:hi] = _rope(y[:, n + lo:n + hi], cos, sup, sdn).astype(BF16)


def _gate_proj_kernel(x_ref, w_ref, o_ref):
    xb = x_ref[...].astype(BF16)
    o_ref[...] = _sigmoid(_dot_nt(xb, w_ref[...])).astype(BF16)


def _flash_cols(qs, k_ref, vt_ref, scratch, q_tile, tq):
    n_heads = len(qs)
    per_head = len(scratch) // n_heads
    tk = tq // 2
    cols, dk = qs[0].shape
    dv = vt_ref.shape[2]
    ones = jnp.ones((SUM_ROWS, tk), BF16)

    class Head:
        def __init__(self, h):
            (self.qt, s0, s1, p0, p1, self.m, a0, a1, self.acc) = scratch[h * per_head:(h + 1) * per_head]
            self.s, self.p, self.a = (s0, s1), (p0, p1), (a0, a1)
            self.lanes = slice(h * dk, (h + 1) * dk)
            self.h = h

    heads = [Head(h) for h in range(n_heads)]
    for hd, q in zip(heads, qs):
        hd.qt[...] = q.astype(F32).T.astype(BF16)
        hd.m[...] = jnp.full(hd.m.shape, -jnp.inf, F32)
        hd.acc[...] = jnp.zeros(hd.acc.shape, F32)
        hd.p[1][...] = jnp.zeros(hd.p[1].shape, BF16)
        hd.a[1][...] = jnp.ones(hd.a[1].shape, F32)

    all_groups = [(0, cols // LANES)]
    late_groups = [((m * tq + tk) // LANES, (m + 1) * tq // LANES) for m in range(cols // tq)]

    def scores(hd, c, half, runs=all_groups):
        k = k_ref[pl.ds(pl.multiple_of(c * tk, tk), tk), hd.lanes]
        for g0, g1 in runs:
            s = _dot(k, hd.qt[:, g0 * LANES:g1 * LANES])
            for g in range(g0, g1):
                hd.s[half][g] = s[:, (g - g0) * LANES:(g - g0 + 1) * LANES]

    def softmax(hd, half, diagonal, runs=all_groups):
        for g in [g for g0, g1 in runs for g in range(g0, g1)]:
            sl = slice(g * LANES, (g + 1) * LANES)
            s = hd.s[half][g]
            if diagonal:
                k_pos = half * tk + lax.broadcasted_iota(jnp.int32, (tk, LANES), 0)
                q_pos = (g * LANES) % tq + lax.broadcasted_iota(jnp.int32, (tk, LANES), 1)
                s = jnp.where(k_pos <= q_pos, s, -jnp.inf)
            m_old = hd.m[:, sl]
            m_new = jnp.maximum(m_old, jnp.max(s, axis=0, keepdims=True))
            a = jnp.exp2(m_old - m_new)
            p = jnp.exp2(s - m_new)
            hd.m[:, sl] = m_new
            hd.a[half][:, sl] = a
            hd.p[half][g] = p.astype(BF16)

    def weighted_values(hd, pair, half, runs=all_groups):
        vt = jnp.concatenate([vt_ref[pair, hd.h][:, half * tk:(half + 1) * tk], ones], axis=0)
        for g0, g1 in runs:
            sl = slice(g0 * LANES, g1 * LANES)
            p = jnp.concatenate([hd.p[half][g] for g in range(g0, g1)], axis=1)
            hd.acc[:, sl] = hd.a[half][:, sl] * hd.acc[:, sl] + _dot(vt, p)

    def pair_step(i, diagonal):
        second = late_groups if diagonal else all_groups
        for hd in heads:
            weighted_values(hd, jnp.maximum(i - 1, 0), 1)
        for hd in heads:
            scores(hd, 2 * i + 1, 1, second)
        for hd in heads:
            softmax(hd, 0, diagonal)
        if not diagonal:
            for hd in heads:
                scores(hd, 2 * i + 2, 0)
        for hd in heads:
            weighted_values(hd, i, 0)
        for hd in heads:
            softmax(hd, 1, diagonal, second)

    for hd in heads:
        scores(hd, 0, 0)
    lax.fori_loop(0, q_tile, lambda i, c: (pair_step(i, False), c)[1], 0)
    pair_step(q_tile, True)
    for hd in heads:
        weighted_values(hd, q_tile, 1, late_groups)
    return [(hd.acc[:dv, :], hd.acc[dv:dv + 1, :]) for hd in heads]


def _flash_scratch(n_heads, dk, dv, cols, tq):
    tk = tq // 2
    stat = pltpu.VMEM((1, cols), F32)
    group_major = lambda dtype: pltpu.VMEM((cols // LANES, tk, LANES), dtype)
    return n_heads * [pltpu.VMEM((dk, cols), BF16),
                      group_major(F32), group_major(F32), group_major(BF16), group_major(BF16),
                      stat, stat, stat, pltpu.VMEM((dv + SUM_ROWS, cols), F32)]


def _mla_attn_kernel(q_ref, k_ref, vt_ref, o_ref, *scratch, tq, n_heads):
    qs = [q_ref[:, h * MLA_QK_PAD:(h + 1) * MLA_QK_PAD] for h in range(n_heads)]
    for h, (acc, l) in enumerate(_flash_cols(qs, k_ref, vt_ref, scratch, pl.program_id(2), tq)):
        o_ref[:, h * MLA_V_DIM:(h + 1) * MLA_V_DIM] = (acc / l).T.astype(o_ref.dtype)


def _diff_attn_kernel(q_ref, k_ref, vt_ref, lq1_ref, lk1_ref, lq2_ref, lk2_ref, g_ref, o_ref, *scratch,
                      tq, n_heads, lambda_init):
    lane = lax.broadcasted_iota(jnp.int32, (tq, LANES), 1)
    zero = jnp.zeros((tq, LANES), BF16)
    qs = []
    for h in range(n_heads):
        q = q_ref[:, h * LANES:(h + 1) * LANES]
        qs.append(jnp.concatenate(
            [jnp.where(lane < DIFF_QK_DIM, q, zero), jnp.where(lane >= DIFF_QK_DIM, q, zero)], axis=0))
    lam = (jnp.exp(jnp.sum(lq1_ref[...] * lk1_ref[...], axis=-1, keepdims=True))
           - jnp.exp(jnp.sum(lq2_ref[...] * lk2_ref[...], axis=-1, keepdims=True)) + lambda_init)
    for h, (acc, l) in enumerate(_flash_cols(qs, k_ref, vt_ref, scratch, pl.program_id(2), tq)):
        o = acc / l
        o = (o[:, :tq] - lam * o[:, tq:]).T
        o_ref[:, h * DIFF_V_DIM:(h + 1) * DIFF_V_DIM] = (
            _rms(o, g_ref[...], SUBLN_EPS) * (1.0 - lambda_init)).astype(o_ref.dtype)


def _mix_out_kernel(a_ref, b_ref, sa_ref, sb_ref, x_ref, wa_ref, wb_ref, wo_ref, g_ref, beta_ref, o_ref):
    ya = _dot(a_ref[...], wa_ref[...])
    yb = _dot(b_ref[...], wb_ref[...])
    m = (sa_ref[...].astype(F32) * ya + sb_ref[...].astype(F32) * yb).astype(BF16)
    mixed = _dot(m, wo_ref[...])
    o_ref[...] = _layer_norm(ALPHA * x_ref[...] + mixed, g_ref[...], beta_ref[...])


def _ffn_kernel(h_ref, wg_ref, wu_ref, wd_ref, g_ref, beta_ref, o_ref, hb_ref, acc_ref):
    f = pl.program_id(1)

    @pl.when(f == 0)
    def _():
        hb_ref[...] = h_ref[...].astype(BF16)
        acc_ref[...] = jnp.zeros_like(acc_ref)

    hb = hb_ref[...]
    gate = _dot(hb, wg_ref[...])
    up = _dot(hb, wu_ref[...])
    act = (gate * _sigmoid(gate) * up).astype(BF16)
    acc_ref[...] += _dot(act, wd_ref[...])

    @pl.when(f == pl.num_programs(1) - 1)
    def _():
        o_ref[...] = _layer_norm(ALPHA * h_ref[...] + acc_ref[...], g_ref[...], beta_ref[...])


def _rope_angles(seq, dim):
    inv_freq = 1.0 / (ROPE_THETA ** (jnp.arange(0, dim, 2, dtype=F32) / dim))
    ang = jnp.arange(seq, dtype=F32)[:, None] * inv_freq[None, :]
    return jnp.cos(ang), jnp.sin(ang)


def _rope_tables(cos, sin, groups):
    seq = cos.shape[0]
    z = jnp.zeros_like(cos)
    pad = jnp.zeros((seq, LANES - 64 * groups), F32)
    cos_t = jnp.concatenate([cos, cos] * groups + [pad], axis=-1)
    sin_up = jnp.concatenate([-sin, z] * groups + [pad], axis=-1)
    sin_dn = jnp.concatenate([z, sin] * groups + [pad], axis=-1)
    return cos_t, sin_up, sin_dn


def kernel(x, w_in, mla_q_norm, mla_w_uq, mla_kv_norm, mla_w_ukv, diff_lambda_q1, diff_lambda_k1,
           diff_lambda_q2, diff_lambda_k2, diff_subln, w_branch_a, w_branch_b, w_out, ln1_g, ln1_b,
           w_ffn_in, w_ffn_down, ln2_g, ln2_b):
    B, S, D = x.shape
    T = B * S
    H = MLA_HEADS
    d_ff = w_ffn_down.shape[1]
    lambda_init = 0.8 - 0.6 * math.exp(-0.3 * 0)
    x2 = x.reshape(T, D)

    w = w_in[0]
    n_diff = DIFF_HEADS * 2 * DIFF_QK_DIM
    o_kr = MLA_Q_RANK + MLA_KV_RANK
    o_dq = o_kr + MLA_ROPE_DIM
    wt = jnp.swapaxes(w, 0, 1)
    wt = wt.astype(BF16)
    w_rows = lambda start, n: pl.BlockSpec((pl.Element(n), pl.Element(D)), lambda *_: (start, 0),
                                           pipeline_mode=pl.Buffered(1))
    uq = mla_w_uq[0].reshape(MLA_Q_RANK, H, MLA_NOPE_DIM + MLA_ROPE_DIM)
    w_qn = uq[:, :, :MLA_NOPE_DIM].reshape(MLA_Q_RANK, H * LANES).astype(BF16)
    w_qp = jnp.pad(uq[:, :, MLA_NOPE_DIM:], ((0, 0), (0, 0), (0, LANES - MLA_ROPE_DIM))
                   ).reshape(MLA_Q_RANK, H * LANES).astype(BF16)
    ukv = mla_w_ukv[0].reshape(MLA_KV_RANK, H, MLA_NOPE_DIM + MLA_V_DIM)
    w_kn = ukv[:, :, :MLA_NOPE_DIM].reshape(MLA_KV_RANK, H * LANES).astype(BF16)
    w_vt = ukv[:, :, MLA_NOPE_DIM:].reshape(MLA_KV_RANK, H * MLA_V_DIM).T.astype(BF16)
    w_a = w_branch_a[0].astype(BF16)
    w_b = w_branch_b[0].astype(BF16)
    w_o = w_out[0].astype(BF16)
    w_fi = w_ffn_in[0].astype(BF16)
    w_fd = w_ffn_down[0].astype(BF16)
    row = lambda v: v.reshape(1, -1).astype(F32)

    assert MLA_ROPE_DIM == DIFF_QK_DIM
    cos, sin = _rope_angles(S, MLA_ROPE_DIM)
    cos_a, sup_a, sdn_a = _rope_tables(cos, sin, 1)
    cos_b, sup_b, sdn_b = _rope_tables(cos, sin, 2)

    bm = 512
    n_pos = S // bm
    tok = lambda n: pl.BlockSpec((bm, n), lambda i: (i, 0))
    pos = pl.BlockSpec((bm, LANES), lambda i: (i % n_pos, 0))
    vt_spec = lambda n: pl.BlockSpec((None, n, bm), lambda i: (i, 0, 0))
    vt_shape = lambda n: jax.ShapeDtypeStruct((T // bm, n, bm), BF16)
    q_a, k_a, vt_a = pl.pallas_call(
        functools.partial(_mla_proj_kernel, scale=LOG2_E * (MLA_NOPE_DIM + MLA_ROPE_DIM) ** -0.5),
        grid=(T // bm,),
        in_specs=[tok(D), w_rows(0, o_kr + LANES), _const_spec((1, MLA_Q_RANK)), _const_spec((1, MLA_KV_RANK)),
                  _const_spec(w_qn.shape), _const_spec(w_qp.shape), _const_spec(w_kn.shape),
                  _const_spec(w_vt.shape), pos, pos, pos],
        out_specs=[tok(H * MLA_QK_PAD), tok(H * MLA_QK_PAD), vt_spec(H * MLA_V_DIM)],
        out_shape=[jax.ShapeDtypeStruct((T, H * MLA_QK_PAD), BF16), jax.ShapeDtypeStruct((T, H * MLA_QK_PAD), BF16),
                   vt_shape(H * MLA_V_DIM)],
        compiler_params=_params("parallel"), name="mla_proj",
    )(x2, wt, row(mla_q_norm), row(mla_kv_norm), w_qn, w_qp, w_kn, w_vt, cos_a, sup_a, sdn_a)

    q_b, k_b, vt_b = pl.pallas_call(
        functools.partial(_diff_proj_kernel, scale=LOG2_E * DIFF_QK_DIM ** -0.5),
        grid=(T // bm,),
        in_specs=[tok(D), w_rows(o_dq, 2 * n_diff), w_rows(o_dq + 2 * n_diff, n_diff), pos, pos, pos],
        out_specs=[tok(n_diff), tok(n_diff), vt_spec(n_diff)],
        out_shape=[jax.ShapeDtypeStruct((T, n_diff), BF16)] * 2 + [vt_shape(n_diff)],
        compiler_params=_params("parallel"), name="diff_proj",
    )(x2, wt, wt, cos_b, sup_b, sdn_b)

    bmg = 512
    sig = pl.pallas_call(
        _gate_proj_kernel,
        grid=(T // bmg,),
        in_specs=[pl.BlockSpec((bmg, D), lambda i: (i, 0)), w_rows(o_dq + 3 * n_diff, 2 * D)],
        out_specs=pl.BlockSpec((bmg, 2 * D), lambda i: (i, 0)),
        out_shape=jax.ShapeDtypeStruct((T, 2 * D), BF16),
        compiler_params=_params("parallel"), name="gate_proj",
    )(x2, wt)

    tq = bm
    hp = 4
    head_blk = lambda rows, width, full: pl.BlockSpec(
        (None, rows, hp * width), (lambda b, h, i: (b, 0, h)) if full else (lambda b, h, i: (b, i, h)))
    vt_blk = lambda dv: pl.BlockSpec((None, S // tq, hp, dv, tq), lambda b, h, i: (b, 0, h, 0, 0))
    attn_a = pl.pallas_call(
        functools.partial(_mla_attn_kernel, tq=tq, n_heads=hp),
        grid=(B, H // hp, S // tq),
        in_specs=[head_blk(tq, MLA_QK_PAD, False), head_blk(S, MLA_QK_PAD, True), vt_blk(MLA_V_DIM)],
        out_specs=head_blk(tq, MLA_V_DIM, False),
        out_shape=jax.ShapeDtypeStruct((B, S, H * MLA_V_DIM), BF16),
        scratch_shapes=_flash_scratch(hp, MLA_QK_PAD, MLA_V_DIM, tq, tq),
        compiler_params=_params("parallel", "parallel", "parallel"), name="mla_attn",
    )(q_a.reshape(B, S, -1), k_a.reshape(B, S, -1), vt_a.reshape(B, S // tq, H, MLA_V_DIM, tq))

    lam_spec = pl.BlockSpec((1, DIFF_QK_DIM), lambda b, h, i: (0, 0))
    attn_b = pl.pallas_call(
        functools.partial(_diff_attn_kernel, tq=tq, n_heads=hp, lambda_init=lambda_init),
        grid=(B, DIFF_HEADS // hp, S // tq),
        in_specs=[head_blk(tq, LANES, False), head_blk(S, LANES, True), vt_blk(DIFF_V_DIM),
                  lam_spec, lam_spec, lam_spec, lam_spec, pl.BlockSpec((1, DIFF_V_DIM), lambda b, h, i: (0, 0))],
        out_specs=head_blk(tq, DIFF_V_DIM, False),
        out_shape=jax.ShapeDtypeStruct((B, S, DIFF_HEADS * DIFF_V_DIM), BF16),
        scratch_shapes=_flash_scratch(hp, LANES, DIFF_V_DIM, 2 * tq, tq),
        compiler_params=_params("parallel", "parallel", "parallel"), name="diff_attn",
    )(q_b.reshape(B, S, -1), k_b.reshape(B, S, -1), vt_b.reshape(B, S // tq, DIFF_HEADS, DIFF_V_DIM, tq),
      row(diff_lambda_q1), row(diff_lambda_k1), row(diff_lambda_q2), row(diff_lambda_k2), row(diff_subln))

    bmo = 256
    tokm = lambda n, j=0: pl.BlockSpec((bmo, n), lambda i: (i, j))
    h1 = pl.pallas_call(
        _mix_out_kernel,
        grid=(T // bmo,),
        in_specs=[tokm(w_a.shape[0]), tokm(w_b.shape[0]), tokm(D, 0), tokm(D, 1), tokm(D),
                  _const_spec(w_a.shape), _const_spec(w_b.shape), _const_spec(w_o.shape),
                  _const_spec((1, D)), _const_spec((1, D))],
        out_specs=tokm(D),
        out_shape=jax.ShapeDtypeStruct((T, D), F32),
        compiler_params=_params("parallel"), name="mix_out",
    )(attn_a.reshape(T, -1), attn_b.reshape(T, -1), sig, sig, x2, w_a, w_b, w_o, row(ln1_g), row(ln1_b))

    bmf, tf = 512, 512
    n_f = d_ff // tf
    out = pl.pallas_call(
        _ffn_kernel,
        grid=(T // bmf, n_f),
        in_specs=[pl.BlockSpec((bmf, D), lambda i, f: (i, 0)),
                  pl.BlockSpec((D, tf), lambda i, f: (0, f)),
                  pl.BlockSpec((D, tf), lambda i, f: (0, n_f + f)),
                  pl.BlockSpec((tf, D), lambda i, f: (f, 0)),
                  pl.BlockSpec((1, D), lambda i, f: (0, 0)), pl.BlockSpec((1, D), lambda i, f: (0, 0))],
        out_specs=pl.BlockSpec((bmf, D), lambda i, f: (i, 0)),
        out_shape=jax.ShapeDtypeStruct((T, D), F32),
        scratch_shapes=[pltpu.VMEM((bmf, D), BF16), pltpu.VMEM((bmf, D), F32)],
        compiler_params=_params("parallel", "arbitrary"), name="ffn",
    )(h1, w_fi, w_fi, w_fd, row(ln2_g), row(ln2_b))

    return out.reshape(B, S, D)
```

```python
import functools
import math

import jax
import jax.numpy as jnp
from jax import lax
from jax.experimental import pallas as pl
from jax.experimental.pallas import tpu as pltpu

MLA_HEADS = 8
MLA_NOPE_DIM = 128
MLA_ROPE_DIM = 64
MLA_V_DIM = 128
MLA_Q_RANK = 512
MLA_KV_RANK = 512
DIFF_HEADS = 8
DIFF_QK_DIM = 64
DIFF_V_DIM = 128
ROPE_THETA = 10000.0
DEPTH = 1
ALPHA = (2 * DEPTH) ** 0.25
RMS_EPS = 1e-6
SUBLN_EPS = 1e-5
LN_EPS = 1e-5

LOG2_E = math.log2(math.e)
LANES = 128
MLA_QK_PAD = 256
SUM_ROWS = 16
VMEM_LIMIT = 56 * 1024 * 1024

BF16 = jnp.bfloat16
F32 = jnp.float32


def _dot(a, b):
    return jnp.dot(a, b, preferred_element_type=F32)


def _dot_nt(a, b):
    return lax.dot_general(a, b, (((1,), (1,)), ((), ())), preferred_element_type=F32)


def _rms(x, g, eps):
    return x * lax.rsqrt(jnp.mean(x * x, axis=-1, keepdims=True) + eps) * g


def _layer_norm(x, g, b):
    mu = jnp.mean(x, axis=-1, keepdims=True)
    xc = x - mu
    var = jnp.mean(xc * xc, axis=-1, keepdims=True)
    return xc * lax.rsqrt(var + LN_EPS) * g + b


def _rope(x, cos, sin_up, sin_dn):
    up = pltpu.roll(x, LANES - 32, 1)
    dn = pltpu.roll(x, 32, 1)
    return x * cos + up * sin_up + dn * sin_dn


def _sigmoid(x):
    return 1.0 / (1.0 + jnp.exp(-x))


def _const_spec(shape):
    return pl.BlockSpec(shape, lambda *_: (0,) * len(shape), pipeline_mode=pl.Buffered(1))


def _params(*sem):
    return pltpu.CompilerParams(dimension_semantics=sem, vmem_limit_bytes=VMEM_LIMIT)


def _rope_rows(x, cos_t, sin_t):
    half = cos_t.shape[0]
    out = []
    for j in range(x.shape[0] // (2 * half)):
        x1 = x[2 * half * j:2 * half * j + half]
        x2 = x[2 * half * j + half:2 * half * (j + 1)]
        out += [x1 * cos_t - x2 * sin_t, x2 * cos_t + x1 * sin_t]
    return jnp.concatenate(out, axis=0)


def _mla_proj_kernel(x_ref, w_lat_ref, gq_ref, gkv_ref, w_qnt_ref, w_qpt_ref, w_kn_ref, w_vt_ref,
                     cos_ref, sup_ref, sdn_ref, cost_ref, sint_ref, qt_ref, k_ref, vt_ref, *, scale):
    xb = x_ref[...].astype(BF16)
    lat = _dot_nt(xb, w_lat_ref[...])
    cq = _rms(lat[:, :MLA_Q_RANK], gq_ref[...], RMS_EPS).astype(BF16)
    ckv = _rms(lat[:, MLA_Q_RANK:MLA_Q_RANK + MLA_KV_RANK], gkv_ref[...], RMS_EPS).astype(BF16)
    kr = lat[:, MLA_Q_RANK + MLA_KV_RANK:]
    kr = jnp.where(lax.broadcasted_iota(jnp.int32, kr.shape, 1) < MLA_ROPE_DIM, kr, 0.0)
    k_pe = _rope(kr, cos_ref[...], sup_ref[...], sdn_ref[...]).astype(BF16)
    kn = _dot(ckv, w_kn_ref[...])
    vt_ref[...] = _dot_nt(w_vt_ref[...], ckv).astype(BF16)
    qnt = _dot_nt(w_qnt_ref[...], cq)
    qpt = _dot_nt(w_qpt_ref[...], cq)
    cos_t, sin_t = cost_ref[...], sint_ref[...]
    pad = jnp.zeros((MLA_QK_PAD - MLA_NOPE_DIM - MLA_ROPE_DIM, qnt.shape[1]), BF16)
    for h in range(MLA_HEADS):
        base = h * MLA_QK_PAD
        qt_ref[base:base + MLA_NOPE_DIM, :] = (qnt[h * MLA_NOPE_DIM:(h + 1) * MLA_NOPE_DIM] * scale).astype(BF16)
        q_pe = _rope_rows(qpt[h * MLA_ROPE_DIM:(h + 1) * MLA_ROPE_DIM], cos_t, sin_t)
        qt_ref[base + MLA_NOPE_DIM:base + MLA_NOPE_DIM + MLA_ROPE_DIM, :] = (q_pe * scale).astype(BF16)
        qt_ref[base + MLA_NOPE_DIM + MLA_ROPE_DIM:base + MLA_QK_PAD, :] = pad
        k_ref[:, base:base + LANES] = kn[:, h * LANES:(h + 1) * LANES].astype(BF16)
        k_ref[:, base + LANES:base + 2 * LANES] = k_pe


def _diff_proj_kernel(x_ref, w_q_ref, w_k_ref, w_vt_ref, cos_ref, sup_ref, sdn_ref, cost_ref, sint_ref,
                      qt_ref, k_ref, vt_ref, *, scale):
    xb = x_ref[...].astype(BF16)
    yk = _dot_nt(xb, w_k_ref[...])
    vt_ref[...] = _dot_nt(w_vt_ref[...], xb).astype(BF16)
    qt = _dot_nt(w_q_ref[...], xb)
    qt_ref[...] = (_rope_rows(qt, cost_ref[...], sint_ref[...]) * scale).astype(BF16)
    cos, sup, sdn = cos_ref[...], sup_ref[...], sdn_ref[...]
    for h in range(DIFF_HEADS):
        lo, hi = h * LANES, (h + 1) * LANES
        k_ref[:, lo:hi] = _rope(yk[:, lo:hi], cos, sup, sdn).astype(BF16)


def _gate_proj_kernel(x_ref, w_ref, o_ref):
    xb = x_ref[...].astype(BF16)
    o_ref[...] = _sigmoid(_dot_nt(xb, w_ref[...])).astype(BF16)


def _flash_cols(qts, k_ref, vt_ref, scratch, q_tile, tq):
    n_heads = len(qts)
    per_head = len(scratch) // n_heads
    tk = tq // 2
    dk, cols = qts[0].shape
    dv = vt_ref.shape[2]
    ones = jnp.ones((SUM_ROWS, tk), BF16)

    class Head:
        def __init__(self, h):
            (s0, s1, p0, p1, self.m, a0, a1, self.acc) = scratch[h * per_head:(h + 1) * per_head]
            self.qt = qts[h]
            self.s, self.p, self.a = (s0, s1), (p0, p1), (a0, a1)
            self.lanes = slice(h * dk, (h + 1) * dk)
            self.h = h

    heads = [Head(h) for h in range(n_heads)]
    for hd in heads:
        hd.m[...] = jnp.full(hd.m.shape, -jnp.inf, F32)
        hd.acc[...] = jnp.zeros(hd.acc.shape, F32)
        hd.p[1][...] = jnp.zeros(hd.p[1].shape, BF16)
        hd.a[1][...] = jnp.ones(hd.a[1].shape, F32)

    all_groups = [(0, cols // LANES)]
    late_groups = [((m * tq + tk) // LANES, (m + 1) * tq // LANES) for m in range(cols // tq)]

    def scores(hd, c, half, runs=all_groups):
        k = k_ref[pl.ds(pl.multiple_of(c * tk, tk), tk), hd.lanes]
        for g0, g1 in runs:
            s = _dot(k, hd.qt[:, g0 * LANES:g1 * LANES])
            for g in range(g0, g1):
                hd.s[half][g] = s[:, (g - g0) * LANES:(g - g0 + 1) * LANES]

    def softmax(hd, half, diagonal, runs=all_groups):
        for g in [g for g0, g1 in runs for g in range(g0, g1)]:
            sl = slice(g * LANES, (g + 1) * LANES)
            s = hd.s[half][g]
            if diagonal:
                k_pos = half * tk + lax.broadcasted_iota(jnp.int32, (tk, LANES), 0)
                q_pos = (g * LANES) % tq + lax.broadcasted_iota(jnp.int32, (tk, LANES), 1)
                s = jnp.where(k_pos <= q_pos, s, -jnp.inf)
            m_old = hd.m[:, sl]
            m_new = jnp.maximum(m_old, jnp.max(s, axis=0, keepdims=True))
            a = jnp.exp2(m_old - m_new)
            p = jnp.exp2(s - m_new)
            hd.m[:, sl] = m_new
            hd.a[half][:, sl] = a
            hd.p[half][g] = p.astype(BF16)

    def weighted_values(hd, pair, half, runs=all_groups):
        vt = jnp.concatenate([vt_ref[pair, hd.h][:, half * tk:(half + 1) * tk], ones], axis=0)
        for g0, g1 in runs:
            sl = slice(g0 * LANES, g1 * LANES)
            p = jnp.concatenate([hd.p[half][g] for g in range(g0, g1)], axis=1)
            hd.acc[:, sl] = hd.a[half][:, sl] * hd.acc[:, sl] + _dot(vt, p)

    def pair_step(i, diagonal):
        second = late_groups if diagonal else all_groups
        for hd in heads:
            weighted_values(hd, jnp.maximum(i - 1, 0), 1)
        for hd in heads:
            scores(hd, 2 * i + 1, 1, second)
        for hd in heads:
            softmax(hd, 0, diagonal)
        if not diagonal:
            for hd in heads:
                scores(hd, 2 * i + 2, 0)
        for hd in heads:
            weighted_values(hd, i, 0)
        for hd in heads:
            softmax(hd, 1, diagonal, second)

    for hd in heads:
        scores(hd, 0, 0)
    lax.fori_loop(0, q_tile, lambda i, c: (pair_step(i, False), c)[1], 0)
    pair_step(q_tile, True)
    for hd in heads:
        weighted_values(hd, q_tile, 1, late_groups)
    return [(hd.acc[:dv, :], hd.acc[dv:dv + 1, :]) for hd in heads]


def _flash_scratch(n_heads, dv, cols, tq):
    tk = tq // 2
    stat = pltpu.VMEM((1, cols), F32)
    group_major = lambda dtype: pltpu.VMEM((cols // LANES, tk, LANES), dtype)
    return n_heads * [group_major(F32), group_major(F32), group_major(BF16), group_major(BF16),
                      stat, stat, stat, pltpu.VMEM((dv + SUM_ROWS, cols), F32)]


def _mla_attn_kernel(qt_ref, k_ref, vt_ref, o_ref, *scratch, tq, n_heads):
    qts, scratch = scratch[:n_heads], scratch[n_heads:]
    for h in range(n_heads):
        qts[h][...] = qt_ref[h]
    for h, (acc, l) in enumerate(_flash_cols(qts, k_ref, vt_ref, scratch, pl.program_id(2), tq)):
        o_ref[:, h * MLA_V_DIM:(h + 1) * MLA_V_DIM] = (acc / l).T.astype(o_ref.dtype)


def _diff_attn_kernel(qt_ref, k_ref, vt_ref, lq1_ref, lk1_ref, lq2_ref, lk2_ref, g_ref, o_ref, *scratch,
                      tq, n_heads, lambda_init):
    qt2_refs, scratch = scratch[:n_heads], scratch[n_heads:]
    row = lax.broadcasted_iota(jnp.int32, (LANES, tq), 0)
    zero = jnp.zeros((LANES, tq), BF16)
    for h in range(n_heads):
        qt = qt_ref[h]
        qt2_refs[h][:, :tq] = jnp.where(row < DIFF_QK_DIM, qt, zero)
        qt2_refs[h][:, tq:] = jnp.where(row >= DIFF_QK_DIM, qt, zero)
    lam = (jnp.exp(jnp.sum(lq1_ref[...] * lk1_ref[...], axis=-1, keepdims=True))
           - jnp.exp(jnp.sum(lq2_ref[...] * lk2_ref[...], axis=-1, keepdims=True)) + lambda_init)
    for h, (acc, l) in enumerate(_flash_cols(qt2_refs, k_ref, vt_ref, scratch, pl.program_id(2), tq)):
        o = acc / l
        o = (o[:, :tq] - lam * o[:, tq:]).T
        o_ref[:, h * DIFF_V_DIM:(h + 1) * DIFF_V_DIM] = (
            _rms(o, g_ref[...], SUBLN_EPS) * (1.0 - lambda_init)).astype(o_ref.dtype)


def _mix_out_kernel(a_ref, b_ref, sa_ref, sb_ref, x_ref, wa_ref, wb_ref, wo_ref, g_ref, beta_ref, o_ref):
    ya = _dot(a_ref[...], wa_ref[...])
    yb = _dot(b_ref[...], wb_ref[...])
    m = (sa_ref[...].astype(F32) * ya + sb_ref[...].astype(F32) * yb).astype(BF16)
    mixed = _dot(m, wo_ref[...])
    o_ref[...] = _layer_norm(ALPHA * x_ref[...] + mixed, g_ref[...], beta_ref[...])


def _ffn_kernel(h_ref, wg_ref, wu_ref, wd_ref, g_ref, beta_ref, o_ref, hb_ref, acc_ref):
    f = pl.program_id(1)

    @pl.when(f == 0)
    def _():
        hb_ref[...] = h_ref[...].astype(BF16)
        acc_ref[...] = jnp.zeros_like(acc_ref)

    hb = hb_ref[...]
    gate = _dot(hb, wg_ref[...])
    up = _dot(hb, wu_ref[...])
    act = (gate * _sigmoid(gate) * up).astype(BF16)
    acc_ref[...] += _dot(act, wd_ref[...])

    @pl.when(f == pl.num_programs(1) - 1)
    def _():
        o_ref[...] = _layer_norm(ALPHA * h_ref[...] + acc_ref[...], g_ref[...], beta_ref[...])


def _rope_angles(seq, dim):
    inv_freq = 1.0 / (ROPE_THETA ** (jnp.arange(0, dim, 2, dtype=F32) / dim))
    ang = jnp.arange(seq, dtype=F32)[:, None] * inv_freq[None, :]
    return jnp.cos(ang), jnp.sin(ang)


def _rope_tables(cos, sin, groups):
    seq = cos.shape[0]
    z = jnp.zeros_like(cos)
    pad = jnp.zeros((seq, LANES - 64 * groups), F32)
    cos_t = jnp.concatenate([cos, cos] * groups + [pad], axis=-1)
    sin_up = jnp.concatenate([-sin, z] * groups + [pad], axis=-1)
    sin_dn = jnp.concatenate([z, sin] * groups + [pad], axis=-1)
    return cos_t, sin_up, sin_dn


def kernel(x, w_in, mla_q_norm, mla_w_uq, mla_kv_norm, mla_w_ukv, diff_lambda_q1, diff_lambda_k1,
           diff_lambda_q2, diff_lambda_k2, diff_subln, w_branch_a, w_branch_b, w_out, ln1_g, ln1_b,
           w_ffn_in, w_ffn_down, ln2_g, ln2_b):
    B, S, D = x.shape
    T = B * S
    H = MLA_HEADS
    d_ff = w_ffn_down.shape[1]
    lambda_init = 0.8 - 0.6 * math.exp(-0.3 * 0)
    x2 = x.reshape(T, D)

    w = w_in[0]
    n_diff = DIFF_HEADS * 2 * DIFF_QK_DIM
    o_kr = MLA_Q_RANK + MLA_KV_RANK
    o_dq = o_kr + MLA_ROPE_DIM
    wt = jnp.swapaxes(w, 0, 1)
    wt = wt.astype(BF16)
    w_rows = lambda start, n: pl.BlockSpec((pl.Element(n), pl.Element(D)), lambda *_: (start, 0),
                                           pipeline_mode=pl.Buffered(1))
    uq = mla_w_uq[0].reshape(MLA_Q_RANK, H, MLA_NOPE_DIM + MLA_ROPE_DIM)
    w_qnt = uq[:, :, :MLA_NOPE_DIM].reshape(MLA_Q_RANK, H * MLA_NOPE_DIM).T.astype(BF16)
    w_qpt = uq[:, :, MLA_NOPE_DIM:].reshape(MLA_Q_RANK, H * MLA_ROPE_DIM).T.astype(BF16)
    ukv = mla_w_ukv[0].reshape(MLA_KV_RANK, H, MLA_NOPE_DIM + MLA_V_DIM)
    w_kn = ukv[:, :, :MLA_NOPE_DIM].reshape(MLA_KV_RANK, H * LANES).astype(BF16)
    w_vt = ukv[:, :, MLA_NOPE_DIM:].reshape(MLA_KV_RANK, H * MLA_V_DIM).T.astype(BF16)
    w_a = w_branch_a[0].astype(BF16)
    w_b = w_branch_b[0].astype(BF16)
    w_o = w_out[0].astype(BF16)
    w_fi = w_ffn_in[0].astype(BF16)
    w_fd = w_ffn_down[0].astype(BF16)
    row = lambda v: v.reshape(1, -1).astype(F32)

    assert MLA_ROPE_DIM == DIFF_QK_DIM
    cos, sin = _rope_angles(S, MLA_ROPE_DIM)
    cos_a, sup_a, sdn_a = _rope_tables(cos, sin, 1)
    cos_b, sup_b, sdn_b = _rope_tables(cos, sin, 2)
    cos_t, sin_t = cos.T, sin.T

    bm = 512
    n_pos = S // bm
    tok = lambda n: pl.BlockSpec((bm, n), lambda i: (i, 0))
    pos = pl.BlockSpec((bm, LANES), lambda i: (i % n_pos, 0))
    vt_spec = lambda n: pl.BlockSpec((None, n, bm), lambda i: (i, 0, 0))
    vt_shape = lambda n: jax.ShapeDtypeStruct((T // bm, n, bm), BF16)
    pos_t = pl.BlockSpec((cos_t.shape[0], bm), lambda i: (0, i % n_pos))
    qt_a, k_a, vt_a = pl.pallas_call(
        functools.partial(_mla_proj_kernel, scale=LOG2_E * (MLA_NOPE_DIM + MLA_ROPE_DIM) ** -0.5),
        grid=(T // bm,),
        in_specs=[tok(D), w_rows(0, o_kr + LANES), _const_spec((1, MLA_Q_RANK)), _const_spec((1, MLA_KV_RANK)),
                  _const_spec(w_qnt.shape), _const_spec(w_qpt.shape), _const_spec(w_kn.shape),
                  _const_spec(w_vt.shape), pos, pos, pos, pos_t, pos_t],
        out_specs=[vt_spec(H * MLA_QK_PAD), tok(H * MLA_QK_PAD), vt_spec(H * MLA_V_DIM)],
        out_shape=[vt_shape(H * MLA_QK_PAD), jax.ShapeDtypeStruct((T, H * MLA_QK_PAD), BF16),
                   vt_shape(H * MLA_V_DIM)],
        compiler_params=_params("parallel"), name="mla_proj",
    )(x2, wt, row(mla_q_norm), row(mla_kv_norm), w_qnt, w_qpt, w_kn, w_vt, cos_a, sup_a, sdn_a, cos_t, sin_t)

    qt_b, k_b, vt_b = pl.pallas_call(
        functools.partial(_diff_proj_kernel, scale=LOG2_E * DIFF_QK_DIM ** -0.5),
        grid=(T // bm,),
        in_specs=[tok(D), w_rows(o_dq, n_diff), w_rows(o_dq + n_diff, n_diff), w_rows(o_dq + 2 * n_diff, n_diff),
                  pos, pos, pos, pos_t, pos_t],
        out_specs=[vt_spec(n_diff), tok(n_diff), vt_spec(n_diff)],
        out_shape=[vt_shape(n_diff), jax.ShapeDtypeStruct((T, n_diff), BF16), vt_shape(n_diff)],
        compiler_params=_params("parallel"), name="diff_proj",
    )(x2, wt, wt, wt, cos_b, sup_b, sdn_b, cos_t, sin_t)

    bmg = 512
    sig = pl.pallas_call(
        _gate_proj_kernel,
        grid=(T // bmg,),
        in_specs=[pl.BlockSpec((bmg, D), lambda i: (i, 0)), w_rows(o_dq + 3 * n_diff, 2 * D)],
        out_specs=pl.BlockSpec((bmg, 2 * D), lambda i: (i, 0)),
        out_shape=jax.ShapeDtypeStruct((T, 2 * D), BF16),
        compiler_params=_params("parallel"), name="gate_proj",
    )(x2, wt)

    tq = bm
    hp = 4
    head_blk = lambda rows, width, full: pl.BlockSpec(
        (None, rows, hp * width), (lambda b, h, i: (b, 0, h)) if full else (lambda b, h, i: (b, i, h)))
    vt_blk = lambda dv: pl.BlockSpec((None, S // tq, hp, dv, tq), lambda b, h, i: (b, 0, h, 0, 0))
    qt_blk = lambda dk: pl.BlockSpec((None, None, hp, dk, tq), lambda b, h, i: (b, i, h, 0, 0))
    attn_a = pl.pallas_call(
        functools.partial(_mla_attn_kernel, tq=tq, n_heads=hp),
        grid=(B, H // hp, S // tq),
        in_specs=[qt_blk(MLA_QK_PAD), head_blk(S, MLA_QK_PAD, True), vt_blk(MLA_V_DIM)],
        out_specs=head_blk(tq, MLA_V_DIM, False),
        out_shape=jax.ShapeDtypeStruct((B, S, H * MLA_V_DIM), BF16),
        scratch_shapes=hp * [pltpu.VMEM((MLA_QK_PAD, tq), BF16)] + _flash_scratch(hp, MLA_V_DIM, tq, tq),
        compiler_params=_params("parallel", "parallel", "parallel"), name="mla_attn",
    )(qt_a.reshape(B, S // tq, H, MLA_QK_PAD, tq), k_a.reshape(B, S, -1),
      vt_a.reshape(B, S // tq, H, MLA_V_DIM, tq))

    lam_spec = pl.BlockSpec((1, DIFF_QK_DIM), lambda b, h, i: (0, 0))
    attn_b = pl.pallas_call(
        functools.partial(_diff_attn_kernel, tq=tq, n_heads=hp, lambda_init=lambda_init),
        grid=(B, DIFF_HEADS // hp, S // tq),
        in_specs=[qt_blk(LANES), head_blk(S, LANES, True), vt_blk(DIFF_V_DIM),
                  lam_spec, lam_spec, lam_spec, lam_spec, pl.BlockSpec((1, DIFF_V_DIM), lambda b, h, i: (0, 0))],
        out_specs=head_blk(tq, DIFF_V_DIM, False),
        out_shape=jax.ShapeDtypeStruct((B, S, DIFF_HEADS * DIFF_V_DIM), BF16),
        scratch_shapes=hp * [pltpu.VMEM((LANES, 2 * tq), BF16)] + _flash_scratch(hp, DIFF_V_DIM, 2 * tq, tq),
        compiler_params=_params("parallel", "parallel", "parallel"), name="diff_attn",
    )(qt_b.reshape(B, S // tq, DIFF_HEADS, LANES, tq), k_b.reshape(B, S, -1),
      vt_b.reshape(B, S // tq, DIFF_HEADS, DIFF_V_DIM, tq),
      row(diff_lambda_q1), row(diff_lambda_k1), row(diff_lambda_q2), row(diff_lambda_k2), row(diff_subln))

    bmo = 256
    tokm = lambda n, j=0: pl.BlockSpec((bmo, n), lambda i: (i, j))
    h1 = pl.pallas_call(
        _mix_out_kernel,
        grid=(T // bmo,),
        in_specs=[tokm(w_a.shape[0]), tokm(w_b.shape[0]), tokm(D, 0), tokm(D, 1), tokm(D),
                  _const_spec(w_a.shape), _const_spec(w_b.shape), _const_spec(w_o.shape),
                  _const_spec((1, D)), _const_spec((1, D))],
        out_specs=tokm(D),
        out_shape=jax.ShapeDtypeStruct((T, D), F32),
        compiler_params=_params("parallel"), name="mix_out",
    )(attn_a.reshape(T, -1), attn_b.reshape(T, -1), sig, sig, x2, w_a, w_b, w_o, row(ln1_g), row(ln1_b))

    bmf, tf = 512, 512
    n_f = d_ff // tf
    out = pl.pallas_call(
        _ffn_kernel,
        grid=(T // bmf, n_f),
        in_specs=[pl.BlockSpec((bmf, D), lambda i, f: (i, 0)),
                  pl.BlockSpec((D, tf), lambda i, f: (0, f)),
                  pl.BlockSpec((D, tf), lambda i, f: (0, n_f + f)),
                  pl.BlockSpec((tf, D), lambda i, f: (f, 0)),
                  pl.BlockSpec((1, D), lambda i, f: (0, 0)), pl.BlockSpec((1, D), lambda i, f: (0, 0))],
        out_specs=pl.BlockSpec((bmf, D), lambda i, f: (i, 0)),
        out_shape=jax.ShapeDtypeStruct((T, D), F32),
        scratch_shapes=[pltpu.VMEM((bmf, D), BF16), pltpu.VMEM((bmf, D), F32)],
        compiler_params=_params("parallel", "arbitrary"), name="ffn",
    )(h1, w_fi, w_fi, w_fd, row(ln2_g), row(ln2_b))

    return out.reshape(B, S, D)
```

```python
import functools
import math

import jax
import jax.numpy as jnp
from jax import lax
from jax.experimental import pallas as pl
from jax.experimental.pallas import tpu as pltpu

MLA_HEADS = 8
MLA_NOPE_DIM = 128
MLA_ROPE_DIM = 64
MLA_V_DIM = 128
MLA_Q_RANK = 512
MLA_KV_RANK = 512
DIFF_HEADS = 8
DIFF_QK_DIM = 64
DIFF_V_DIM = 128
ROPE_THETA = 10000.0
DEPTH = 1
ALPHA = (2 * DEPTH) ** 0.25
RMS_EPS = 1e-6
SUBLN_EPS = 1e-5
LN_EPS = 1e-5

LOG2_E = math.log2(math.e)
LANES = 128
MLA_QK_PAD = 256
SUM_ROWS = 16
VMEM_LIMIT = 56 * 1024 * 1024

BF16 = jnp.bfloat16
F32 = jnp.float32


def _dot(a, b):
    return jnp.dot(a, b, preferred_element_type=F32)


def _dot_nt(a, b):
    return lax.dot_general(a, b, (((1,), (1,)), ((), ())), preferred_element_type=F32)


def _rms(x, g, eps):
    return x * lax.rsqrt(jnp.mean(x * x, axis=-1, keepdims=True) + eps) * g


def _layer_norm(x, g, b):
    mu = jnp.mean(x, axis=-1, keepdims=True)
    xc = x - mu
    var = jnp.mean(xc * xc, axis=-1, keepdims=True)
    return xc * lax.rsqrt(var + LN_EPS) * g + b


def _rope(x, cos, sin_up, sin_dn):
    up = pltpu.roll(x, LANES - 32, 1)
    dn = pltpu.roll(x, 32, 1)
    return x * cos + up * sin_up + dn * sin_dn


def _sigmoid(x):
    return 1.0 / (1.0 + jnp.exp(-x))


def _const_spec(shape):
    return pl.BlockSpec(shape, lambda *_: (0,) * len(shape), pipeline_mode=pl.Buffered(1))


def _params(*sem):
    return pltpu.CompilerParams(dimension_semantics=sem, vmem_limit_bytes=VMEM_LIMIT)


def _with_casts(body, n_in, n_out, n_cast):
    def kernel(*refs):
        ins, refs = refs[:n_in], refs[n_in:]
        slabs_in, refs = refs[:n_cast], refs[n_cast:]
        outs, refs = refs[:n_out], refs[n_out:]
        slabs_out, scratch = refs[:n_cast], refs[n_cast:]
        for src, dst in zip(slabs_in, slabs_out):
            dst[...] = src[...].astype(BF16)
        body(*ins, *outs, *scratch)
    return kernel


def _slab_specs(arrays, steps):
    ins, outs, shapes = [], [], []
    for a in arrays:
        rows, width = a.shape
        spec = pl.BlockSpec((rows // steps, width), lambda i: (i, 0))
        ins.append(spec)
        outs.append(spec)
        shapes.append(jax.ShapeDtypeStruct(a.shape, BF16))
    return ins, outs, shapes


def _rope_rows(x, cos_t, sin_t):
    half = cos_t.shape[0]
    out = []
    for j in range(x.shape[0] // (2 * half)):
        x1 = x[2 * half * j:2 * half * j + half]
        x2 = x[2 * half * j + half:2 * half * (j + 1)]
        out += [x1 * cos_t - x2 * sin_t, x2 * cos_t + x1 * sin_t]
    return jnp.concatenate(out, axis=0)


def _mla_proj_kernel(x_ref, w_lat_ref, gq_ref, gkv_ref, w_qnt_ref, w_qpt_ref, w_kn_ref, w_vt_ref,
                     cos_ref, sup_ref, sdn_ref, cost_ref, sint_ref, qt_ref, k_ref, vt_ref, *, scale):
    xb = x_ref[...].astype(BF16)
    lat = _dot_nt(xb, w_lat_ref[...])
    cq = _rms(lat[:, :MLA_Q_RANK], gq_ref[...], RMS_EPS).astype(BF16)
    ckv = _rms(lat[:, MLA_Q_RANK:MLA_Q_RANK + MLA_KV_RANK], gkv_ref[...], RMS_EPS).astype(BF16)
    kr = lat[:, MLA_Q_RANK + MLA_KV_RANK:]
    kr = jnp.where(lax.broadcasted_iota(jnp.int32, kr.shape, 1) < MLA_ROPE_DIM, kr, 0.0)
    k_pe = _rope(kr, cos_ref[...], sup_ref[...], sdn_ref[...]).astype(BF16)
    kn = _dot(ckv, w_kn_ref[...])
    vt_ref[...] = _dot_nt(w_vt_ref[...], ckv).astype(BF16)
    qnt = _dot_nt(w_qnt_ref[...], cq)
    qpt = _dot_nt(w_qpt_ref[...], cq)
    cos_t, sin_t = cost_ref[...], sint_ref[...]
    pad = jnp.zeros((MLA_QK_PAD - MLA_NOPE_DIM - MLA_ROPE_DIM, qnt.shape[1]), BF16)
    for h in range(MLA_HEADS):
        base = h * MLA_QK_PAD
        qt_ref[base:base + MLA_NOPE_DIM, :] = (qnt[h * MLA_NOPE_DIM:(h + 1) * MLA_NOPE_DIM] * scale).astype(BF16)
        q_pe = _rope_rows(qpt[h * MLA_ROPE_DIM:(h + 1) * MLA_ROPE_DIM], cos_t, sin_t)
        qt_ref[base + MLA_NOPE_DIM:base + MLA_NOPE_DIM + MLA_ROPE_DIM, :] = (q_pe * scale).astype(BF16)
        qt_ref[base + MLA_NOPE_DIM + MLA_ROPE_DIM:base + MLA_QK_PAD, :] = pad
        k_ref[:, base:base + LANES] = kn[:, h * LANES:(h + 1) * LANES].astype(BF16)
        k_ref[:, base + LANES:base + 2 * LANES] = k_pe


def _diff_proj_kernel(x_ref, w_q_ref, w_k_ref, w_vt_ref, cos_ref, sup_ref, sdn_ref, cost_ref, sint_ref,
                      qt_ref, k_ref, vt_ref, *, scale):
    xb = x_ref[...].astype(BF16)
    yk = _dot_nt(xb, w_k_ref[...])
    vt_ref[...] = _dot_nt(w_vt_ref[...], xb).astype(BF16)
    qt = _dot_nt(w_q_ref[...], xb)
    qt_ref[...] = (_rope_rows(qt, cost_ref[...], sint_ref[...]) * scale).astype(BF16)
    cos, sup, sdn = cos_ref[...], sup_ref[...], sdn_ref[...]
    for h in range(DIFF_HEADS):
        lo, hi = h * LANES, (h + 1) * LANES
        k_ref[:, lo:hi] = _rope(yk[:, lo:hi], cos, sup, sdn).astype(BF16)


def _gate_proj_kernel(x_ref, w_ref, o_ref):
    xb = x_ref[...].astype(BF16)
    o_ref[...] = _sigmoid(_dot_nt(xb, w_ref[...])).astype(BF16)


def _flash_cols(qts, k_ref, vt_ref, scratch, q_tile, tq):
    n_heads = len(qts)
    per_head = len(scratch) // n_heads
    tk = tq // 2
    dk, cols = qts[0].shape
    dv = vt_ref.shape[2]
    ones = jnp.ones((SUM_ROWS, tk), BF16)

    class Head:
        def __init__(self, h):
            (s0, s1, p0, p1, self.m, a0, a1, self.acc) = scratch[h * per_head:(h + 1) * per_head]
            self.qt = qts[h]
            self.s, self.p, self.a = (s0, s1), (p0, p1), (a0, a1)
            self.lanes = slice(h * dk, (h + 1) * dk)
            self.h = h

    heads = [Head(h) for h in range(n_heads)]
    for hd in heads:
        hd.m[...] = jnp.full(hd.m.shape, -jnp.inf, F32)
        hd.acc[...] = jnp.zeros(hd.acc.shape, F32)
        hd.p[1][...] = jnp.zeros(hd.p[1].shape, BF16)
        hd.a[1][...] = jnp.ones(hd.a[1].shape, F32)

    all_groups = [(0, cols // LANES)]
    late_groups = [((m * tq + tk) // LANES, (m + 1) * tq // LANES) for m in range(cols // tq)]

    def scores(hd, c, half, runs=all_groups):
        k = k_ref[pl.ds(pl.multiple_of(c * tk, tk), tk), hd.lanes]
        for g0, g1 in runs:
            s = _dot(k, hd.qt[:, g0 * LANES:g1 * LANES])
            for g in range(g0, g1):
                hd.s[half][g] = s[:, (g - g0) * LANES:(g - g0 + 1) * LANES]

    def softmax(hd, half, diagonal, runs=all_groups):
        for g in [g for g0, g1 in runs for g in range(g0, g1)]:
            sl = slice(g * LANES, (g + 1) * LANES)
            s = hd.s[half][g]
            if diagonal:
                k_pos = half * tk + lax.broadcasted_iota(jnp.int32, (tk, LANES), 0)
                q_pos = (g * LANES) % tq + lax.broadcasted_iota(jnp.int32, (tk, LANES), 1)
                s = jnp.where(k_pos <= q_pos, s, -jnp.inf)
            m_old = hd.m[:, sl]
            m_new = jnp.maximum(m_old, jnp.max(s, axis=0, keepdims=True))
            a = jnp.exp2(m_old - m_new)
            p = jnp.exp2(s - m_new)
            hd.m[:, sl] = m_new
            hd.a[half][:, sl] = a
            hd.p[half][g] = p.astype(BF16)

    def weighted_values(hd, pair, half, runs=all_groups):
        vt = jnp.concatenate([vt_ref[pair, hd.h][:, half * tk:(half + 1) * tk], ones], axis=0)
        for g0, g1 in runs:
            sl = slice(g0 * LANES, g1 * LANES)
            p = jnp.concatenate([hd.p[half][g] for g in range(g0, g1)], axis=1)
            hd.acc[:, sl] = hd.a[half][:, sl] * hd.acc[:, sl] + _dot(vt, p)

    def pair_step(i, diagonal):
        second = late_groups if diagonal else all_groups
        for hd in heads:
            weighted_values(hd, jnp.maximum(i - 1, 0), 1)
        for hd in heads:
            scores(hd, 2 * i + 1, 1, second)
        for hd in heads:
            softmax(hd, 0, diagonal)
        if not diagonal:
            for hd in heads:
                scores(hd, 2 * i + 2, 0)
        for hd in heads:
            weighted_values(hd, i, 0)
        for hd in heads:
            softmax(hd, 1, diagonal, second)

    for hd in heads:
        scores(hd, 0, 0)
    lax.fori_loop(0, q_tile, lambda i, c: (pair_step(i, False), c)[1], 0)
    pair_step(q_tile, True)
    for hd in heads:
        weighted_values(hd, q_tile, 1, late_groups)
    return [(hd.acc[:dv, :], hd.acc[dv:dv + 1, :]) for hd in heads]


def _flash_scratch(n_heads, dv, cols, tq):
    tk = tq // 2
    stat = pltpu.VMEM((1, cols), F32)
    group_major = lambda dtype: pltpu.VMEM((cols // LANES, tk, LANES), dtype)
    return n_heads * [group_major(F32), group_major(F32), group_major(BF16), group_major(BF16),
                      stat, stat, stat, pltpu.VMEM((dv + SUM_ROWS, cols), F32)]


def _mla_attn_kernel(qt_ref, k_ref, vt_ref, o_ref, *scratch, tq, n_heads):
    qts, scratch = scratch[:n_heads], scratch[n_heads:]
    for h in range(n_heads):
        qts[h][...] = qt_ref[h]
    for h, (acc, l) in enumerate(_flash_cols(qts, k_ref, vt_ref, scratch, pl.program_id(2), tq)):
        o_ref[:, h * MLA_V_DIM:(h + 1) * MLA_V_DIM] = (acc / l).T.astype(o_ref.dtype)


def _diff_attn_kernel(qt_ref, k_ref, vt_ref, lq1_ref, lk1_ref, lq2_ref, lk2_ref, g_ref, o_ref, *scratch,
                      tq, n_heads, lambda_init):
    qt2_refs, scratch = scratch[:n_heads], scratch[n_heads:]
    row = lax.broadcasted_iota(jnp.int32, (LANES, tq), 0)
    zero = jnp.zeros((LANES, tq), BF16)
    for h in range(n_heads):
        qt = qt_ref[h]
        qt2_refs[h][:, :tq] = jnp.where(row < DIFF_QK_DIM, qt, zero)
        qt2_refs[h][:, tq:] = jnp.where(row >= DIFF_QK_DIM, qt, zero)
    lam = (jnp.exp(jnp.sum(lq1_ref[...] * lk1_ref[...], axis=-1, keepdims=True))
           - jnp.exp(jnp.sum(lq2_ref[...] * lk2_ref[...], axis=-1, keepdims=True)) + lambda_init)
    for h, (acc, l) in enumerate(_flash_cols(qt2_refs, k_ref, vt_ref, scratch, pl.program_id(2), tq)):
        o = acc / l
        o = (o[:, :tq] - lam * o[:, tq:]).T
        o_ref[:, h * DIFF_V_DIM:(h + 1) * DIFF_V_DIM] = (
            _rms(o, g_ref[...], SUBLN_EPS) * (1.0 - lambda_init)).astype(o_ref.dtype)


def _mix_out_kernel(a_ref, b_ref, sa_ref, sb_ref, x_ref, wa_ref, wb_ref, wo_ref, g_ref, beta_ref, o_ref):
    ya = _dot(a_ref[...], wa_ref[...])
    yb = _dot(b_ref[...], wb_ref[...])
    m = (sa_ref[...].astype(F32) * ya + sb_ref[...].astype(F32) * yb).astype(BF16)
    mixed = _dot(m, wo_ref[...])
    o_ref[...] = _layer_norm(ALPHA * x_ref[...] + mixed, g_ref[...], beta_ref[...])


def _ffn_kernel(h_ref, wg_ref, wu_ref, wd_ref, g_ref, beta_ref, o_ref, hb_ref, acc_ref):
    f = pl.program_id(1)

    @pl.when(f == 0)
    def _():
        hb_ref[...] = h_ref[...].astype(BF16)
        acc_ref[...] = jnp.zeros_like(acc_ref)

    hb = hb_ref[...]
    gate = _dot(hb, wg_ref[...])
    up = _dot(hb, wu_ref[...])
    act = (gate * _sigmoid(gate) * up).astype(BF16)
    acc_ref[...] += _dot(act, wd_ref[...])

    @pl.when(f == pl.num_programs(1) - 1)
    def _():
        o_ref[...] = _layer_norm(ALPHA * h_ref[...] + acc_ref[...], g_ref[...], beta_ref[...])


def _rope_angles(seq, dim):
    inv_freq = 1.0 / (ROPE_THETA ** (jnp.arange(0, dim, 2, dtype=F32) / dim))
    ang = jnp.arange(seq, dtype=F32)[:, None] * inv_freq[None, :]
    return jnp.cos(ang), jnp.sin(ang)


def _rope_tables(cos, sin, groups):
    seq = cos.shape[0]
    z = jnp.zeros_like(cos)
    pad = jnp.zeros((seq, LANES - 64 * groups), F32)
    cos_t = jnp.concatenate([cos, cos] * groups + [pad], axis=-1)
    sin_up = jnp.concatenate([-sin, z] * groups + [pad], axis=-1)
    sin_dn = jnp.concatenate([z, sin] * groups + [pad], axis=-1)
    return cos_t, sin_up, sin_dn


def kernel(x, w_in, mla_q_norm, mla_w_uq, mla_kv_norm, mla_w_ukv, diff_lambda_q1, diff_lambda_k1,
           diff_lambda_q2, diff_lambda_k2, diff_subln, w_branch_a, w_branch_b, w_out, ln1_g, ln1_b,
           w_ffn_in, w_ffn_down, ln2_g, ln2_b):
    B, S, D = x.shape
    T = B * S
    H = MLA_HEADS
    d_ff = w_ffn_down.shape[1]
    lambda_init = 0.8 - 0.6 * math.exp(-0.3 * 0)
    x2 = x.reshape(T, D)

    w = w_in[0]
    n_diff = DIFF_HEADS * 2 * DIFF_QK_DIM
    o_kr = MLA_Q_RANK + MLA_KV_RANK
    o_dq = o_kr + MLA_ROPE_DIM
    wt = jnp.swapaxes(w, 0, 1)
    wt = wt.astype(BF16)
    w_rows = lambda start, n: pl.BlockSpec((pl.Element(n), pl.Element(D)), lambda *_: (start, 0),
                                           pipeline_mode=pl.Buffered(1))
    uq = mla_w_uq[0].reshape(MLA_Q_RANK, H, MLA_NOPE_DIM + MLA_ROPE_DIM)
    w_qnt = uq[:, :, :MLA_NOPE_DIM].reshape(MLA_Q_RANK, H * MLA_NOPE_DIM).T.astype(BF16)
    w_qpt = uq[:, :, MLA_NOPE_DIM:].reshape(MLA_Q_RANK, H * MLA_ROPE_DIM).T.astype(BF16)
    ukv = mla_w_ukv[0].reshape(MLA_KV_RANK, H, MLA_NOPE_DIM + MLA_V_DIM)
    w_kn = ukv[:, :, :MLA_NOPE_DIM].reshape(MLA_KV_RANK, H * LANES).astype(BF16)
    w_vt = ukv[:, :, MLA_NOPE_DIM:].reshape(MLA_KV_RANK, H * MLA_V_DIM).T.astype(BF16)
    row = lambda v: v.reshape(1, -1).astype(F32)

    assert MLA_ROPE_DIM == DIFF_QK_DIM
    cos, sin = _rope_angles(S, MLA_ROPE_DIM)
    cos_a, sup_a, sdn_a = _rope_tables(cos, sin, 1)
    cos_b, sup_b, sdn_b = _rope_tables(cos, sin, 2)
    cos_t, sin_t = cos.T, sin.T

    bm = 512
    n_pos = S // bm
    tok = lambda n: pl.BlockSpec((bm, n), lambda i: (i, 0))
    pos = pl.BlockSpec((bm, LANES), lambda i: (i % n_pos, 0))
    vt_spec = lambda n: pl.BlockSpec((None, n, bm), lambda i: (i, 0, 0))
    vt_shape = lambda n: jax.ShapeDtypeStruct((T // bm, n, bm), BF16)
    pos_t = pl.BlockSpec((cos_t.shape[0], bm), lambda i: (0, i % n_pos))
    mix_weights = [w_branch_a[0], w_branch_b[0], w_out[0]]
    c_in, c_out, c_shape = _slab_specs(mix_weights, T // bm)
    qt_a, k_a, vt_a, w_a, w_b, w_o = pl.pallas_call(
        _with_casts(functools.partial(_mla_proj_kernel, scale=LOG2_E * (MLA_NOPE_DIM + MLA_ROPE_DIM) ** -0.5),
                    13, 3, len(mix_weights)),
        grid=(T // bm,),
        in_specs=[tok(D), w_rows(0, o_kr + LANES), _const_spec((1, MLA_Q_RANK)), _const_spec((1, MLA_KV_RANK)),
                  _const_spec(w_qnt.shape), _const_spec(w_qpt.shape), _const_spec(w_kn.shape),
                  _const_spec(w_vt.shape), pos, pos, pos, pos_t, pos_t] + c_in,
        out_specs=[vt_spec(H * MLA_QK_PAD), tok(H * MLA_QK_PAD), vt_spec(H * MLA_V_DIM)] + c_out,
        out_shape=[vt_shape(H * MLA_QK_PAD), jax.ShapeDtypeStruct((T, H * MLA_QK_PAD), BF16),
                   vt_shape(H * MLA_V_DIM)] + c_shape,
        compiler_params=_params("parallel"), name="mla_proj",
    )(x2, wt, row(mla_q_norm), row(mla_kv_norm), w_qnt, w_qpt, w_kn, w_vt, cos_a, sup_a, sdn_a, cos_t, sin_t,
      *mix_weights)

    c_in, c_out, c_shape = _slab_specs([w_ffn_down[0]], T // bm)
    qt_b, k_b, vt_b, w_fd = pl.pallas_call(
        _with_casts(functools.partial(_diff_proj_kernel, scale=LOG2_E * DIFF_QK_DIM ** -0.5), 9, 3, 1),
        grid=(T // bm,),
        in_specs=[tok(D), w_rows(o_dq, n_diff), w_rows(o_dq + n_diff, n_diff), w_rows(o_dq + 2 * n_diff, n_diff),
                  pos, pos, pos, pos_t, pos_t] + c_in,
        out_specs=[vt_spec(n_diff), tok(n_diff), vt_spec(n_diff)] + c_out,
        out_shape=[vt_shape(n_diff), jax.ShapeDtypeStruct((T, n_diff), BF16), vt_shape(n_diff)] + c_shape,
        compiler_params=_params("parallel"), name="diff_proj",
    )(x2, wt, wt, wt, cos_b, sup_b, sdn_b, cos_t, sin_t, w_ffn_down[0])

    bmg = 512
    sig = pl.pallas_call(
        _gate_proj_kernel,
        grid=(T // bmg,),
        in_specs=[pl.BlockSpec((bmg, D), lambda i: (i, 0)), w_rows(o_dq + 3 * n_diff, 2 * D)],
        out_specs=pl.BlockSpec((bmg, 2 * D), lambda i: (i, 0)),
        out_shape=jax.ShapeDtypeStruct((T, 2 * D), BF16),
        compiler_params=_params("parallel"), name="gate_proj",
    )(x2, wt)

    tq = bm
    hp = 4
    head_blk = lambda rows, width, full: pl.BlockSpec(
        (None, rows, hp * width), (lambda b, h, i: (b, 0, h)) if full else (lambda b, h, i: (b, i, h)))
    vt_blk = lambda dv: pl.BlockSpec((None, S // tq, hp, dv, tq), lambda b, h, i: (b, 0, h, 0, 0))
    qt_blk = lambda dk: pl.BlockSpec((None, None, hp, dk, tq), lambda b, h, i: (b, i, h, 0, 0))
    attn_a = pl.pallas_call(
        functools.partial(_mla_attn_kernel, tq=tq, n_heads=hp),
        grid=(B, H // hp, S // tq),
        in_specs=[qt_blk(MLA_QK_PAD), head_blk(S, MLA_QK_PAD, True), vt_blk(MLA_V_DIM)],
        out_specs=head_blk(tq, MLA_V_DIM, False),
        out_shape=jax.ShapeDtypeStruct((B, S, H * MLA_V_DIM), BF16),
        scratch_shapes=hp * [pltpu.VMEM((MLA_QK_PAD, tq), BF16)] + _flash_scratch(hp, MLA_V_DIM, tq, tq),
        compiler_params=_params("parallel", "parallel", "parallel"), name="mla_attn",
    )(qt_a.reshape(B, S // tq, H, MLA_QK_PAD, tq), k_a.reshape(B, S, -1),
      vt_a.reshape(B, S // tq, H, MLA_V_DIM, tq))

    lam_spec = pl.BlockSpec((1, DIFF_QK_DIM), lambda b, h, i: (0, 0))
    attn_b = pl.pallas_call(
        functools.partial(_diff_attn_kernel, tq=tq, n_heads=hp, lambda_init=lambda_init),
        grid=(B, DIFF_HEADS // hp, S // tq),
        in_specs=[qt_blk(LANES), head_blk(S, LANES, True), vt_blk(DIFF_V_DIM),
                  lam_spec, lam_spec, lam_spec, lam_spec, pl.BlockSpec((1, DIFF_V_DIM), lambda b, h, i: (0, 0))],
        out_specs=head_blk(tq, DIFF_V_DIM, False),
        out_shape=jax.ShapeDtypeStruct((B, S, DIFF_HEADS * DIFF_V_DIM), BF16),
        scratch_shapes=hp * [pltpu.VMEM((LANES, 2 * tq), BF16)] + _flash_scratch(hp, DIFF_V_DIM, 2 * tq, tq),
        compiler_params=_params("parallel", "parallel", "parallel"), name="diff_attn",
    )(qt_b.reshape(B, S // tq, DIFF_HEADS, LANES, tq), k_b.reshape(B, S, -1),
      vt_b.reshape(B, S // tq, DIFF_HEADS, DIFF_V_DIM, tq),
      row(diff_lambda_q1), row(diff_lambda_k1), row(diff_lambda_q2), row(diff_lambda_k2), row(diff_subln))

    bmo = 256
    tokm = lambda n, j=0: pl.BlockSpec((bmo, n), lambda i: (i, j))
    c_in, c_out, c_shape = _slab_specs([w_ffn_in[0]], T // bmo)
    h1, w_fi = pl.pallas_call(
        _with_casts(_mix_out_kernel, 10, 1, 1),
        grid=(T // bmo,),
        in_specs=[tokm(w_a.shape[0]), tokm(w_b.shape[0]), tokm(D, 0), tokm(D, 1), tokm(D),
                  _const_spec(w_a.shape), _const_spec(w_b.shape), _const_spec(w_o.shape),
                  _const_spec((1, D)), _const_spec((1, D))] + c_in,
        out_specs=[tokm(D)] + c_out,
        out_shape=[jax.ShapeDtypeStruct((T, D), F32)] + c_shape,
        compiler_params=_params("parallel"), name="mix_out",
    )(attn_a.reshape(T, -1), attn_b.reshape(T, -1), sig, sig, x2, w_a, w_b, w_o, row(ln1_g), row(ln1_b), w_ffn_in[0])

    bmf, tf = 512, 512
    n_f = d_ff // tf
    out = pl.pallas_call(
        _ffn_kernel,
        grid=(T // bmf, n_f),
        in_specs=[pl.BlockSpec((bmf, D), lambda i, f: (i, 0)),
                  pl.BlockSpec((D, tf), lambda i, f: (0, f)),
                  pl.BlockSpec((D, tf), lambda i, f: (0, n_f + f)),
                  pl.BlockSpec((tf, D), lambda i, f: (f, 0)),
                  pl.BlockSpec((1, D), lambda i, f: (0, 0)), pl.BlockSpec((1, D), lambda i, f: (0, 0))],
        out_specs=pl.BlockSpec((bmf, D), lambda i, f: (i, 0)),
        out_shape=jax.ShapeDtypeStruct((T, D), F32),
        scratch_shapes=[pltpu.VMEM((bmf, D), BF16), pltpu.VMEM((bmf, D), F32)],
        compiler_params=_params("parallel", "arbitrary"), name="ffn",
    )(h1, w_fi, w_fi, w_fd, row(ln2_g), row(ln2_b))

    return out.reshape(B, S, D)
```

```python
import functools
import math

import jax
import jax.numpy as jnp
from jax import lax
from jax.experimental import pallas as pl
from jax.experimental.pallas import tpu as pltpu

MLA_HEADS = 8
MLA_NOPE_DIM = 128
MLA_ROPE_DIM = 64
MLA_V_DIM = 128
MLA_Q_RANK = 512
MLA_KV_RANK = 512
DIFF_HEADS = 8
DIFF_QK_DIM = 64
DIFF_V_DIM = 128
ROPE_THETA = 10000.0
DEPTH = 1
ALPHA = (2 * DEPTH) ** 0.25
RMS_EPS = 1e-6
SUBLN_EPS = 1e-5
LN_EPS = 1e-5

LOG2_E = math.log2(math.e)
LANES = 128
MLA_QK_PAD = 256
SUM_ROWS = 16
VMEM_LIMIT = 56 * 1024 * 1024

BF16 = jnp.bfloat16
F32 = jnp.float32


def _dot(a, b):
    return jnp.dot(a, b, preferred_element_type=F32)


def _dot_nt(a, b):
    return lax.dot_general(a, b, (((1,), (1,)), ((), ())), preferred_element_type=F32)


def _rms(x, g, eps):
    return x * lax.rsqrt(jnp.mean(x * x, axis=-1, keepdims=True) + eps) * g


def _layer_norm(x, g, b):
    mu = jnp.mean(x, axis=-1, keepdims=True)
    xc = x - mu
    var = jnp.mean(xc * xc, axis=-1, keepdims=True)
    return xc * lax.rsqrt(var + LN_EPS) * g + b


def _rope(x, cos, sin_signed):
    up = pltpu.roll(x, LANES - 32, 1)
    dn = pltpu.roll(x, 32, 1)
    first_half = lax.broadcasted_iota(jnp.int32, x.shape, 1) % 64 < 32
    return x * cos + jnp.where(first_half, up, dn) * sin_signed


def _sigmoid(x):
    return 1.0 / (1.0 + jnp.exp(-x))


def _const_spec(shape):
    return pl.BlockSpec(shape, lambda *_: (0,) * len(shape), pipeline_mode=pl.Buffered(1))


def _params(*sem):
    return pltpu.CompilerParams(dimension_semantics=sem, vmem_limit_bytes=VMEM_LIMIT)


def _with_casts(body, n_in, n_out, n_cast):
    def kernel(*refs):
        ins, refs = refs[:n_in], refs[n_in:]
        slabs_in, refs = refs[:n_cast], refs[n_cast:]
        outs, refs = refs[:n_out], refs[n_out:]
        slabs_out, scratch = refs[:n_cast], refs[n_cast:]
        for src, dst in zip(slabs_in, slabs_out):
            dst[...] = src[...].astype(BF16)
        body(*ins, *outs, *scratch)
    return kernel


def _slab_specs(arrays, steps):
    ins, outs, shapes = [], [], []
    for a in arrays:
        rows, width = a.shape
        spec = pl.BlockSpec((rows // steps, width), lambda i: (i, 0))
        ins.append(spec)
        outs.append(spec)
        shapes.append(jax.ShapeDtypeStruct(a.shape, BF16))
    return ins, outs, shapes


def _rope_rows(x, cos_t, sin_t):
    half = cos_t.shape[0]
    out = []
    for j in range(x.shape[0] // (2 * half)):
        x1 = x[2 * half * j:2 * half * j + half]
        x2 = x[2 * half * j + half:2 * half * (j + 1)]
        out += [x1 * cos_t - x2 * sin_t, x2 * cos_t + x1 * sin_t]
    return jnp.concatenate(out, axis=0)


def _mla_proj_kernel(x_ref, w_lat_ref, gq_ref, gkv_ref, w_qnt_ref, w_qpt_ref, w_kn_ref, w_vt_ref,
                     cos_ref, sin_ref, cost_ref, sint_ref, qt_ref, k_ref, vt_ref, w_lat_bf_ref, *, scale):
    @pl.when(pl.program_id(0) == 0)
    def _():
        w_lat_bf_ref[...] = w_lat_ref[...].astype(BF16)

    xb = x_ref[...].astype(BF16)
    lat = _dot_nt(xb, w_lat_bf_ref[...])
    cq = _rms(lat[:, :MLA_Q_RANK], gq_ref[...], RMS_EPS).astype(BF16)
    ckv = _rms(lat[:, MLA_Q_RANK:MLA_Q_RANK + MLA_KV_RANK], gkv_ref[...], RMS_EPS).astype(BF16)
    kr = lat[:, MLA_Q_RANK + MLA_KV_RANK:]
    kr = jnp.where(lax.broadcasted_iota(jnp.int32, kr.shape, 1) < MLA_ROPE_DIM, kr, 0.0)
    k_pe = _rope(kr, cos_ref[...], sin_ref[...]).astype(BF16)
    kn = _dot(ckv, w_kn_ref[...])
    vt_ref[...] = _dot_nt(w_vt_ref[...], ckv).astype(BF16)
    qnt = _dot_nt(w_qnt_ref[...], cq)
    qpt = _dot_nt(w_qpt_ref[...], cq)
    cos_t, sin_t = cost_ref[...], sint_ref[...]
    pad = jnp.zeros((MLA_QK_PAD - MLA_NOPE_DIM - MLA_ROPE_DIM, qnt.shape[1]), BF16)
    for h in range(MLA_HEADS):
        base = h * MLA_QK_PAD
        qt_ref[base:base + MLA_NOPE_DIM, :] = (qnt[h * MLA_NOPE_DIM:(h + 1) * MLA_NOPE_DIM] * scale).astype(BF16)
        q_pe = _rope_rows(qpt[h * MLA_ROPE_DIM:(h + 1) * MLA_ROPE_DIM], cos_t, sin_t)
        qt_ref[base + MLA_NOPE_DIM:base + MLA_NOPE_DIM + MLA_ROPE_DIM, :] = (q_pe * scale).astype(BF16)
        qt_ref[base + MLA_NOPE_DIM + MLA_ROPE_DIM:base + MLA_QK_PAD, :] = pad
        k_ref[:, base:base + LANES] = kn[:, h * LANES:(h + 1) * LANES].astype(BF16)
        k_ref[:, base + LANES:base + 2 * LANES] = k_pe


def _diff_proj_kernel(x_ref, w_q_ref, w_k_ref, w_vt_ref, cos_ref, sin_ref, cost_ref, sint_ref,
                      qt_ref, k_ref, vt_ref, *, scale):
    xb = x_ref[...].astype(BF16)
    yk = _dot_nt(xb, w_k_ref[...])
    vt_ref[...] = _dot_nt(w_vt_ref[...], xb).astype(BF16)
    qt = _dot_nt(w_q_ref[...], xb)
    qt_ref[...] = (_rope_rows(qt, cost_ref[...], sint_ref[...]) * scale).astype(BF16)
    cos, sin = cos_ref[...], sin_ref[...]
    for h in range(DIFF_HEADS):
        lo, hi = h * LANES, (h + 1) * LANES
        k_ref[:, lo:hi] = _rope(yk[:, lo:hi], cos, sin).astype(BF16)


def _gate_proj_kernel(x_ref, w_ref, o_ref):
    xb = x_ref[...].astype(BF16)
    o_ref[...] = _sigmoid(_dot_nt(xb, w_ref[...])).astype(BF16)


def _flash_cols(qts, k_ref, vt_ref, scratch, q_tile, tq):
    n_heads = len(qts)
    per_head = len(scratch) // n_heads
    tk = tq // 2
    dk, cols = qts[0].shape
    dv = vt_ref.shape[2]
    ones = jnp.ones((SUM_ROWS, tk), BF16)

    class Head:
        def __init__(self, h):
            (s0, s1, p0, p1, self.m, a0, a1, self.acc) = scratch[h * per_head:(h + 1) * per_head]
            self.qt = qts[h]
            self.s, self.p, self.a = (s0, s1), (p0, p1), (a0, a1)
            self.lanes = slice(h * dk, (h + 1) * dk)
            self.h = h

    heads = [Head(h) for h in range(n_heads)]
    for hd in heads:
        hd.m[...] = jnp.full(hd.m.shape, -jnp.inf, F32)
        hd.acc[...] = jnp.zeros(hd.acc.shape, F32)
        hd.p[1][...] = jnp.zeros(hd.p[1].shape, BF16)
        hd.a[1][...] = jnp.ones(hd.a[1].shape, F32)

    all_groups = [(0, cols // LANES)]
    late_groups = [((m * tq + tk) // LANES, (m + 1) * tq // LANES) for m in range(cols // tq)]

    def scores(hd, c, half, runs=all_groups):
        k = k_ref[pl.ds(pl.multiple_of(c * tk, tk), tk), hd.lanes]
        for g0, g1 in runs:
            s = _dot(k, hd.qt[:, g0 * LANES:g1 * LANES])
            for g in range(g0, g1):
                hd.s[half][g] = s[:, (g - g0) * LANES:(g - g0 + 1) * LANES]

    def softmax(hd, half, diagonal, runs=all_groups):
        for g in [g for g0, g1 in runs for g in range(g0, g1)]:
            sl = slice(g * LANES, (g + 1) * LANES)
            s = hd.s[half][g]
            if diagonal:
                k_pos = half * tk + lax.broadcasted_iota(jnp.int32, (tk, LANES), 0)
                q_pos = (g * LANES) % tq + lax.broadcasted_iota(jnp.int32, (tk, LANES), 1)
                s = jnp.where(k_pos <= q_pos, s, -jnp.inf)
            m_old = hd.m[:, sl]
            m_new = jnp.maximum(m_old, jnp.max(s, axis=0, keepdims=True))
            a = jnp.exp2(m_old - m_new)
            p = jnp.exp2(s - m_new)
            hd.m[:, sl] = m_new
            hd.a[half][:, sl] = a
            hd.p[half][g] = p.astype(BF16)

    def weighted_values(hd, pair, half, runs=all_groups):
        vt = jnp.concatenate([vt_ref[pair, hd.h][:, half * tk:(half + 1) * tk], ones], axis=0)
        for g0, g1 in runs:
            sl = slice(g0 * LANES, g1 * LANES)
            p = jnp.concatenate([hd.p[half][g] for g in range(g0, g1)], axis=1)
            hd.acc[:, sl] = hd.a[half][:, sl] * hd.acc[:, sl] + _dot(vt, p)

    def pair_step(i, diagonal):
        second = late_groups if diagonal else all_groups
        for hd in heads:
            weighted_values(hd, jnp.maximum(i - 1, 0), 1)
        for hd in heads:
            scores(hd, 2 * i + 1, 1, second)
        for hd in heads:
            softmax(hd, 0, diagonal)
        if not diagonal:
            for hd in heads:
                scores(hd, 2 * i + 2, 0)
        for hd in heads:
            weighted_values(hd, i, 0)
        for hd in heads:
            softmax(hd, 1, diagonal, second)

    for hd in heads:
        scores(hd, 0, 0)
    lax.fori_loop(0, q_tile, lambda i, c: (pair_step(i, False), c)[1], 0)
    pair_step(q_tile, True)
    for hd in heads:
        weighted_values(hd, q_tile, 1, late_groups)
    return [(hd.acc[:dv, :], hd.acc[dv:dv + 1, :]) for hd in heads]


def _flash_scratch(n_heads, dv, cols, tq):
    tk = tq // 2
    stat = pltpu.VMEM((1, cols), F32)
    group_major = lambda dtype: pltpu.VMEM((cols // LANES, tk, LANES), dtype)
    return n_heads * [group_major(F32), group_major(F32), group_major(BF16), group_major(BF16),
                      stat, stat, stat, pltpu.VMEM((dv + SUM_ROWS, cols), F32)]


def _mla_attn_kernel(qt_ref, k_ref, vt_ref, o_ref, *scratch, tq, n_heads):
    qts, scratch = scratch[:n_heads], scratch[n_heads:]
    for h in range(n_heads):
        qts[h][...] = qt_ref[h]
    for h, (acc, l) in enumerate(_flash_cols(qts, k_ref, vt_ref, scratch, pl.program_id(2), tq)):
        o_ref[:, h * MLA_V_DIM:(h + 1) * MLA_V_DIM] = (acc / l).T.astype(o_ref.dtype)


def _diff_attn_kernel(qt_ref, k_ref, vt_ref, lq1_ref, lk1_ref, lq2_ref, lk2_ref, g_ref, o_ref, *scratch,
                      tq, n_heads, lambda_init):
    qt2_refs, scratch = scratch[:n_heads], scratch[n_heads:]
    row = lax.broadcasted_iota(jnp.int32, (LANES, tq), 0)
    zero = jnp.zeros((LANES, tq), BF16)
    for h in range(n_heads):
        qt = qt_ref[h]
        qt2_refs[h][:, :tq] = jnp.where(row < DIFF_QK_DIM, qt, zero)
        qt2_refs[h][:, tq:] = jnp.where(row >= DIFF_QK_DIM, qt, zero)
    lam = (jnp.exp(jnp.sum(lq1_ref[...] * lk1_ref[...], axis=-1, keepdims=True))
           - jnp.exp(jnp.sum(lq2_ref[...] * lk2_ref[...], axis=-1, keepdims=True)) + lambda_init)
    for h, (acc, l) in enumerate(_flash_cols(qt2_refs, k_ref, vt_ref, scratch, pl.program_id(2), tq)):
        o = acc / l
        o = (o[:, :tq] - lam * o[:, tq:]).T
        o_ref[:, h * DIFF_V_DIM:(h + 1) * DIFF_V_DIM] = (
            _rms(o, g_ref[...], SUBLN_EPS) * (1.0 - lambda_init)).astype(o_ref.dtype)


def _mix_out_kernel(a_ref, b_ref, sa_ref, sb_ref, x_ref, wa_ref, wb_ref, wo_ref, g_ref, beta_ref, o_ref):
    ya = _dot(a_ref[...], wa_ref[...])
    yb = _dot(b_ref[...], wb_ref[...])
    m = (sa_ref[...].astype(F32) * ya + sb_ref[...].astype(F32) * yb).astype(BF16)
    mixed = _dot(m, wo_ref[...])
    o_ref[...] = _layer_norm(ALPHA * x_ref[...] + mixed, g_ref[...], beta_ref[...])


def _ffn_kernel(h_ref, wg_ref, wu_ref, wd_ref, g_ref, beta_ref, o_ref, hb_ref, acc_ref):
    f = pl.program_id(1)

    @pl.when(f == 0)
    def _():
        hb_ref[...] = h_ref[...].astype(BF16)
        acc_ref[...] = jnp.zeros_like(acc_ref)

    hb = hb_ref[...]
    gate = _dot(hb, wg_ref[...])
    up = _dot(hb, wu_ref[...])
    act = (gate * _sigmoid(gate) * up).astype(BF16)
    acc_ref[...] += _dot(act, wd_ref[...])

    @pl.when(f == pl.num_programs(1) - 1)
    def _():
        o_ref[...] = _layer_norm(ALPHA * h_ref[...] + acc_ref[...], g_ref[...], beta_ref[...])


def _rope_angles(seq, dim):
    inv_freq = 1.0 / (ROPE_THETA ** (jnp.arange(0, dim, 2, dtype=F32) / dim))
    ang = jnp.arange(seq, dtype=F32)[:, None] * inv_freq[None, :]
    return jnp.cos(ang), jnp.sin(ang)


def kernel(x, w_in, mla_q_norm, mla_w_uq, mla_kv_norm, mla_w_ukv, diff_lambda_q1, diff_lambda_k1,
           diff_lambda_q2, diff_lambda_k2, diff_subln, w_branch_a, w_branch_b, w_out, ln1_g, ln1_b,
           w_ffn_in, w_ffn_down, ln2_g, ln2_b):
    B, S, D = x.shape
    T = B * S
    H = MLA_HEADS
    d_ff = w_ffn_down.shape[1]
    lambda_init = 0.8 - 0.6 * math.exp(-0.3 * 0)
    x2 = x.reshape(T, D)

    w = w_in[0]
    n_diff = DIFF_HEADS * 2 * DIFF_QK_DIM
    o_kr = MLA_Q_RANK + MLA_KV_RANK
    o_dq = o_kr + MLA_ROPE_DIM
    wt = jnp.swapaxes(w, 0, 1)
    w_rows = lambda start, n: pl.BlockSpec((pl.Element(n), pl.Element(D)), lambda *_: (start, 0),
                                           pipeline_mode=pl.Buffered(1))
    uq = mla_w_uq[0].reshape(MLA_Q_RANK, H, MLA_NOPE_DIM + MLA_ROPE_DIM)
    w_qnt = uq[:, :, :MLA_NOPE_DIM].reshape(MLA_Q_RANK, H * MLA_NOPE_DIM).T.astype(BF16)
    w_qpt = uq[:, :, MLA_NOPE_DIM:].reshape(MLA_Q_RANK, H * MLA_ROPE_DIM).T.astype(BF16)
    ukv = mla_w_ukv[0].reshape(MLA_KV_RANK, H, MLA_NOPE_DIM + MLA_V_DIM)
    w_kn = ukv[:, :, :MLA_NOPE_DIM].reshape(MLA_KV_RANK, H * LANES).astype(BF16)
    w_vt = ukv[:, :, MLA_NOPE_DIM:].reshape(MLA_KV_RANK, H * MLA_V_DIM).T.astype(BF16)
    row = lambda v: v.reshape(1, -1).astype(F32)

    assert MLA_ROPE_DIM == DIFF_QK_DIM
    cos, sin = _rope_angles(S, MLA_ROPE_DIM)
    cos_k, sin_k = jnp.tile(cos, (1, 4)), jnp.tile(jnp.concatenate([-sin, sin], axis=1), (1, 2))
    cos_t, sin_t = cos.T, sin.T

    bm = 512
    n_pos = S // bm
    tok = lambda n: pl.BlockSpec((bm, n), lambda i: (i, 0))
    pos = pl.BlockSpec((bm, LANES), lambda i: (i % n_pos, 0))
    vt_spec = lambda n: pl.BlockSpec((None, n, bm), lambda i: (i, 0, 0))
    vt_shape = lambda n: jax.ShapeDtypeStruct((T // bm, n, bm), BF16)
    pos_t = pl.BlockSpec((cos_t.shape[0], bm), lambda i: (0, i % n_pos))
    n_steps = T // bm
    rest_rows = (wt.shape[0] - o_dq) // n_steps
    qt_a, k_a, vt_a, w_rest = pl.pallas_call(
        _with_casts(functools.partial(_mla_proj_kernel, scale=LOG2_E * (MLA_NOPE_DIM + MLA_ROPE_DIM) ** -0.5),
                    12, 3, 1),
        grid=(n_steps,),
        in_specs=[tok(D), w_rows(0, o_kr + LANES), _const_spec((1, MLA_Q_RANK)), _const_spec((1, MLA_KV_RANK)),
                  _const_spec(w_qnt.shape), _const_spec(w_qpt.shape), _const_spec(w_kn.shape),
                  _const_spec(w_vt.shape), pos, pos, pos_t, pos_t,
                  pl.BlockSpec((pl.Element(rest_rows), pl.Element(D)),
                               lambda i: ((o_dq // 64 + rest_rows // 64 * i) * 64, 0))],
        out_specs=[vt_spec(H * MLA_QK_PAD), tok(H * MLA_QK_PAD), vt_spec(H * MLA_V_DIM),
                   pl.BlockSpec((rest_rows, D), lambda i: (i, 0))],
        out_shape=[vt_shape(H * MLA_QK_PAD), jax.ShapeDtypeStruct((T, H * MLA_QK_PAD), BF16),
                   vt_shape(H * MLA_V_DIM), jax.ShapeDtypeStruct((wt.shape[0] - o_dq, D), BF16)],
        scratch_shapes=[pltpu.VMEM((o_kr + LANES, D), BF16)],
        compiler_params=_params("arbitrary"), name="mla_proj",
    )(x2, wt, row(mla_q_norm), row(mla_kv_norm), w_qnt, w_qpt, w_kn, w_vt, cos_k, sin_k, cos_t, sin_t, wt)

    c_in, c_out, c_shape = _slab_specs([w_ffn_down[0]], T // bm)
    qt_b, k_b, vt_b, w_fd = pl.pallas_call(
        _with_casts(functools.partial(_diff_proj_kernel, scale=LOG2_E * DIFF_QK_DIM ** -0.5), 8, 3, 1),
        grid=(T // bm,),
        in_specs=[tok(D), w_rows(0, n_diff), w_rows(n_diff, n_diff), w_rows(2 * n_diff, n_diff),
                  pos, pos, pos_t, pos_t] + c_in,
        out_specs=[vt_spec(n_diff), tok(n_diff), vt_spec(n_diff)] + c_out,
        out_shape=[vt_shape(n_diff), jax.ShapeDtypeStruct((T, n_diff), BF16), vt_shape(n_diff)] + c_shape,
        compiler_params=_params("parallel"), name="diff_proj",
    )(x2, w_rest, w_rest, w_rest, cos_k, sin_k, cos_t, sin_t, w_ffn_down[0])

    bmg = 512
    mix_weights = [w_branch_a[0], w_branch_b[0], w_out[0]]
    c_in, c_out, c_shape = _slab_specs(mix_weights, T // bmg)
    sig, w_a, w_b, w_o = pl.pallas_call(
        _with_casts(_gate_proj_kernel, 2, 1, len(mix_weights)),
        grid=(T // bmg,),
        in_specs=[pl.BlockSpec((bmg, D), lambda i: (i, 0)), w_rows(3 * n_diff, 2 * D)] + c_in,
        out_specs=[pl.BlockSpec((bmg, 2 * D), lambda i: (i, 0))] + c_out,
        out_shape=[jax.ShapeDtypeStruct((T, 2 * D), BF16)] + c_shape,
        compiler_params=_params("parallel"), name="gate_proj",
    )(x2, w_rest, *mix_weights)

    tq = bm
    hp = 4
    head_blk = lambda rows, width, full: pl.BlockSpec(
        (None, rows, hp * width), (lambda b, h, i: (b, 0, h)) if full else (lambda b, h, i: (b, i, h)))
    vt_blk = lambda dv: pl.BlockSpec((None, S // tq, hp, dv, tq), lambda b, h, i: (b, 0, h, 0, 0))
    qt_blk = lambda dk: pl.BlockSpec((None, None, hp, dk, tq), lambda b, h, i: (b, i, h, 0, 0))
    attn_a = pl.pallas_call(
        functools.partial(_mla_attn_kernel, tq=tq, n_heads=hp),
        grid=(B, H // hp, S // tq),
        in_specs=[qt_blk(MLA_QK_PAD), head_blk(S, MLA_QK_PAD, True), vt_blk(MLA_V_DIM)],
        out_specs=head_blk(tq, MLA_V_DIM, False),
        out_shape=jax.ShapeDtypeStruct((B, S, H * MLA_V_DIM), BF16),
        scratch_shapes=hp * [pltpu.VMEM((MLA_QK_PAD, tq), BF16)] + _flash_scratch(hp, MLA_V_DIM, tq, tq),
        compiler_params=_params("parallel", "parallel", "parallel"), name="mla_attn",
    )(qt_a.reshape(B, S // tq, H, MLA_QK_PAD, tq), k_a.reshape(B, S, -1),
      vt_a.reshape(B, S // tq, H, MLA_V_DIM, tq))

    lam_spec = pl.BlockSpec((1, DIFF_QK_DIM), lambda b, h, i: (0, 0))
    attn_b = pl.pallas_call(
        functools.partial(_diff_attn_kernel, tq=tq, n_heads=hp, lambda_init=lambda_init),
        grid=(B, DIFF_HEADS // hp, S // tq),
        in_specs=[qt_blk(LANES), head_blk(S, LANES, True), vt_blk(DIFF_V_DIM),
                  lam_spec, lam_spec, lam_spec, lam_spec, pl.BlockSpec((1, DIFF_V_DIM), lambda b, h, i: (0, 0))],
        out_specs=head_blk(tq, DIFF_V_DIM, False),
        out_shape=jax.ShapeDtypeStruct((B, S, DIFF_HEADS * DIFF_V_DIM), BF16),
        scratch_shapes=hp * [pltpu.VMEM((LANES, 2 * tq), BF16)] + _flash_scratch(hp, DIFF_V_DIM, 2 * tq, tq),
        compiler_params=_params("parallel", "parallel", "parallel"), name="diff_attn",
    )(qt_b.reshape(B, S // tq, DIFF_HEADS, LANES, tq), k_b.reshape(B, S, -1),
      vt_b.reshape(B, S // tq, DIFF_HEADS, DIFF_V_DIM, tq),
      row(diff_lambda_q1), row(diff_lambda_k1), row(diff_lambda_q2), row(diff_lambda_k2), row(diff_subln))

    bmo = 256
    tokm = lambda n, j=0: pl.BlockSpec((bmo, n), lambda i: (i, j))
    c_in, c_out, c_shape = _slab_specs([w_ffn_in[0]], T // bmo)
    h1, w_fi = pl.pallas_call(
        _with_casts(_mix_out_kernel, 10, 1, 1),
        grid=(T // bmo,),
        in_specs=[tokm(w_a.shape[0]), tokm(w_b.shape[0]), tokm(D, 0), tokm(D, 1), tokm(D),
                  _const_spec(w_a.shape), _const_spec(w_b.shape), _const_spec(w_o.shape),
                  _const_spec((1, D)), _const_spec((1, D))] + c_in,
        out_specs=[tokm(D)] + c_out,
        out_shape=[jax.ShapeDtypeStruct((T, D), F32)] + c_shape,
        compiler_params=_params("parallel"), name="mix_out",
    )(attn_a.reshape(T, -1), attn_b.reshape(T, -1), sig, sig, x2, w_a, w_b, w_o, row(ln1_g), row(ln1_b), w_ffn_in[0])

    bmf, tf = 512, 512
    n_f = d_ff // tf
    out = pl.pallas_call(
        _ffn_kernel,
        grid=(T // bmf, n_f),
        in_specs=[pl.BlockSpec((bmf, D), lambda i, f: (i, 0)),
                  pl.BlockSpec((D, tf), lambda i, f: (0, f)),
                  pl.BlockSpec((D, tf), lambda i, f: (0, n_f + f)),
                  pl.BlockSpec((tf, D), lambda i, f: (f, 0)),
                  pl.BlockSpec((1, D), lambda i, f: (0, 0)), pl.BlockSpec((1, D), lambda i, f: (0, 0))],
        out_specs=pl.BlockSpec((bmf, D), lambda i, f: (i, 0)),
        out_shape=jax.ShapeDtypeStruct((T, D), F32),
        scratch_shapes=[pltpu.VMEM((bmf, D), BF16), pltpu.VMEM((bmf, D), F32)],
        compiler_params=_params("parallel", "arbitrary"), name="ffn",
    )(h1, w_fi, w_fi, w_fd, row(ln2_g), row(ln2_b))

    return out.reshape(B, S, D)
```

```python
import functools
import math

import jax
import jax.numpy as jnp
from jax import lax
from jax.experimental import pallas as pl
from jax.experimental.pallas import tpu as pltpu

MLA_HEADS = 8
MLA_NOPE_DIM = 128
MLA_ROPE_DIM = 64
MLA_V_DIM = 128
MLA_Q_RANK = 512
MLA_KV_RANK = 512
DIFF_HEADS = 8
DIFF_QK_DIM = 64
DIFF_V_DIM = 128
ROPE_THETA = 10000.0
DEPTH = 1
ALPHA = (2 * DEPTH) ** 0.25
RMS_EPS = 1e-6
SUBLN_EPS = 1e-5
LN_EPS = 1e-5

LOG2_E = math.log2(math.e)
LANES = 128
MLA_QK_PAD = 256
SUM_ROWS = 16
ROPE_HALF = MLA_ROPE_DIM // 2
VMEM_LIMIT = 56 * 1024 * 1024

PROJ_ROWS = 512
GATE_ROWS = 512
MIX_ROWS = 256
FFN_ROWS = 512
FFN_COLS = 512
ATTN_HEADS_PER_STEP = 4

BF16 = jnp.bfloat16
F32 = jnp.float32


def _dot(a, b):
    return jnp.dot(a, b, preferred_element_type=F32)


def _dot_nt(a, b):
    return lax.dot_general(a, b, (((1,), (1,)), ((), ())), preferred_element_type=F32)


def _rms(x, g, eps):
    return x * lax.rsqrt(jnp.mean(x * x, axis=-1, keepdims=True) + eps) * g


def _layer_norm(x, g, b):
    mu = jnp.mean(x, axis=-1, keepdims=True)
    xc = x - mu
    var = jnp.mean(xc * xc, axis=-1, keepdims=True)
    return xc * lax.rsqrt(var + LN_EPS) * g + b


def _rope(x, cos, sin_signed):
    up = pltpu.roll(x, LANES - ROPE_HALF, 1)
    dn = pltpu.roll(x, ROPE_HALF, 1)
    first_half = lax.broadcasted_iota(jnp.int32, x.shape, 1) % (2 * ROPE_HALF) < ROPE_HALF
    return x * cos + jnp.where(first_half, up, dn) * sin_signed


def _sigmoid(x):
    return 1.0 / (1.0 + jnp.exp(-x))


def _const_spec(shape):
    return pl.BlockSpec(shape, lambda *_: (0,) * len(shape), pipeline_mode=pl.Buffered(1))


def _params(*sem):
    return pltpu.CompilerParams(dimension_semantics=sem, vmem_limit_bytes=VMEM_LIMIT)


def _with_casts(body, n_in, n_out, n_cast):
    def kernel(*refs):
        ins, refs = refs[:n_in], refs[n_in:]
        slabs_in, refs = refs[:n_cast], refs[n_cast:]
        outs, refs = refs[:n_out], refs[n_out:]
        slabs_out, scratch = refs[:n_cast], refs[n_cast:]
        for src, dst in zip(slabs_in, slabs_out):
            dst[...] = src[...].astype(BF16)
        body(*ins, *outs, *scratch)
    return kernel


def _slab_specs(arrays, steps):
    ins, outs, shapes = [], [], []
    for a in arrays:
        rows, width = a.shape
        spec = pl.BlockSpec((rows // steps, width), lambda i: (i, 0))
        ins.append(spec)
        outs.append(spec)
        shapes.append(jax.ShapeDtypeStruct(a.shape, BF16))
    return ins, outs, shapes


def _rope_rows(x, cos_t, sin_t):
    half = cos_t.shape[0]
    out = []
    for j in range(x.shape[0] // (2 * half)):
        x1 = x[2 * half * j:2 * half * j + half]
        x2 = x[2 * half * j + half:2 * half * (j + 1)]
        out += [x1 * cos_t - x2 * sin_t, x2 * cos_t + x1 * sin_t]
    return jnp.concatenate(out, axis=0)


def _mla_proj_kernel(x_ref, w_lat_ref, gq_ref, gkv_ref, w_qnt_ref, w_qpt_ref, w_kn_ref, w_vt_ref,
                     cos_ref, sin_ref, cost_ref, sint_ref, qt_ref, k_ref, vt_ref, w_lat_bf_ref, *, scale):
    @pl.when(pl.program_id(0) == 0)
    def _():
        w_lat_bf_ref[...] = w_lat_ref[...].astype(BF16)

    xb = x_ref[...].astype(BF16)
    lat = _dot_nt(xb, w_lat_bf_ref[...])
    cq = _rms(lat[:, :MLA_Q_RANK], gq_ref[...], RMS_EPS).astype(BF16)
    ckv = _rms(lat[:, MLA_Q_RANK:MLA_Q_RANK + MLA_KV_RANK], gkv_ref[...], RMS_EPS).astype(BF16)
    kr = lat[:, MLA_Q_RANK + MLA_KV_RANK:]
    kr = jnp.where(lax.broadcasted_iota(jnp.int32, kr.shape, 1) < MLA_ROPE_DIM, kr, 0.0)
    k_pe = _rope(kr, cos_ref[...], sin_ref[...]).astype(BF16)
    kn = _dot(ckv, w_kn_ref[...])
    vt_ref[...] = _dot_nt(w_vt_ref[...], ckv).astype(BF16)
    qnt = _dot_nt(w_qnt_ref[...], cq)
    qpt = _dot_nt(w_qpt_ref[...], cq)
    cos_t, sin_t = cost_ref[...], sint_ref[...]
    pad = jnp.zeros((MLA_QK_PAD - MLA_NOPE_DIM - MLA_ROPE_DIM, qnt.shape[1]), BF16)
    for h in range(MLA_HEADS):
        base = h * MLA_QK_PAD
        qt_ref[base:base + MLA_NOPE_DIM, :] = (qnt[h * MLA_NOPE_DIM:(h + 1) * MLA_NOPE_DIM] * scale).astype(BF16)
        q_pe = _rope_rows(qpt[h * MLA_ROPE_DIM:(h + 1) * MLA_ROPE_DIM], cos_t, sin_t)
        qt_ref[base + MLA_NOPE_DIM:base + MLA_NOPE_DIM + MLA_ROPE_DIM, :] = (q_pe * scale).astype(BF16)
        qt_ref[base + MLA_NOPE_DIM + MLA_ROPE_DIM:base + MLA_QK_PAD, :] = pad
        k_ref[:, base:base + LANES] = kn[:, h * LANES:(h + 1) * LANES].astype(BF16)
        k_ref[:, base + LANES:base + 2 * LANES] = k_pe


def _diff_proj_kernel(x_ref, w_q_ref, w_k_ref, w_vt_ref, cos_ref, sin_ref, cost_ref, sint_ref,
                      qt_ref, k_ref, vt_ref, *, scale):
    xb = x_ref[...].astype(BF16)
    yk = _dot_nt(xb, w_k_ref[...])
    vt_ref[...] = _dot_nt(w_vt_ref[...], xb).astype(BF16)
    qt = _dot_nt(w_q_ref[...], xb)
    qt_ref[...] = (_rope_rows(qt, cost_ref[...], sint_ref[...]) * scale).astype(BF16)
    cos, sin = cos_ref[...], sin_ref[...]
    for h in range(DIFF_HEADS):
        lo, hi = h * LANES, (h + 1) * LANES
        k_ref[:, lo:hi] = _rope(yk[:, lo:hi], cos, sin).astype(BF16)


def _gate_proj_kernel(x_ref, w_ref, o_ref):
    xb = x_ref[...].astype(BF16)
    o_ref[...] = _sigmoid(_dot_nt(xb, w_ref[...])).astype(BF16)


def _flash_cols(qts, k_ref, vt_ref, scratch, q_tile, tq):
    n_heads = len(qts)
    per_head = len(scratch) // n_heads
    tk = tq // 2
    dk, cols = qts[0].shape
    dv = vt_ref.shape[2]
    ones = jnp.ones((SUM_ROWS, tk), BF16)

    class Head:
        def __init__(self, h):
            (s0, s1, p0, p1, self.m, a0, a1, self.acc) = scratch[h * per_head:(h + 1) * per_head]
            self.qt = qts[h]
            self.s, self.p, self.a = (s0, s1), (p0, p1), (a0, a1)
            self.lanes = slice(h * dk, (h + 1) * dk)
            self.h = h

    heads = [Head(h) for h in range(n_heads)]
    for hd in heads:
        hd.m[...] = jnp.full(hd.m.shape, -jnp.inf, F32)
        hd.acc[...] = jnp.zeros(hd.acc.shape, F32)
        hd.p[1][...] = jnp.zeros(hd.p[1].shape, BF16)
        hd.a[1][...] = jnp.ones(hd.a[1].shape, F32)

    all_groups = [(0, cols // LANES)]
    late_groups = [((m * tq + tk) // LANES, (m + 1) * tq // LANES) for m in range(cols // tq)]

    def scores(hd, c, half, runs=all_groups):
        k = k_ref[pl.ds(pl.multiple_of(c * tk, tk), tk), hd.lanes]
        for g0, g1 in runs:
            s = _dot(k, hd.qt[:, g0 * LANES:g1 * LANES])
            for g in range(g0, g1):
                hd.s[half][g] = s[:, (g - g0) * LANES:(g - g0 + 1) * LANES]

    def softmax(hd, half, diagonal, runs=all_groups):
        for g in [g for g0, g1 in runs for g in range(g0, g1)]:
            sl = slice(g * LANES, (g + 1) * LANES)
            s = hd.s[half][g]
            if diagonal:
                k_pos = half * tk + lax.broadcasted_iota(jnp.int32, (tk, LANES), 0)
                q_pos = (g * LANES) % tq + lax.broadcasted_iota(jnp.int32, (tk, LANES), 1)
                s = jnp.where(k_pos <= q_pos, s, -jnp.inf)
            m_old = hd.m[:, sl]
            m_new = jnp.maximum(m_old, jnp.max(s, axis=0, keepdims=True))
            a = jnp.exp2(m_old - m_new)
            p = jnp.exp2(s - m_new)
            hd.m[:, sl] = m_new
            hd.a[half][:, sl] = a
            hd.p[half][g] = p.astype(BF16)

    def weighted_values(hd, pair, half, runs=all_groups):
        vt = jnp.concatenate([vt_ref[pair, hd.h][:, half * tk:(half + 1) * tk], ones], axis=0)
        for g0, g1 in runs:
            p = jnp.concatenate([hd.p[half][g] for g in range(g0, g1)], axis=1)
            pv = _dot(vt, p)
            for g in range(g0, g1):
                gl = slice(g * LANES, (g + 1) * LANES)
                hd.acc[g] = hd.a[half][:, gl] * hd.acc[g] + pv[:, (g - g0) * LANES:(g - g0 + 1) * LANES]

    def pair_step(i, diagonal):
        second = late_groups if diagonal else all_groups
        for hd in heads:
            weighted_values(hd, jnp.maximum(i - 1, 0), 1)
        for hd in heads:
            scores(hd, 2 * i + 1, 1, second)
        for hd in heads:
            softmax(hd, 0, diagonal)
        if not diagonal:
            for hd in heads:
                scores(hd, 2 * i + 2, 0)
        for hd in heads:
            weighted_values(hd, i, 0)
        for hd in heads:
            softmax(hd, 1, diagonal, second)

    for hd in heads:
        scores(hd, 0, 0)
    lax.fori_loop(0, q_tile, lambda i, c: (pair_step(i, False), c)[1], 0)
    pair_step(q_tile, True)
    for hd in heads:
        weighted_values(hd, q_tile, 1, late_groups)
    out = []
    for hd in heads:
        acc = jnp.concatenate([hd.acc[g] for g in range(cols // LANES)], axis=1)
        out.append((acc[:dv], acc[dv:dv + 1]))
    return out


def _flash_scratch(n_heads, dv, cols, tq):
    tk = tq // 2
    stat = pltpu.VMEM((1, cols), F32)
    group_major = lambda rows, dtype: pltpu.VMEM((cols // LANES, rows, LANES), dtype)
    return n_heads * [group_major(tk, F32), group_major(tk, F32), group_major(tk, BF16), group_major(tk, BF16),
                      stat, stat, stat, group_major(dv + SUM_ROWS, F32)]


def _mla_attn_kernel(qt_ref, k_ref, vt_ref, o_ref, *scratch, tq, n_heads):
    qts, scratch = scratch[:n_heads], scratch[n_heads:]
    for h in range(n_heads):
        qts[h][...] = qt_ref[h]
    for h, (acc, l) in enumerate(_flash_cols(qts, k_ref, vt_ref, scratch, pl.program_id(2), tq)):
        o_ref[:, h * MLA_V_DIM:(h + 1) * MLA_V_DIM] = (acc / l).T.astype(o_ref.dtype)


def _diff_attn_kernel(qt_ref, k_ref, vt_ref, lq1_ref, lk1_ref, lq2_ref, lk2_ref, g_ref, o_ref, *scratch,
                      tq, n_heads, lambda_init):
    qt2_refs, scratch = scratch[:n_heads], scratch[n_heads:]
    row = lax.broadcasted_iota(jnp.int32, (LANES, tq), 0)
    zero = jnp.zeros((LANES, tq), BF16)
    for h in range(n_heads):
        qt = qt_ref[h]
        qt2_refs[h][:, :tq] = jnp.where(row < DIFF_QK_DIM, qt, zero)
        qt2_refs[h][:, tq:] = jnp.where(row >= DIFF_QK_DIM, qt, zero)
    lam = (jnp.exp(jnp.sum(lq1_ref[...] * lk1_ref[...], axis=-1, keepdims=True))
           - jnp.exp(jnp.sum(lq2_ref[...] * lk2_ref[...], axis=-1, keepdims=True)) + lambda_init)
    for h, (acc, l) in enumerate(_flash_cols(qt2_refs, k_ref, vt_ref, scratch, pl.program_id(2), tq)):
        o = acc / l
        o = (o[:, :tq] - lam * o[:, tq:]).T
        o_ref[:, h * DIFF_V_DIM:(h + 1) * DIFF_V_DIM] = (
            _rms(o, g_ref[...], SUBLN_EPS) * (1.0 - lambda_init)).astype(o_ref.dtype)


def _mix_out_kernel(a_ref, b_ref, sa_ref, sb_ref, x_ref, wa_ref, wb_ref, wo_ref, g_ref, beta_ref, o_ref):
    ya = _dot(a_ref[...], wa_ref[...])
    yb = _dot(b_ref[...], wb_ref[...])
    m = (sa_ref[...].astype(F32) * ya + sb_ref[...].astype(F32) * yb).astype(BF16)
    mixed = _dot(m, wo_ref[...])
    o_ref[...] = _layer_norm(ALPHA * x_ref[...] + mixed, g_ref[...], beta_ref[...])


def _ffn_kernel(h_ref, wg_ref, wu_ref, wd_ref, g_ref, beta_ref, o_ref, hb_ref, acc_ref):
    f = pl.program_id(1)

    @pl.when(f == 0)
    def _():
        h = h_ref[...]
        hb_ref[...] = h.astype(BF16)
        acc_ref[...] = ALPHA * h

    hb = hb_ref[...]
    gate = _dot(hb, wg_ref[...])
    up = _dot(hb, wu_ref[...])
    act = (gate * _sigmoid(gate) * up).astype(BF16)
    acc_ref[...] += _dot(act, wd_ref[...])

    @pl.when(f == pl.num_programs(1) - 1)
    def _():
        o_ref[...] = _layer_norm(acc_ref[...], g_ref[...], beta_ref[...])


def _rope_angles(seq, dim):
    inv_freq = 1.0 / (ROPE_THETA ** (jnp.arange(0, dim, 2, dtype=F32) / dim))
    ang = jnp.arange(seq, dtype=F32)[:, None] * inv_freq[None, :]
    return jnp.cos(ang), jnp.sin(ang)


def kernel(x, w_in, mla_q_norm, mla_w_uq, mla_kv_norm, mla_w_ukv, diff_lambda_q1, diff_lambda_k1,
           diff_lambda_q2, diff_lambda_k2, diff_subln, w_branch_a, w_branch_b, w_out, ln1_g, ln1_b,
           w_ffn_in, w_ffn_down, ln2_g, ln2_b):
    B, S, D = x.shape
    T = B * S
    H = MLA_HEADS
    d_ff = w_ffn_down.shape[1]
    lambda_init = 0.8 - 0.6 * math.exp(-0.3 * 0)
    x2 = x.reshape(T, D)

    w = w_in[0]
    n_diff = DIFF_HEADS * 2 * DIFF_QK_DIM
    o_kr = MLA_Q_RANK + MLA_KV_RANK
    o_dq = o_kr + MLA_ROPE_DIM
    wt = jnp.swapaxes(w, 0, 1)
    w_rows = lambda start, n: pl.BlockSpec((pl.Element(n), pl.Element(D)), lambda *_: (start, 0),
                                           pipeline_mode=pl.Buffered(1))
    uq = mla_w_uq[0].reshape(MLA_Q_RANK, H, MLA_NOPE_DIM + MLA_ROPE_DIM)
    w_qnt = uq[:, :, :MLA_NOPE_DIM].reshape(MLA_Q_RANK, H * MLA_NOPE_DIM).T.astype(BF16)
    w_qpt = uq[:, :, MLA_NOPE_DIM:].reshape(MLA_Q_RANK, H * MLA_ROPE_DIM).T.astype(BF16)
    ukv = mla_w_ukv[0].reshape(MLA_KV_RANK, H, MLA_NOPE_DIM + MLA_V_DIM)
    w_kn = ukv[:, :, :MLA_NOPE_DIM].reshape(MLA_KV_RANK, H * LANES).astype(BF16)
    w_vt = ukv[:, :, MLA_NOPE_DIM:].reshape(MLA_KV_RANK, H * MLA_V_DIM).T.astype(BF16)
    row = lambda v: v.reshape(1, -1).astype(F32)

    assert MLA_ROPE_DIM == DIFF_QK_DIM
    cos, sin = _rope_angles(S, MLA_ROPE_DIM)
    cos_k, sin_k = jnp.tile(cos, (1, 4)), jnp.tile(jnp.concatenate([-sin, sin], axis=1), (1, 2))
    cos_t, sin_t = cos.T, sin.T

    bm = PROJ_ROWS
    n_pos = S // bm
    tok = lambda n: pl.BlockSpec((bm, n), lambda i: (i, 0))
    pos = pl.BlockSpec((bm, LANES), lambda i: (i % n_pos, 0))
    vt_spec = lambda n: pl.BlockSpec((None, n, bm), lambda i: (i, 0, 0))
    vt_shape = lambda n: jax.ShapeDtypeStruct((T // bm, n, bm), BF16)
    pos_t = pl.BlockSpec((cos_t.shape[0], bm), lambda i: (0, i % n_pos))
    n_steps = T // bm
    rest_rows = (wt.shape[0] - o_dq) // n_steps
    row_unit = math.gcd(o_dq, rest_rows)
    qt_a, k_a, vt_a, w_rest = pl.pallas_call(
        _with_casts(functools.partial(_mla_proj_kernel, scale=LOG2_E * (MLA_NOPE_DIM + MLA_ROPE_DIM) ** -0.5),
                    12, 3, 1),
        grid=(n_steps,),
        in_specs=[tok(D), w_rows(0, o_kr + LANES), _const_spec((1, MLA_Q_RANK)), _const_spec((1, MLA_KV_RANK)),
                  _const_spec(w_qnt.shape), _const_spec(w_qpt.shape), _const_spec(w_kn.shape),
                  _const_spec(w_vt.shape), pos, pos, pos_t, pos_t,
                  pl.BlockSpec((pl.Element(rest_rows), pl.Element(D)),
                               lambda i: ((o_dq // row_unit + rest_rows // row_unit * i) * row_unit, 0))],
        out_specs=[vt_spec(H * MLA_QK_PAD), tok(H * MLA_QK_PAD), vt_spec(H * MLA_V_DIM),
                   pl.BlockSpec((rest_rows, D), lambda i: (i, 0))],
        out_shape=[vt_shape(H * MLA_QK_PAD), jax.ShapeDtypeStruct((T, H * MLA_QK_PAD), BF16),
                   vt_shape(H * MLA_V_DIM), jax.ShapeDtypeStruct((wt.shape[0] - o_dq, D), BF16)],
        scratch_shapes=[pltpu.VMEM((o_kr + LANES, D), BF16)],
        compiler_params=_params("arbitrary"), name="mla_proj",
    )(x2, wt, row(mla_q_norm), row(mla_kv_norm), w_qnt, w_qpt, w_kn, w_vt, cos_k, sin_k, cos_t, sin_t, wt)

    c_in, c_out, c_shape = _slab_specs([w_ffn_down[0]], T // bm)
    qt_b, k_b, vt_b, w_fd = pl.pallas_call(
        _with_casts(functools.partial(_diff_proj_kernel, scale=LOG2_E * DIFF_QK_DIM ** -0.5), 8, 3, 1),
        grid=(T // bm,),
        in_specs=[tok(D), w_rows(0, n_diff), w_rows(n_diff, n_diff), w_rows(2 * n_diff, n_diff),
                  pos, pos, pos_t, pos_t] + c_in,
        out_specs=[vt_spec(n_diff), tok(n_diff), vt_spec(n_diff)] + c_out,
        out_shape=[vt_shape(n_diff), jax.ShapeDtypeStruct((T, n_diff), BF16), vt_shape(n_diff)] + c_shape,
        compiler_params=_params("parallel"), name="diff_proj",
    )(x2, w_rest, w_rest, w_rest, cos_k, sin_k, cos_t, sin_t, w_ffn_down[0])

    bmg = GATE_ROWS
    mix_weights = [w_branch_a[0], w_branch_b[0], w_out[0]]
    c_in, c_out, c_shape = _slab_specs(mix_weights, T // bmg)
    sig, w_a, w_b, w_o = pl.pallas_call(
        _with_casts(_gate_proj_kernel, 2, 1, len(mix_weights)),
        grid=(T // bmg,),
        in_specs=[pl.BlockSpec((bmg, D), lambda i: (i, 0)), w_rows(3 * n_diff, 2 * D)] + c_in,
        out_specs=[pl.BlockSpec((bmg, 2 * D), lambda i: (i, 0))] + c_out,
        out_shape=[jax.ShapeDtypeStruct((T, 2 * D), BF16)] + c_shape,
        compiler_params=_params("parallel"), name="gate_proj",
    )(x2, w_rest, *mix_weights)

    tq = bm
    hp = ATTN_HEADS_PER_STEP
    head_blk = lambda rows, width, full: pl.BlockSpec(
        (None, rows, hp * width), (lambda b, h, i: (b, 0, h)) if full else (lambda b, h, i: (b, i, h)))
    vt_blk = lambda dv: pl.BlockSpec((None, S // tq, hp, dv, tq), lambda b, h, i: (b, 0, h, 0, 0))
    qt_blk = lambda dk: pl.BlockSpec((None, None, hp, dk, tq), lambda b, h, i: (b, i, h, 0, 0))
    attn_a = pl.pallas_call(
        functools.partial(_mla_attn_kernel, tq=tq, n_heads=hp),
        grid=(B, H // hp, S // tq),
        in_specs=[qt_blk(MLA_QK_PAD), head_blk(S, MLA_QK_PAD, True), vt_blk(MLA_V_DIM)],
        out_specs=head_blk(tq, MLA_V_DIM, False),
        out_shape=jax.ShapeDtypeStruct((B, S, H * MLA_V_DIM), BF16),
        scratch_shapes=hp * [pltpu.VMEM((MLA_QK_PAD, tq), BF16)] + _flash_scratch(hp, MLA_V_DIM, tq, tq),
        compiler_params=_params("parallel", "parallel", "parallel"), name="mla_attn",
    )(qt_a.reshape(B, S // tq, H, MLA_QK_PAD, tq), k_a.reshape(B, S, -1),
      vt_a.reshape(B, S // tq, H, MLA_V_DIM, tq))

    lam_spec = pl.BlockSpec((1, DIFF_QK_DIM), lambda b, h, i: (0, 0))
    attn_b = pl.pallas_call(
        functools.partial(_diff_attn_kernel, tq=tq, n_heads=hp, lambda_init=lambda_init),
        grid=(B, DIFF_HEADS // hp, S // tq),
        in_specs=[qt_blk(LANES), head_blk(S, LANES, True), vt_blk(DIFF_V_DIM),
                  lam_spec, lam_spec, lam_spec, lam_spec, pl.BlockSpec((1, DIFF_V_DIM), lambda b, h, i: (0, 0))],
        out_specs=head_blk(tq, DIFF_V_DIM, False),
        out_shape=jax.ShapeDtypeStruct((B, S, DIFF_HEADS * DIFF_V_DIM), BF16),
        scratch_shapes=hp * [pltpu.VMEM((LANES, 2 * tq), BF16)] + _flash_scratch(hp, DIFF_V_DIM, 2 * tq, tq),
        compiler_params=_params("parallel", "parallel", "parallel"), name="diff_attn",
    )(qt_b.reshape(B, S // tq, DIFF_HEADS, LANES, tq), k_b.reshape(B, S, -1),
      vt_b.reshape(B, S // tq, DIFF_HEADS, DIFF_V_DIM, tq),
      row(diff_lambda_q1), row(diff_lambda_k1), row(diff_lambda_q2), row(diff_lambda_k2), row(diff_subln))

    bmo = MIX_ROWS
    tokm = lambda n, j=0: pl.BlockSpec((bmo, n), lambda i: (i, j))
    c_in, c_out, c_shape = _slab_specs([w_ffn_in[0]], T // bmo)
    h1, w_fi = pl.pallas_call(
        _with_casts(_mix_out_kernel, 10, 1, 1),
        grid=(T // bmo,),
        in_specs=[tokm(w_a.shape[0]), tokm(w_b.shape[0]), tokm(D, 0), tokm(D, 1), tokm(D),
                  _const_spec(w_a.shape), _const_spec(w_b.shape), _const_spec(w_o.shape),
                  _const_spec((1, D)), _const_spec((1, D))] + c_in,
        out_specs=[tokm(D)] + c_out,
        out_shape=[jax.ShapeDtypeStruct((T, D), F32)] + c_shape,
        compiler_params=_params("parallel"), name="mix_out",
    )(attn_a.reshape(T, -1), attn_b.reshape(T, -1), sig, sig, x2, w_a, w_b, w_o, row(ln1_g), row(ln1_b), w_ffn_in[0])

    bmf, tf = FFN_ROWS, FFN_COLS
    n_f = d_ff // tf
    out = pl.pallas_call(
        _ffn_kernel,
        grid=(T // bmf, n_f),
        in_specs=[pl.BlockSpec((bmf, D), lambda i, f: (i, 0)),
                  pl.BlockSpec((D, tf), lambda i, f: (0, f)),
                  pl.BlockSpec((D, tf), lambda i, f: (0, n_f + f)),
                  pl.BlockSpec((tf, D), lambda i, f: (f, 0)),
                  pl.BlockSpec((1, D), lambda i, f: (0, 0)), pl.BlockSpec((1, D), lambda i, f: (0, 0))],
        out_specs=pl.BlockSpec((bmf, D), lambda i, f: (i, 0)),
        out_shape=jax.ShapeDtypeStruct((T, D), F32),
        scratch_shapes=[pltpu.VMEM((bmf, D), BF16), pltpu.VMEM((bmf, D), F32)],
        compiler_params=_params("parallel", "arbitrary"), name="ffn",
    )(h1, w_fi, w_fi, w_fd, row(ln2_g), row(ln2_b))

    return out.reshape(B, S, D)
```

```python
import functools
import math

import jax
import jax.numpy as jnp
from jax import lax
from jax.experimental import pallas as pl
from jax.experimental.pallas import tpu as pltpu

MLA_HEADS = 8
MLA_NOPE_DIM = 128
MLA_ROPE_DIM = 64
MLA_V_DIM = 128
MLA_Q_RANK = 512
MLA_KV_RANK = 512
DIFF_HEADS = 8
DIFF_QK_DIM = 64
DIFF_V_DIM = 128
ROPE_THETA = 10000.0
DEPTH = 1
ALPHA = (2 * DEPTH) ** 0.25
RMS_EPS = 1e-6
SUBLN_EPS = 1e-5
LN_EPS = 1e-5

LOG2_E = math.log2(math.e)
LANES = 128
MLA_QK_PAD = 256
SUM_ROWS = 16
ROPE_HALF = MLA_ROPE_DIM // 2
VMEM_LIMIT = 56 * 1024 * 1024

PROJ_ROWS = 512
GATE_ROWS = 512
MIX_ROWS = 256
FFN_ROWS = 512
FFN_COLS = 512
ATTN_HEADS_PER_STEP = 4

BF16 = jnp.bfloat16
F32 = jnp.float32


def _dot(a, b):
    return jnp.dot(a, b, preferred_element_type=F32)


def _dot_nt(a, b):
    return lax.dot_general(a, b, (((1,), (1,)), ((), ())), preferred_element_type=F32)


def _rms(x, g, eps):
    return x * lax.rsqrt(jnp.mean(x * x, axis=-1, keepdims=True) + eps) * g


def _layer_norm(x, g, b):
    mu = jnp.mean(x, axis=-1, keepdims=True)
    xc = x - mu
    var = jnp.mean(xc * xc, axis=-1, keepdims=True)
    return xc * lax.rsqrt(var + LN_EPS) * g + b


def _rope(x, cos, sin_signed):
    up = pltpu.roll(x, LANES - ROPE_HALF, 1)
    dn = pltpu.roll(x, ROPE_HALF, 1)
    first_half = lax.broadcasted_iota(jnp.int32, x.shape, 1) % (2 * ROPE_HALF) < ROPE_HALF
    return x * cos + jnp.where(first_half, up, dn) * sin_signed


def _sigmoid(x):
    return 1.0 / (1.0 + jnp.exp(-x))


def _const_spec(shape):
    return pl.BlockSpec(shape, lambda *_: (0,) * len(shape), pipeline_mode=pl.Buffered(1))


def _params(*sem):
    return pltpu.CompilerParams(dimension_semantics=sem, vmem_limit_bytes=VMEM_LIMIT)


def _with_casts(body, n_in, n_out, n_cast):
    def kernel(*refs):
        ins, refs = refs[:n_in], refs[n_in:]
        slabs_in, refs = refs[:n_cast], refs[n_cast:]
        outs, refs = refs[:n_out], refs[n_out:]
        slabs_out, scratch = refs[:n_cast], refs[n_cast:]
        for src, dst in zip(slabs_in, slabs_out):
            dst[...] = src[...].astype(BF16)
        body(*ins, *outs, *scratch)
    return kernel


def _slab_specs(arrays, steps):
    ins, outs, shapes = [], [], []
    for a in arrays:
        rows, width = a.shape
        spec = pl.BlockSpec((rows // steps, width), lambda i: (i, 0))
        ins.append(spec)
        outs.append(spec)
        shapes.append(jax.ShapeDtypeStruct(a.shape, BF16))
    return ins, outs, shapes


def _rope_rows(x, cos_t, sin_t):
    half = cos_t.shape[0]
    out = []
    for j in range(x.shape[0] // (2 * half)):
        x1 = x[2 * half * j:2 * half * j + half]
        x2 = x[2 * half * j + half:2 * half * (j + 1)]
        out += [x1 * cos_t - x2 * sin_t, x2 * cos_t + x1 * sin_t]
    return jnp.concatenate(out, axis=0)


def _mla_proj_kernel(x_ref, w_lat_ref, gq_ref, gkv_ref, w_qnt_ref, w_qpt_ref, w_kn_ref, w_vt_ref,
                     cos_ref, sin_ref, cost_ref, sint_ref, qt_ref, k_ref, vt_ref, w_lat_bf_ref, *, scale):
    @pl.when(pl.program_id(0) == 0)
    def _():
        w_lat_bf_ref[...] = w_lat_ref[...].astype(BF16)

    xb = x_ref[...].astype(BF16)
    lat = _dot_nt(xb, w_lat_bf_ref[...])
    cq = _rms(lat[:, :MLA_Q_RANK], gq_ref[...], RMS_EPS).astype(BF16)
    ckv = _rms(lat[:, MLA_Q_RANK:MLA_Q_RANK + MLA_KV_RANK], gkv_ref[...], RMS_EPS).astype(BF16)
    kr = lat[:, MLA_Q_RANK + MLA_KV_RANK:]
    kr = jnp.where(lax.broadcasted_iota(jnp.int32, kr.shape, 1) < MLA_ROPE_DIM, kr, 0.0)
    k_pe = _rope(kr, cos_ref[...], sin_ref[...]).astype(BF16)
    kn = _dot(ckv, w_kn_ref[...])
    vt_ref[...] = _dot_nt(w_vt_ref[...], ckv).astype(BF16)
    qnt = _dot_nt(w_qnt_ref[...], cq)
    qpt = _dot_nt(w_qpt_ref[...], cq)
    cos_t, sin_t = cost_ref[...], sint_ref[...]
    pad = jnp.zeros((MLA_QK_PAD - MLA_NOPE_DIM - MLA_ROPE_DIM, qnt.shape[1]), BF16)
    for h in range(MLA_HEADS):
        base = h * MLA_QK_PAD
        qt_ref[base:base + MLA_NOPE_DIM, :] = (qnt[h * MLA_NOPE_DIM:(h + 1) * MLA_NOPE_DIM] * scale).astype(BF16)
        q_pe = _rope_rows(qpt[h * MLA_ROPE_DIM:(h + 1) * MLA_ROPE_DIM], cos_t, sin_t)
        qt_ref[base + MLA_NOPE_DIM:base + MLA_NOPE_DIM + MLA_ROPE_DIM, :] = (q_pe * scale).astype(BF16)
        qt_ref[base + MLA_NOPE_DIM + MLA_ROPE_DIM:base + MLA_QK_PAD, :] = pad
        k_ref[:, base:base + LANES] = kn[:, h * LANES:(h + 1) * LANES].astype(BF16)
        k_ref[:, base + LANES:base + 2 * LANES] = k_pe


def _diff_proj_kernel(x_ref, w_q_ref, w_k_ref, w_vt_ref, cos_ref, sin_ref, cost_ref, sint_ref,
                      qt_ref, k_ref, vt_ref, *, scale):
    xb = x_ref[...].astype(BF16)
    yk = _dot_nt(xb, w_k_ref[...])
    vt_ref[...] = _dot_nt(w_vt_ref[...], xb).astype(BF16)
    qt = _dot_nt(w_q_ref[...], xb)
    qt_ref[...] = (_rope_rows(qt, cost_ref[...], sint_ref[...]) * scale).astype(BF16)
    cos, sin = cos_ref[...], sin_ref[...]
    for h in range(DIFF_HEADS):
        lo, hi = h * LANES, (h + 1) * LANES
        k_ref[:, lo:hi] = _rope(yk[:, lo:hi], cos, sin).astype(BF16)


def _gate_proj_kernel(x_ref, w_ref, o_ref):
    xb = x_ref[...].astype(BF16)
    o_ref[...] = _sigmoid(_dot_nt(xb, w_ref[...])).astype(BF16)


def _flash_cols(qts, k_ref, vt_ref, scratch, q_tile, tq):
    n_heads = len(qts)
    per_head = len(scratch) // n_heads
    tk = tq // 2
    n_groups, dk, _ = qts[0].shape
    cols = n_groups * LANES
    dv = vt_ref.shape[2]
    ones = jnp.ones((SUM_ROWS, tk), BF16)

    class Head:
        def __init__(self, h):
            (s0, s1, p0, p1, self.m, a0, a1, self.acc) = scratch[h * per_head:(h + 1) * per_head]
            self.qt = qts[h]
            self.s, self.p, self.a = (s0, s1), (p0, p1), (a0, a1)
            self.lanes = slice(h * dk, (h + 1) * dk)
            self.h = h

    heads = [Head(h) for h in range(n_heads)]
    for hd in heads:
        hd.m[...] = jnp.full(hd.m.shape, -jnp.inf, F32)
        hd.acc[...] = jnp.zeros(hd.acc.shape, F32)
        hd.p[1][...] = jnp.zeros(hd.p[1].shape, BF16)
        hd.a[1][...] = jnp.ones(hd.a[1].shape, F32)

    all_groups = [(0, cols // LANES)]
    late_groups = [((m * tq + tk) // LANES, (m + 1) * tq // LANES) for m in range(cols // tq)]

    def scores(hd, c, half, runs=all_groups):
        k = k_ref[pl.ds(pl.multiple_of(c * tk, tk), tk), hd.lanes]
        for g0, g1 in runs:
            s = _dot(k, jnp.concatenate([hd.qt[g] for g in range(g0, g1)], axis=1))
            for g in range(g0, g1):
                hd.s[half][g] = s[:, (g - g0) * LANES:(g - g0 + 1) * LANES]

    def softmax(hd, half, diagonal, runs=all_groups):
        for g in [g for g0, g1 in runs for g in range(g0, g1)]:
            sl = slice(g * LANES, (g + 1) * LANES)
            s = hd.s[half][g]
            if diagonal:
                k_pos = half * tk + lax.broadcasted_iota(jnp.int32, (tk, LANES), 0)
                q_pos = (g * LANES) % tq + lax.broadcasted_iota(jnp.int32, (tk, LANES), 1)
                s = jnp.where(k_pos <= q_pos, s, -jnp.inf)
            m_old = hd.m[:, sl]
            m_new = jnp.maximum(m_old, jnp.max(s, axis=0, keepdims=True))
            a = jnp.exp2(m_old - m_new)
            p = jnp.exp2(s - m_new)
            hd.m[:, sl] = m_new
            hd.a[half][:, sl] = a
            hd.p[half][g] = p.astype(BF16)

    def weighted_values(hd, pair, half, runs=all_groups):
        vt = jnp.concatenate([vt_ref[pair, hd.h][:, half * tk:(half + 1) * tk], ones], axis=0)
        for g0, g1 in runs:
            sl = slice(g0 * LANES, g1 * LANES)
            p = jnp.concatenate([hd.p[half][g] for g in range(g0, g1)], axis=1)
            hd.acc[:, sl] = hd.a[half][:, sl] * hd.acc[:, sl] + _dot(vt, p)

    def pair_step(i, diagonal):
        second = late_groups if diagonal else all_groups
        for hd in heads:
            weighted_values(hd, jnp.maximum(i - 1, 0), 1)
        for hd in heads:
            scores(hd, 2 * i + 1, 1, second)
        for hd in heads:
            softmax(hd, 0, diagonal)
        if not diagonal:
            for hd in heads:
                scores(hd, 2 * i + 2, 0)
        for hd in heads:
            weighted_values(hd, i, 0)
        for hd in heads:
            softmax(hd, 1, diagonal, second)

    for hd in heads:
        scores(hd, 0, 0)
    lax.fori_loop(0, q_tile, lambda i, c: (pair_step(i, False), c)[1], 0)
    pair_step(q_tile, True)
    for hd in heads:
        weighted_values(hd, q_tile, 1, late_groups)
    return [(hd.acc[:dv, :], hd.acc[dv:dv + 1, :]) for hd in heads]


def _flash_scratch(n_heads, dv, cols, tq):
    tk = tq // 2
    stat = pltpu.VMEM((1, cols), F32)
    group_major = lambda dtype: pltpu.VMEM((cols // LANES, tk, LANES), dtype)
    return n_heads * [group_major(F32), group_major(F32), group_major(BF16), group_major(BF16),
                      stat, stat, stat, pltpu.VMEM((dv + SUM_ROWS, cols), F32)]


def _mla_attn_kernel(qt_ref, k_ref, vt_ref, o_ref, *scratch, tq, n_heads):
    qts, scratch = scratch[:n_heads], scratch[n_heads:]
    for h in range(n_heads):
        for g in range(tq // LANES):
            qts[h][g] = qt_ref[h, :, g * LANES:(g + 1) * LANES]
    for h, (acc, l) in enumerate(_flash_cols(qts, k_ref, vt_ref, scratch, pl.program_id(2), tq)):
        o_ref[:, h * MLA_V_DIM:(h + 1) * MLA_V_DIM] = (acc / l).T.astype(o_ref.dtype)


def _diff_attn_kernel(qt_ref, k_ref, vt_ref, lq1_ref, lk1_ref, lq2_ref, lk2_ref, g_ref, o_ref, *scratch,
                      tq, n_heads, lambda_init):
    qt2_refs, scratch = scratch[:n_heads], scratch[n_heads:]
    row = lax.broadcasted_iota(jnp.int32, (LANES, LANES), 0)
    zero = jnp.zeros((LANES, LANES), BF16)
    n_g = tq // LANES
    for h in range(n_heads):
        for g in range(n_g):
            qt = qt_ref[h, :, g * LANES:(g + 1) * LANES]
            qt2_refs[h][g] = jnp.where(row < DIFF_QK_DIM, qt, zero)
            qt2_refs[h][n_g + g] = jnp.where(row >= DIFF_QK_DIM, qt, zero)
    lam = (jnp.exp(jnp.sum(lq1_ref[...] * lk1_ref[...], axis=-1, keepdims=True))
           - jnp.exp(jnp.sum(lq2_ref[...] * lk2_ref[...], axis=-1, keepdims=True)) + lambda_init)
    for h, (acc, l) in enumerate(_flash_cols(qt2_refs, k_ref, vt_ref, scratch, pl.program_id(2), tq)):
        o = acc / l
        o = (o[:, :tq] - lam * o[:, tq:]).T
        o_ref[:, h * DIFF_V_DIM:(h + 1) * DIFF_V_DIM] = (
            _rms(o, g_ref[...], SUBLN_EPS) * (1.0 - lambda_init)).astype(o_ref.dtype)


def _mix_out_kernel(a_ref, b_ref, sa_ref, sb_ref, x_ref, wa_ref, wb_ref, wo_ref, g_ref, beta_ref, o_ref):
    ya = _dot(a_ref[...], wa_ref[...])
    yb = _dot(b_ref[...], wb_ref[...])
    m = (sa_ref[...].astype(F32) * ya + sb_ref[...].astype(F32) * yb).astype(BF16)
    mixed = _dot(m, wo_ref[...])
    o_ref[...] = _layer_norm(ALPHA * x_ref[...] + mixed, g_ref[...], beta_ref[...])


def _ffn_kernel(h_ref, wg_ref, wu_ref, wd_ref, g_ref, beta_ref, o_ref, hb_ref, acc_ref):
    f = pl.program_id(1)

    @pl.when(f == 0)
    def _():
        h = h_ref[...]
        hb_ref[...] = h.astype(BF16)
        acc_ref[...] = ALPHA * h

    hb = hb_ref[...]
    gate = _dot(hb, wg_ref[...])
    up = _dot(hb, wu_ref[...])
    act = (gate * _sigmoid(gate) * up).astype(BF16)
    acc_ref[...] += _dot(act, wd_ref[...])

    @pl.when(f == pl.num_programs(1) - 1)
    def _():
        o_ref[...] = _layer_norm(acc_ref[...], g_ref[...], beta_ref[...])


def _rope_angles(seq, dim):
    inv_freq = 1.0 / (ROPE_THETA ** (jnp.arange(0, dim, 2, dtype=F32) / dim))
    ang = jnp.arange(seq, dtype=F32)[:, None] * inv_freq[None, :]
    return jnp.cos(ang), jnp.sin(ang)


def kernel(x, w_in, mla_q_norm, mla_w_uq, mla_kv_norm, mla_w_ukv, diff_lambda_q1, diff_lambda_k1,
           diff_lambda_q2, diff_lambda_k2, diff_subln, w_branch_a, w_branch_b, w_out, ln1_g, ln1_b,
           w_ffn_in, w_ffn_down, ln2_g, ln2_b):
    B, S, D = x.shape
    T = B * S
    H = MLA_HEADS
    d_ff = w_ffn_down.shape[1]
    lambda_init = 0.8 - 0.6 * math.exp(-0.3 * 0)
    x2 = x.reshape(T, D)

    w = w_in[0]
    n_diff = DIFF_HEADS * 2 * DIFF_QK_DIM
    o_kr = MLA_Q_RANK + MLA_KV_RANK
    o_dq = o_kr + MLA_ROPE_DIM
    wt = jnp.swapaxes(w, 0, 1)
    w_rows = lambda start, n: pl.BlockSpec((pl.Element(n), pl.Element(D)), lambda *_: (start, 0),
                                           pipeline_mode=pl.Buffered(1))
    uq = mla_w_uq[0].reshape(MLA_Q_RANK, H, MLA_NOPE_DIM + MLA_ROPE_DIM)
    w_qnt = uq[:, :, :MLA_NOPE_DIM].reshape(MLA_Q_RANK, H * MLA_NOPE_DIM).T.astype(BF16)
    w_qpt = uq[:, :, MLA_NOPE_DIM:].reshape(MLA_Q_RANK, H * MLA_ROPE_DIM).T.astype(BF16)
    ukv = mla_w_ukv[0].reshape(MLA_KV_RANK, H, MLA_NOPE_DIM + MLA_V_DIM)
    w_kn = ukv[:, :, :MLA_NOPE_DIM].reshape(MLA_KV_RANK, H * LANES).astype(BF16)
    w_vt = ukv[:, :, MLA_NOPE_DIM:].reshape(MLA_KV_RANK, H * MLA_V_DIM).T.astype(BF16)
    row = lambda v: v.reshape(1, -1).astype(F32)

    assert MLA_ROPE_DIM == DIFF_QK_DIM
    cos, sin = _rope_angles(S, MLA_ROPE_DIM)
    cos_k, sin_k = jnp.tile(cos, (1, 4)), jnp.tile(jnp.concatenate([-sin, sin], axis=1), (1, 2))
    cos_t, sin_t = cos.T, sin.T

    bm = PROJ_ROWS
    n_pos = S // bm
    tok = lambda n: pl.BlockSpec((bm, n), lambda i: (i, 0))
    pos = pl.BlockSpec((bm, LANES), lambda i: (i % n_pos, 0))
    vt_spec = lambda n: pl.BlockSpec((None, n, bm), lambda i: (i, 0, 0))
    vt_shape = lambda n: jax.ShapeDtypeStruct((T // bm, n, bm), BF16)
    pos_t = pl.BlockSpec((cos_t.shape[0], bm), lambda i: (0, i % n_pos))
    n_steps = T // bm
    rest_rows = (wt.shape[0] - o_dq) // n_steps
    row_unit = math.gcd(o_dq, rest_rows)
    qt_a, k_a, vt_a, w_rest = pl.pallas_call(
        _with_casts(functools.partial(_mla_proj_kernel, scale=LOG2_E * (MLA_NOPE_DIM + MLA_ROPE_DIM) ** -0.5),
                    12, 3, 1),
        grid=(n_steps,),
        in_specs=[tok(D), w_rows(0, o_kr + LANES), _const_spec((1, MLA_Q_RANK)), _const_spec((1, MLA_KV_RANK)),
                  _const_spec(w_qnt.shape), _const_spec(w_qpt.shape), _const_spec(w_kn.shape),
                  _const_spec(w_vt.shape), pos, pos, pos_t, pos_t,
                  pl.BlockSpec((pl.Element(rest_rows), pl.Element(D)),
                               lambda i: ((o_dq // row_unit + rest_rows // row_unit * i) * row_unit, 0))],
        out_specs=[vt_spec(H * MLA_QK_PAD), tok(H * MLA_QK_PAD), vt_spec(H * MLA_V_DIM),
                   pl.BlockSpec((rest_rows, D), lambda i: (i, 0))],
        out_shape=[vt_shape(H * MLA_QK_PAD), jax.ShapeDtypeStruct((T, H * MLA_QK_PAD), BF16),
                   vt_shape(H * MLA_V_DIM), jax.ShapeDtypeStruct((wt.shape[0] - o_dq, D), BF16)],
        scratch_shapes=[pltpu.VMEM((o_kr + LANES, D), BF16)],
        compiler_params=_params("arbitrary"), name="mla_proj",
    )(x2, wt, row(mla_q_norm), row(mla_kv_norm), w_qnt, w_qpt, w_kn, w_vt, cos_k, sin_k, cos_t, sin_t, wt)

    c_in, c_out, c_shape = _slab_specs([w_ffn_down[0]], T // bm)
    qt_b, k_b, vt_b, w_fd = pl.pallas_call(
        _with_casts(functools.partial(_diff_proj_kernel, scale=LOG2_E * DIFF_QK_DIM ** -0.5), 8, 3, 1),
        grid=(T // bm,),
        in_specs=[tok(D), w_rows(0, n_diff), w_rows(n_diff, n_diff), w_rows(2 * n_diff, n_diff),
                  pos, pos, pos_t, pos_t] + c_in,
        out_specs=[vt_spec(n_diff), tok(n_diff), vt_spec(n_diff)] + c_out,
        out_shape=[vt_shape(n_diff), jax.ShapeDtypeStruct((T, n_diff), BF16), vt_shape(n_diff)] + c_shape,
        compiler_params=_params("parallel"), name="diff_proj",
    )(x2, w_rest, w_rest, w_rest, cos_k, sin_k, cos_t, sin_t, w_ffn_down[0])

    bmg = GATE_ROWS
    mix_weights = [w_branch_a[0], w_branch_b[0], w_out[0]]
    c_in, c_out, c_shape = _slab_specs(mix_weights, T // bmg)
    sig, w_a, w_b, w_o = pl.pallas_call(
        _with_casts(_gate_proj_kernel, 2, 1, len(mix_weights)),
        grid=(T // bmg,),
        in_specs=[pl.BlockSpec((bmg, D), lambda i: (i, 0)), w_rows(3 * n_diff, 2 * D)] + c_in,
        out_specs=[pl.BlockSpec((bmg, 2 * D), lambda i: (i, 0))] + c_out,
        out_shape=[jax.ShapeDtypeStruct((T, 2 * D), BF16)] + c_shape,
        compiler_params=_params("parallel"), name="gate_proj",
    )(x2, w_rest, *mix_weights)

    tq = bm
    hp = ATTN_HEADS_PER_STEP
    head_blk = lambda rows, width, full: pl.BlockSpec(
        (None, rows, hp * width), (lambda b, h, i: (b, 0, h)) if full else (lambda b, h, i: (b, i, h)))
    vt_blk = lambda dv: pl.BlockSpec((None, S // tq, hp, dv, tq), lambda b, h, i: (b, 0, h, 0, 0))
    qt_blk = lambda dk: pl.BlockSpec((None, None, hp, dk, tq), lambda b, h, i: (b, i, h, 0, 0))
    attn_a = pl.pallas_call(
        functools.partial(_mla_attn_kernel, tq=tq, n_heads=hp),
        grid=(B, H // hp, S // tq),
        in_specs=[qt_blk(MLA_QK_PAD), head_blk(S, MLA_QK_PAD, True), vt_blk(MLA_V_DIM)],
        out_specs=head_blk(tq, MLA_V_DIM, False),
        out_shape=jax.ShapeDtypeStruct((B, S, H * MLA_V_DIM), BF16),
        scratch_shapes=(hp * [pltpu.VMEM((tq // LANES, MLA_QK_PAD, LANES), BF16)]
                        + _flash_scratch(hp, MLA_V_DIM, tq, tq)),
        compiler_params=_params("parallel", "parallel", "parallel"), name="mla_attn",
    )(qt_a.reshape(B, S // tq, H, MLA_QK_PAD, tq), k_a.reshape(B, S, -1),
      vt_a.reshape(B, S // tq, H, MLA_V_DIM, tq))

    lam_spec = pl.BlockSpec((1, DIFF_QK_DIM), lambda b, h, i: (0, 0))
    attn_b = pl.pallas_call(
        functools.partial(_diff_attn_kernel, tq=tq, n_heads=hp, lambda_init=lambda_init),
        grid=(B, DIFF_HEADS // hp, S // tq),
        in_specs=[qt_blk(LANES), head_blk(S, LANES, True), vt_blk(DIFF_V_DIM),
                  lam_spec, lam_spec, lam_spec, lam_spec, pl.BlockSpec((1, DIFF_V_DIM), lambda b, h, i: (0, 0))],
        out_specs=head_blk(tq, DIFF_V_DIM, False),
        out_shape=jax.ShapeDtypeStruct((B, S, DIFF_HEADS * DIFF_V_DIM), BF16),
        scratch_shapes=(hp * [pltpu.VMEM((2 * tq // LANES, LANES, LANES), BF16)]
                        + _flash_scratch(hp, DIFF_V_DIM, 2 * tq, tq)),
        compiler_params=_params("parallel", "parallel", "parallel"), name="diff_attn",
    )(qt_b.reshape(B, S // tq, DIFF_HEADS, LANES, tq), k_b.reshape(B, S, -1),
      vt_b.reshape(B, S // tq, DIFF_HEADS, DIFF_V_DIM, tq),
      row(diff_lambda_q1), row(diff_lambda_k1), row(diff_lambda_q2), row(diff_lambda_k2), row(diff_subln))

    bmo = MIX_ROWS
    tokm = lambda n, j=0: pl.BlockSpec((bmo, n), lambda i: (i, j))
    c_in, c_out, c_shape = _slab_specs([w_ffn_in[0]], T // bmo)
    h1, w_fi = pl.pallas_call(
        _with_casts(_mix_out_kernel, 10, 1, 1),
        grid=(T // bmo,),
        in_specs=[tokm(w_a.shape[0]), tokm(w_b.shape[0]), tokm(D, 0), tokm(D, 1), tokm(D),
                  _const_spec(w_a.shape), _const_spec(w_b.shape), _const_spec(w_o.shape),
                  _const_spec((1, D)), _const_spec((1, D))] + c_in,
        out_specs=[tokm(D)] + c_out,
        out_shape=[jax.ShapeDtypeStruct((T, D), F32)] + c_shape,
        compiler_params=_params("parallel"), name="mix_out",
    )(attn_a.reshape(T, -1), attn_b.reshape(T, -1), sig, sig, x2, w_a, w_b, w_o, row(ln1_g), row(ln1_b), w_ffn_in[0])

    bmf, tf = FFN_ROWS, FFN_COLS
    n_f = d_ff // tf
    out = pl.pallas_call(
        _ffn_kernel,
        grid=(T // bmf, n_f),
        in_specs=[pl.BlockSpec((bmf, D), lambda i, f: (i, 0)),
                  pl.BlockSpec((D, tf), lambda i, f: (0, f)),
                  pl.BlockSpec((D, tf), lambda i, f: (0, n_f + f)),
                  pl.BlockSpec((tf, D), lambda i, f: (f, 0)),
                  pl.BlockSpec((1, D), lambda i, f: (0, 0)), pl.BlockSpec((1, D), lambda i, f: (0, 0))],
        out_specs=pl.BlockSpec((bmf, D), lambda i, f: (i, 0)),
        out_shape=jax.ShapeDtypeStruct((T, D), F32),
        scratch_shapes=[pltpu.VMEM((bmf, D), BF16), pltpu.VMEM((bmf, D), F32)],
        compiler_params=_params("parallel", "arbitrary"), name="ffn",
    )(h1, w_fi, w_fi, w_fd, row(ln2_g), row(ln2_b))

    return out.reshape(B, S, D)
```

```python
import functools
import math

import jax
import jax.numpy as jnp
from jax import lax
from jax.experimental import pallas as pl
from jax.experimental.pallas import tpu as pltpu

MLA_HEADS = 8
MLA_NOPE_DIM = 128
MLA_ROPE_DIM = 64
MLA_V_DIM = 128
MLA_Q_RANK = 512
MLA_KV_RANK = 512
DIFF_HEADS = 8
DIFF_QK_DIM = 64
DIFF_V_DIM = 128
ROPE_THETA = 10000.0
DEPTH = 1
ALPHA = (2 * DEPTH) ** 0.25
RMS_EPS = 1e-6
SUBLN_EPS = 1e-5
LN_EPS = 1e-5

LOG2_E = math.log2(math.e)
LANES = 128
MLA_QK_PAD = 256
SUM_ROWS = 16
ROPE_HALF = MLA_ROPE_DIM // 2
VMEM_LIMIT = 56 * 1024 * 1024

PROJ_ROWS = 512
GATE_ROWS = 512
MIX_ROWS = 256
FFN_ROWS = 512
FFN_COLS = 512
ATTN_HEADS_PER_STEP = 4

BF16 = jnp.bfloat16
F32 = jnp.float32


def _dot(a, b):
    return jnp.dot(a, b, preferred_element_type=F32)


def _dot_nt(a, b):
    return lax.dot_general(a, b, (((1,), (1,)), ((), ())), preferred_element_type=F32)


def _rms(x, g, eps):
    return x * lax.rsqrt(jnp.mean(x * x, axis=-1, keepdims=True) + eps) * g


def _layer_norm(x, g, b):
    mu = jnp.mean(x, axis=-1, keepdims=True)
    xc = x - mu
    var = jnp.mean(xc * xc, axis=-1, keepdims=True)
    return xc * lax.rsqrt(var + LN_EPS) * g + b


def _rope(x, cos, sin_signed):
    up = pltpu.roll(x, LANES - ROPE_HALF, 1)
    dn = pltpu.roll(x, ROPE_HALF, 1)
    first_half = lax.broadcasted_iota(jnp.int32, x.shape, 1) % (2 * ROPE_HALF) < ROPE_HALF
    return x * cos + jnp.where(first_half, up, dn) * sin_signed


def _sigmoid(x):
    return 1.0 / (1.0 + jnp.exp(-x))


def _const_spec(shape):
    return pl.BlockSpec(shape, lambda *_: (0,) * len(shape), pipeline_mode=pl.Buffered(1))


def _params(*sem):
    return pltpu.CompilerParams(dimension_semantics=sem, vmem_limit_bytes=VMEM_LIMIT)


def _with_casts(body, n_in, n_out, n_cast):
    def kernel(*refs):
        ins, refs = refs[:n_in], refs[n_in:]
        slabs_in, refs = refs[:n_cast], refs[n_cast:]
        outs, refs = refs[:n_out], refs[n_out:]
        slabs_out, scratch = refs[:n_cast], refs[n_cast:]
        for src, dst in zip(slabs_in, slabs_out):
            dst[...] = src[...].astype(BF16)
        body(*ins, *outs, *scratch)
    return kernel


def _slab_specs(arrays, steps):
    ins, outs, shapes = [], [], []
    for a in arrays:
        rows, width = a.shape
        spec = pl.BlockSpec((rows // steps, width), lambda i: (i, 0))
        ins.append(spec)
        outs.append(spec)
        shapes.append(jax.ShapeDtypeStruct(a.shape, BF16))
    return ins, outs, shapes


def _rope_rows(x, cos_t, sin_t):
    half = cos_t.shape[0]
    out = []
    for j in range(x.shape[0] // (2 * half)):
        x1 = x[2 * half * j:2 * half * j + half]
        x2 = x[2 * half * j + half:2 * half * (j + 1)]
        out += [x1 * cos_t - x2 * sin_t, x2 * cos_t + x1 * sin_t]
    return jnp.concatenate(out, axis=0)


def _mla_proj_kernel(x_ref, w_lat_ref, gq_ref, gkv_ref, w_qnt_ref, w_qpt_ref, w_kn_ref, w_vt_ref,
                     cos_ref, sin_ref, cost_ref, sint_ref, qt_ref, k_ref, vt_ref, w_lat_bf_ref, *, scale):
    @pl.when(pl.program_id(0) == 0)
    def _():
        w_lat_bf_ref[...] = w_lat_ref[...].astype(BF16)

    xb = x_ref[...].astype(BF16)
    lat = _dot_nt(xb, w_lat_bf_ref[...])
    cq = _rms(lat[:, :MLA_Q_RANK], gq_ref[...], RMS_EPS).astype(BF16)
    ckv = _rms(lat[:, MLA_Q_RANK:MLA_Q_RANK + MLA_KV_RANK], gkv_ref[...], RMS_EPS).astype(BF16)
    kr = lat[:, MLA_Q_RANK + MLA_KV_RANK:]
    kr = jnp.where(lax.broadcasted_iota(jnp.int32, kr.shape, 1) < MLA_ROPE_DIM, kr, 0.0)
    k_pe = _rope(kr, cos_ref[...], sin_ref[...]).astype(BF16)
    kn = _dot(ckv, w_kn_ref[...])
    vt_ref[...] = _dot_nt(w_vt_ref[...], ckv).astype(BF16)
    qnt = _dot_nt(w_qnt_ref[...], cq)
    qpt = _dot_nt(w_qpt_ref[...], cq)
    cos_t, sin_t = cost_ref[...], sint_ref[...]
    pad = jnp.zeros((MLA_QK_PAD - MLA_NOPE_DIM - MLA_ROPE_DIM, qnt.shape[1]), BF16)
    for h in range(MLA_HEADS):
        base = h * MLA_QK_PAD
        qt_ref[base:base + MLA_NOPE_DIM, :] = (qnt[h * MLA_NOPE_DIM:(h + 1) * MLA_NOPE_DIM] * scale).astype(BF16)
        q_pe = _rope_rows(qpt[h * MLA_ROPE_DIM:(h + 1) * MLA_ROPE_DIM], cos_t, sin_t)
        qt_ref[base + MLA_NOPE_DIM:base + MLA_NOPE_DIM + MLA_ROPE_DIM, :] = (q_pe * scale).astype(BF16)
        qt_ref[base + MLA_NOPE_DIM + MLA_ROPE_DIM:base + MLA_QK_PAD, :] = pad
        k_ref[:, base:base + LANES] = kn[:, h * LANES:(h + 1) * LANES].astype(BF16)
        k_ref[:, base + LANES:base + 2 * LANES] = k_pe


def _diff_proj_kernel(x_ref, w_q_ref, w_k_ref, w_vt_ref, cos_ref, sin_ref, cost_ref, sint_ref,
                      qt_ref, k_ref, vt_ref, *, scale):
    xb = x_ref[...].astype(BF16)
    yk = _dot_nt(xb, w_k_ref[...])
    vt_ref[...] = _dot_nt(w_vt_ref[...], xb).astype(BF16)
    qt = _dot_nt(w_q_ref[...], xb)
    qt_ref[...] = (_rope_rows(qt, cost_ref[...], sint_ref[...]) * scale).astype(BF16)
    cos, sin = cos_ref[...], sin_ref[...]
    for h in range(DIFF_HEADS):
        lo, hi = h * LANES, (h + 1) * LANES
        k_ref[:, lo:hi] = _rope(yk[:, lo:hi], cos, sin).astype(BF16)


def _gate_proj_kernel(x_ref, w_ref, o_ref):
    xb = x_ref[...].astype(BF16)
    o_ref[...] = _sigmoid(_dot_nt(xb, w_ref[...])).astype(BF16)


def _flash_cols(qts, k_ref, vt_ref, scratch, q_tile, tq):
    n_heads = len(qts)
    per_head = len(scratch) // n_heads
    tk = tq // 2
    dk, cols = qts[0].shape
    dv = vt_ref.shape[2]
    ones = jnp.ones((SUM_ROWS, tk), BF16)

    class Head:
        def __init__(self, h):
            (s0, s1, p0, p1, self.m, a0, a1, self.acc) = scratch[h * per_head:(h + 1) * per_head]
            self.qt = qts[h]
            self.s, self.p, self.a = (s0, s1), (p0, p1), (a0, a1)
            self.lanes = slice(h * dk, (h + 1) * dk)
            self.h = h

    heads = [Head(h) for h in range(n_heads)]
    for hd in heads:
        hd.m[...] = jnp.full(hd.m.shape, -jnp.inf, F32)
        hd.acc[...] = jnp.zeros(hd.acc.shape, F32)
        hd.p[1][...] = jnp.zeros(hd.p[1].shape, BF16)
        hd.a[1][...] = jnp.ones(hd.a[1].shape, F32)

    all_groups = [(0, cols // LANES)]
    late_groups = [((m * tq + tk) // LANES, (m + 1) * tq // LANES) for m in range(cols // tq)]

    def scores(hd, c, half, runs=all_groups):
        k = k_ref[pl.ds(pl.multiple_of(c * tk, tk), tk), hd.lanes]
        for g0, g1 in runs:
            s = _dot(k, hd.qt[:, g0 * LANES:g1 * LANES])
            for g in range(g0, g1):
                hd.s[half][g] = s[:, (g - g0) * LANES:(g - g0 + 1) * LANES]

    def softmax(hd, half, diagonal, runs=all_groups):
        for g in [g for g0, g1 in runs for g in range(g0, g1)]:
            sl = slice(g * LANES, (g + 1) * LANES)
            s = hd.s[half][g]
            if diagonal:
                k_pos = half * tk + lax.broadcasted_iota(jnp.int32, (tk, LANES), 0)
                q_pos = (g * LANES) % tq + lax.broadcasted_iota(jnp.int32, (tk, LANES), 1)
                s = jnp.where(k_pos <= q_pos, s, -jnp.inf)
            m_old = hd.m[:, sl]
            m_new = jnp.maximum(m_old, jnp.max(s, axis=0, keepdims=True))
            a = jnp.exp2(m_old - m_new)
            p = jnp.exp2(s - m_new)
            hd.m[:, sl] = m_new
            hd.a[half][:, sl] = a
            hd.p[half][:, sl] = p.astype(BF16)

    def weighted_values(hd, pair, half, runs=all_groups):
        vt = jnp.concatenate([vt_ref[pair, hd.h][:, half * tk:(half + 1) * tk], ones], axis=0)
        for g0, g1 in runs:
            sl = slice(g0 * LANES, g1 * LANES)
            p = hd.p[half][:, sl]
            hd.acc[:, sl] = hd.a[half][:, sl] * hd.acc[:, sl] + _dot(vt, p)

    def pair_step(i, diagonal):
        second = late_groups if diagonal else all_groups
        for hd in heads:
            weighted_values(hd, jnp.maximum(i - 1, 0), 1)
        for hd in heads:
            scores(hd, 2 * i + 1, 1, second)
        for hd in heads:
            softmax(hd, 0, diagonal)
        if not diagonal:
            for hd in heads:
                scores(hd, 2 * i + 2, 0)
        for hd in heads:
            weighted_values(hd, i, 0)
        for hd in heads:
            softmax(hd, 1, diagonal, second)

    for hd in heads:
        scores(hd, 0, 0)
    lax.fori_loop(0, q_tile, lambda i, c: (pair_step(i, False), c)[1], 0)
    pair_step(q_tile, True)
    for hd in heads:
        weighted_values(hd, q_tile, 1, late_groups)
    return [(hd.acc[:dv, :], hd.acc[dv:dv + 1, :]) for hd in heads]


def _flash_scratch(n_heads, dv, cols, tq):
    tk = tq // 2
    stat = pltpu.VMEM((1, cols), F32)
    scores = pltpu.VMEM((cols // LANES, tk, LANES), F32)
    probs = pltpu.VMEM((tk, cols), BF16)
    return n_heads * [scores, scores, probs, probs,
                      stat, stat, stat, pltpu.VMEM((dv + SUM_ROWS, cols), F32)]


def _mla_attn_kernel(qt_ref, k_ref, vt_ref, o_ref, *scratch, tq, n_heads):
    qts, scratch = scratch[:n_heads], scratch[n_heads:]
    for h in range(n_heads):
        qts[h][...] = qt_ref[h]
    for h, (acc, l) in enumerate(_flash_cols(qts, k_ref, vt_ref, scratch, pl.program_id(2), tq)):
        o_ref[:, h * MLA_V_DIM:(h + 1) * MLA_V_DIM] = (acc / l).T.astype(o_ref.dtype)


def _diff_attn_kernel(qt_ref, k_ref, vt_ref, lq1_ref, lk1_ref, lq2_ref, lk2_ref, g_ref, o_ref, *scratch,
                      tq, n_heads, lambda_init):
    qt2_refs, scratch = scratch[:n_heads], scratch[n_heads:]
    row = lax.broadcasted_iota(jnp.int32, (LANES, tq), 0)
    zero = jnp.zeros((LANES, tq), BF16)
    for h in range(n_heads):
        qt = qt_ref[h]
        qt2_refs[h][:, :tq] = jnp.where(row < DIFF_QK_DIM, qt, zero)
        qt2_refs[h][:, tq:] = jnp.where(row >= DIFF_QK_DIM, qt, zero)
    lam = (jnp.exp(jnp.sum(lq1_ref[...] * lk1_ref[...], axis=-1, keepdims=True))
           - jnp.exp(jnp.sum(lq2_ref[...] * lk2_ref[...], axis=-1, keepdims=True)) + lambda_init)
    for h, (acc, l) in enumerate(_flash_cols(qt2_refs, k_ref, vt_ref, scratch, pl.program_id(2), tq)):
        o = acc / l
        o = (o[:, :tq] - lam * o[:, tq:]).T
        o_ref[:, h * DIFF_V_DIM:(h + 1) * DIFF_V_DIM] = (
            _rms(o, g_ref[...], SUBLN_EPS) * (1.0 - lambda_init)).astype(o_ref.dtype)


def _mix_out_kernel(a_ref, b_ref, sa_ref, sb_ref, x_ref, wa_ref, wb_ref, wo_ref, g_ref, beta_ref, o_ref):
    ya = _dot(a_ref[...], wa_ref[...])
    yb = _dot(b_ref[...], wb_ref[...])
    m = (sa_ref[...].astype(F32) * ya + sb_ref[...].astype(F32) * yb).astype(BF16)
    mixed = _dot(m, wo_ref[...])
    o_ref[...] = _layer_norm(ALPHA * x_ref[...] + mixed, g_ref[...], beta_ref[...])


def _ffn_kernel(h_ref, wg_ref, wu_ref, wd_ref, g_ref, beta_ref, o_ref, hb_ref, acc_ref):
    f = pl.program_id(1)

    @pl.when(f == 0)
    def _():
        h = h_ref[...]
        hb_ref[...] = h.astype(BF16)
        acc_ref[...] = ALPHA * h

    hb = hb_ref[...]
    gate = _dot(hb, wg_ref[...])
    up = _dot(hb, wu_ref[...])
    act = (gate * _sigmoid(gate) * up).astype(BF16)
    acc_ref[...] += _dot(act, wd_ref[...])

    @pl.when(f == pl.num_programs(1) - 1)
    def _():
        o_ref[...] = _layer_norm(acc_ref[...], g_ref[...], beta_ref[...])


def _rope_angles(seq, dim):
    inv_freq = 1.0 / (ROPE_THETA ** (jnp.arange(0, dim, 2, dtype=F32) / dim))
    ang = jnp.arange(seq, dtype=F32)[:, None] * inv_freq[None, :]
    return jnp.cos(ang), jnp.sin(ang)


def kernel(x, w_in, mla_q_norm, mla_w_uq, mla_kv_norm, mla_w_ukv, diff_lambda_q1, diff_lambda_k1,
           diff_lambda_q2, diff_lambda_k2, diff_subln, w_branch_a, w_branch_b, w_out, ln1_g, ln1_b,
           w_ffn_in, w_ffn_down, ln2_g, ln2_b):
    B, S, D = x.shape
    T = B * S
    H = MLA_HEADS
    d_ff = w_ffn_down.shape[1]
    lambda_init = 0.8 - 0.6 * math.exp(-0.3 * 0)
    x2 = x.reshape(T, D)

    w = w_in[0]
    n_diff = DIFF_HEADS * 2 * DIFF_QK_DIM
    o_kr = MLA_Q_RANK + MLA_KV_RANK
    o_dq = o_kr + MLA_ROPE_DIM
    wt = jnp.swapaxes(w, 0, 1)
    w_rows = lambda start, n: pl.BlockSpec((pl.Element(n), pl.Element(D)), lambda *_: (start, 0),
                                           pipeline_mode=pl.Buffered(1))
    uq = mla_w_uq[0].reshape(MLA_Q_RANK, H, MLA_NOPE_DIM + MLA_ROPE_DIM)
    w_qnt = uq[:, :, :MLA_NOPE_DIM].reshape(MLA_Q_RANK, H * MLA_NOPE_DIM).T.astype(BF16)
    w_qpt = uq[:, :, MLA_NOPE_DIM:].reshape(MLA_Q_RANK, H * MLA_ROPE_DIM).T.astype(BF16)
    ukv = mla_w_ukv[0].reshape(MLA_KV_RANK, H, MLA_NOPE_DIM + MLA_V_DIM)
    w_kn = ukv[:, :, :MLA_NOPE_DIM].reshape(MLA_KV_RANK, H * LANES).astype(BF16)
    w_vt = ukv[:, :, MLA_NOPE_DIM:].reshape(MLA_KV_RANK, H * MLA_V_DIM).T.astype(BF16)
    row = lambda v: v.reshape(1, -1).astype(F32)

    assert MLA_ROPE_DIM == DIFF_QK_DIM
    cos, sin = _rope_angles(S, MLA_ROPE_DIM)
    cos_k, sin_k = jnp.tile(cos, (1, 4)), jnp.tile(jnp.concatenate([-sin, sin], axis=1), (1, 2))
    cos_t, sin_t = cos.T, sin.T

    bm = PROJ_ROWS
    n_pos = S // bm
    tok = lambda n: pl.BlockSpec((bm, n), lambda i: (i, 0))
    pos = pl.BlockSpec((bm, LANES), lambda i: (i % n_pos, 0))
    vt_spec = lambda n: pl.BlockSpec((None, n, bm), lambda i: (i, 0, 0))
    vt_shape = lambda n: jax.ShapeDtypeStruct((T // bm, n, bm), BF16)
    pos_t = pl.BlockSpec((cos_t.shape[0], bm), lambda i: (0, i % n_pos))
    n_steps = T // bm
    rest_rows = (wt.shape[0] - o_dq) // n_steps
    row_unit = math.gcd(o_dq, rest_rows)
    qt_a, k_a, vt_a, w_rest = pl.pallas_call(
        _with_casts(functools.partial(_mla_proj_kernel, scale=LOG2_E * (MLA_NOPE_DIM + MLA_ROPE_DIM) ** -0.5),
                    12, 3, 1),
        grid=(n_steps,),
        in_specs=[tok(D), w_rows(0, o_kr + LANES), _const_spec((1, MLA_Q_RANK)), _const_spec((1, MLA_KV_RANK)),
                  _const_spec(w_qnt.shape), _const_spec(w_qpt.shape), _const_spec(w_kn.shape),
                  _const_spec(w_vt.shape), pos, pos, pos_t, pos_t,
                  pl.BlockSpec((pl.Element(rest_rows), pl.Element(D)),
                               lambda i: ((o_dq // row_unit + rest_rows // row_unit * i) * row_unit, 0))],
        out_specs=[vt_spec(H * MLA_QK_PAD), tok(H * MLA_QK_PAD), vt_spec(H * MLA_V_DIM),
                   pl.BlockSpec((rest_rows, D), lambda i: (i, 0))],
        out_shape=[vt_shape(H * MLA_QK_PAD), jax.ShapeDtypeStruct((T, H * MLA_QK_PAD), BF16),
                   vt_shape(H * MLA_V_DIM), jax.ShapeDtypeStruct((wt.shape[0] - o_dq, D), BF16)],
        scratch_shapes=[pltpu.VMEM((o_kr + LANES, D), BF16)],
        compiler_params=_params("arbitrary"), name="mla_proj",
    )(x2, wt, row(mla_q_norm), row(mla_kv_norm), w_qnt, w_qpt, w_kn, w_vt, cos_k, sin_k, cos_t, sin_t, wt)

    c_in, c_out, c_shape = _slab_specs([w_ffn_down[0]], T // bm)
    qt_b, k_b, vt_b, w_fd = pl.pallas_call(
        _with_casts(functools.partial(_diff_proj_kernel, scale=LOG2_E * DIFF_QK_DIM ** -0.5), 8, 3, 1),
        grid=(T // bm,),
        in_specs=[tok(D), w_rows(0, n_diff), w_rows(n_diff, n_diff), w_rows(2 * n_diff, n_diff),
                  pos, pos, pos_t, pos_t] + c_in,
        out_specs=[vt_spec(n_diff), tok(n_diff), vt_spec(n_diff)] + c_out,
        out_shape=[vt_shape(n_diff), jax.ShapeDtypeStruct((T, n_diff), BF16), vt_shape(n_diff)] + c_shape,
        compiler_params=_params("parallel"), name="diff_proj",
    )(x2, w_rest, w_rest, w_rest, cos_k, sin_k, cos_t, sin_t, w_ffn_down[0])

    bmg = GATE_ROWS
    mix_weights = [w_branch_a[0], w_branch_b[0], w_out[0]]
    c_in, c_out, c_shape = _slab_specs(mix_weights, T // bmg)
    sig, w_a, w_b, w_o = pl.pallas_call(
        _with_casts(_gate_proj_kernel, 2, 1, len(mix_weights)),
        grid=(T // bmg,),
        in_specs=[pl.BlockSpec((bmg, D), lambda i: (i, 0)), w_rows(3 * n_diff, 2 * D)] + c_in,
        out_specs=[pl.BlockSpec((bmg, 2 * D), lambda i: (i, 0))] + c_out,
        out_shape=[jax.ShapeDtypeStruct((T, 2 * D), BF16)] + c_shape,
        compiler_params=_params("parallel"), name="gate_proj",
    )(x2, w_rest, *mix_weights)

    tq = bm
    hp = ATTN_HEADS_PER_STEP
    head_blk = lambda rows, width, full: pl.BlockSpec(
        (None, rows, hp * width), (lambda b, h, i: (b, 0, h)) if full else (lambda b, h, i: (b, i, h)))
    vt_blk = lambda dv: pl.BlockSpec((None, S // tq, hp, dv, tq), lambda b, h, i: (b, 0, h, 0, 0))
    qt_blk = lambda dk: pl.BlockSpec((None, None, hp, dk, tq), lambda b, h, i: (b, i, h, 0, 0))
    attn_a = pl.pallas_call(
        functools.partial(_mla_attn_kernel, tq=tq, n_heads=hp),
        grid=(B, H // hp, S // tq),
        in_specs=[qt_blk(MLA_QK_PAD), head_blk(S, MLA_QK_PAD, True), vt_blk(MLA_V_DIM)],
        out_specs=head_blk(tq, MLA_V_DIM, False),
        out_shape=jax.ShapeDtypeStruct((B, S, H * MLA_V_DIM), BF16),
        scratch_shapes=hp * [pltpu.VMEM((MLA_QK_PAD, tq), BF16)] + _flash_scratch(hp, MLA_V_DIM, tq, tq),
        compiler_params=_params("parallel", "parallel", "parallel"), name="mla_attn",
    )(qt_a.reshape(B, S // tq, H, MLA_QK_PAD, tq), k_a.reshape(B, S, -1),
      vt_a.reshape(B, S // tq, H, MLA_V_DIM, tq))

    lam_spec = pl.BlockSpec((1, DIFF_QK_DIM), lambda b, h, i: (0, 0))
    attn_b = pl.pallas_call(
        functools.partial(_diff_attn_kernel, tq=tq, n_heads=hp, lambda_init=lambda_init),
        grid=(B, DIFF_HEADS // hp, S // tq),
        in_specs=[qt_blk(LANES), head_blk(S, LANES, True), vt_blk(DIFF_V_DIM),
                  lam_spec, lam_spec, lam_spec, lam_spec, pl.BlockSpec((1, DIFF_V_DIM), lambda b, h, i: (0, 0))],
        out_specs=head_blk(tq, DIFF_V_DIM, False),
        out_shape=jax.ShapeDtypeStruct((B, S, DIFF_HEADS * DIFF_V_DIM), BF16),
        scratch_shapes=hp * [pltpu.VMEM((LANES, 2 * tq), BF16)] + _flash_scratch(hp, DIFF_V_DIM, 2 * tq, tq),
        compiler_params=_params("parallel", "parallel", "parallel"), name="diff_attn",
    )(qt_b.reshape(B, S // tq, DIFF_HEADS, LANES, tq), k_b.reshape(B, S, -1),
      vt_b.reshape(B, S // tq, DIFF_HEADS, DIFF_V_DIM, tq),
      row(diff_lambda_q1), row(diff_lambda_k1), row(diff_lambda_q2), row(diff_lambda_k2), row(diff_subln))

    bmo = MIX_ROWS
    tokm = lambda n, j=0: pl.BlockSpec((bmo, n), lambda i: (i, j))
    c_in, c_out, c_shape = _slab_specs([w_ffn_in[0]], T // bmo)
    h1, w_fi = pl.pallas_call(
        _with_casts(_mix_out_kernel, 10, 1, 1),
        grid=(T // bmo,),
        in_specs=[tokm(w_a.shape[0]), tokm(w_b.shape[0]), tokm(D, 0), tokm(D, 1), tokm(D),
                  _const_spec(w_a.shape), _const_spec(w_b.shape), _const_spec(w_o.shape),
                  _const_spec((1, D)), _const_spec((1, D))] + c_in,
        out_specs=[tokm(D)] + c_out,
        out_shape=[jax.ShapeDtypeStruct((T, D), F32)] + c_shape,
        compiler_params=_params("parallel"), name="mix_out",
    )(attn_a.reshape(T, -1), attn_b.reshape(T, -1), sig, sig, x2, w_a, w_b, w_o, row(ln1_g), row(ln1_b), w_ffn_in[0])

    bmf, tf = FFN_ROWS, FFN_COLS
    n_f = d_ff // tf
    out = pl.pallas_call(
        _ffn_kernel,
        grid=(T // bmf, n_f),
        in_specs=[pl.BlockSpec((bmf, D), lambda i, f: (i, 0)),
                  pl.BlockSpec((D, tf), lambda i, f: (0, f)),
                  pl.BlockSpec((D, tf), lambda i, f: (0, n_f + f)),
                  pl.BlockSpec((tf, D), lambda i, f: (f, 0)),
                  pl.BlockSpec((1, D), lambda i, f: (0, 0)), pl.BlockSpec((1, D), lambda i, f: (0, 0))],
        out_specs=pl.BlockSpec((bmf, D), lambda i, f: (i, 0)),
        out_shape=jax.ShapeDtypeStruct((T, D), F32),
        scratch_shapes=[pltpu.VMEM((bmf, D), BF16), pltpu.VMEM((bmf, D), F32)],
        compiler_params=_params("parallel", "arbitrary"), name="ffn",
    )(h1, w_fi, w_fi, w_fd, row(ln2_g), row(ln2_b))

    return out.reshape(B, S, D)
```

```python
import functools
import math

import jax
import jax.numpy as jnp
from jax import lax
from jax.experimental import pallas as pl
from jax.experimental.pallas import tpu as pltpu

MLA_HEADS = 8
MLA_NOPE_DIM = 128
MLA_ROPE_DIM = 64
MLA_V_DIM = 128
MLA_Q_RANK = 512
MLA_KV_RANK = 512
DIFF_HEADS = 8
DIFF_QK_DIM = 64
DIFF_V_DIM = 128
ROPE_THETA = 10000.0
DEPTH = 1
ALPHA = (2 * DEPTH) ** 0.25
RMS_EPS = 1e-6
SUBLN_EPS = 1e-5
LN_EPS = 1e-5

LOG2_E = math.log2(math.e)
LANES = 128
MLA_QK_PAD = 256
SUM_ROWS = 16
ROPE_HALF = MLA_ROPE_DIM // 2
VMEM_LIMIT = 56 * 1024 * 1024

PROJ_ROWS = 512
GATE_ROWS = 512
MIX_ROWS = 256
FFN_ROWS = 512
FFN_COLS = 512
ATTN_HEADS_PER_STEP = 4

BF16 = jnp.bfloat16
F32 = jnp.float32


def _dot(a, b):
    return jnp.dot(a, b, preferred_element_type=F32)


def _dot_nt(a, b):
    return lax.dot_general(a, b, (((1,), (1,)), ((), ())), preferred_element_type=F32)


def _rms(x, g, eps):
    return x * lax.rsqrt(jnp.mean(x * x, axis=-1, keepdims=True) + eps) * g


def _layer_norm(x, g, b):
    mu = jnp.mean(x, axis=-1, keepdims=True)
    xc = x - mu
    var = jnp.mean(xc * xc, axis=-1, keepdims=True)
    return xc * lax.rsqrt(var + LN_EPS) * g + b


def _rope(x, cos, sin_signed):
    up = pltpu.roll(x, LANES - ROPE_HALF, 1)
    dn = pltpu.roll(x, ROPE_HALF, 1)
    first_half = lax.broadcasted_iota(jnp.int32, x.shape, 1) % (2 * ROPE_HALF) < ROPE_HALF
    return x * cos + jnp.where(first_half, up, dn) * sin_signed


def _sigmoid(x):
    return 0.5 + 0.5 * jnp.tanh(0.5 * x)


def _const_spec(shape):
    return pl.BlockSpec(shape, lambda *_: (0,) * len(shape), pipeline_mode=pl.Buffered(1))


def _params(*sem):
    return pltpu.CompilerParams(dimension_semantics=sem, vmem_limit_bytes=VMEM_LIMIT)


def _with_casts(body, n_in, n_out, n_cast):
    def kernel(*refs):
        ins, refs = refs[:n_in], refs[n_in:]
        slabs_in, refs = refs[:n_cast], refs[n_cast:]
        outs, refs = refs[:n_out], refs[n_out:]
        slabs_out, scratch = refs[:n_cast], refs[n_cast:]
        for src, dst in zip(slabs_in, slabs_out):
            dst[...] = src[...].astype(BF16)
        body(*ins, *outs, *scratch)
    return kernel


def _slab_specs(arrays, steps):
    ins, outs, shapes = [], [], []
    for a in arrays:
        rows, width = a.shape
        spec = pl.BlockSpec((rows // steps, width), lambda i: (i, 0))
        ins.append(spec)
        outs.append(spec)
        shapes.append(jax.ShapeDtypeStruct(a.shape, BF16))
    return ins, outs, shapes


def _rope_rows(x, cos_t, sin_t):
    half = cos_t.shape[0]
    out = []
    for j in range(x.shape[0] // (2 * half)):
        x1 = x[2 * half * j:2 * half * j + half]
        x2 = x[2 * half * j + half:2 * half * (j + 1)]
        out += [x1 * cos_t - x2 * sin_t, x2 * cos_t + x1 * sin_t]
    return jnp.concatenate(out, axis=0)


def _mla_proj_kernel(x_ref, w_lat_ref, gq_ref, gkv_ref, w_qnt_ref, w_qpt_ref, w_kn_ref, w_vt_ref,
                     cos_ref, sin_ref, cost_ref, sint_ref, qt_ref, k_ref, vt_ref, w_lat_bf_ref, *, scale):
    @pl.when(pl.program_id(0) == 0)
    def _():
        w_lat_bf_ref[...] = w_lat_ref[...].astype(BF16)

    xb = x_ref[...].astype(BF16)
    lat = _dot_nt(xb, w_lat_bf_ref[...])
    cq = _rms(lat[:, :MLA_Q_RANK], gq_ref[...], RMS_EPS).astype(BF16)
    ckv = _rms(lat[:, MLA_Q_RANK:MLA_Q_RANK + MLA_KV_RANK], gkv_ref[...], RMS_EPS).astype(BF16)
    kr = lat[:, MLA_Q_RANK + MLA_KV_RANK:]
    kr = jnp.where(lax.broadcasted_iota(jnp.int32, kr.shape, 1) < MLA_ROPE_DIM, kr, 0.0)
    k_pe = _rope(kr, cos_ref[...], sin_ref[...]).astype(BF16)
    kn = _dot(ckv, w_kn_ref[...])
    vt_ref[...] = _dot_nt(w_vt_ref[...], ckv).astype(BF16)
    qnt = _dot_nt(w_qnt_ref[...], cq)
    qpt = _dot_nt(w_qpt_ref[...], cq)
    cos_t, sin_t = cost_ref[...], sint_ref[...]
    pad = jnp.zeros((MLA_QK_PAD - MLA_NOPE_DIM - MLA_ROPE_DIM, qnt.shape[1]), BF16)
    for h in range(MLA_HEADS):
        base = h * MLA_QK_PAD
        qt_ref[base:base + MLA_NOPE_DIM, :] = (qnt[h * MLA_NOPE_DIM:(h + 1) * MLA_NOPE_DIM] * scale).astype(BF16)
        q_pe = _rope_rows(qpt[h * MLA_ROPE_DIM:(h + 1) * MLA_ROPE_DIM], cos_t, sin_t)
        qt_ref[base + MLA_NOPE_DIM:base + MLA_NOPE_DIM + MLA_ROPE_DIM, :] = (q_pe * scale).astype(BF16)
        qt_ref[base + MLA_NOPE_DIM + MLA_ROPE_DIM:base + MLA_QK_PAD, :] = pad
        k_ref[:, base:base + LANES] = kn[:, h * LANES:(h + 1) * LANES].astype(BF16)
        k_ref[:, base + LANES:base + 2 * LANES] = k_pe


def _diff_proj_kernel(x_ref, w_q_ref, w_k_ref, w_vt_ref, cos_ref, sin_ref, cost_ref, sint_ref,
                      qt_ref, k_ref, vt_ref, *, scale):
    xb = x_ref[...].astype(BF16)
    yk = _dot_nt(xb, w_k_ref[...])
    vt_ref[...] = _dot_nt(w_vt_ref[...], xb).astype(BF16)
    qt = _dot_nt(w_q_ref[...], xb)
    qt_ref[...] = (_rope_rows(qt, cost_ref[...], sint_ref[...]) * scale).astype(BF16)
    cos, sin = cos_ref[...], sin_ref[...]
    for h in range(DIFF_HEADS):
        lo, hi = h * LANES, (h + 1) * LANES
        k_ref[:, lo:hi] = _rope(yk[:, lo:hi], cos, sin).astype(BF16)


def _gate_proj_kernel(x_ref, w_ref, o_ref):
    xb = x_ref[...].astype(BF16)
    o_ref[...] = _sigmoid(_dot_nt(xb, w_ref[...])).astype(BF16)


def _flash_cols(qts, k_ref, vt_ref, scratch, q_tile, tq):
    n_heads = len(qts)
    per_head = len(scratch) // n_heads
    tk = tq // 2
    dk, cols = qts[0].shape
    dv = vt_ref.shape[2]
    ones = jnp.ones((SUM_ROWS, tk), BF16)

    class Head:
        def __init__(self, h):
            (s0, s1, p0, p1, self.m, a0, a1, self.acc) = scratch[h * per_head:(h + 1) * per_head]
            self.qt = qts[h]
            self.s, self.p, self.a = (s0, s1), (p0, p1), (a0, a1)
            self.lanes = slice(h * dk, (h + 1) * dk)
            self.h = h

    heads = [Head(h) for h in range(n_heads)]
    for hd in heads:
        hd.m[...] = jnp.full(hd.m.shape, -jnp.inf, F32)
        hd.acc[...] = jnp.zeros(hd.acc.shape, F32)
        hd.p[1][...] = jnp.zeros(hd.p[1].shape, BF16)
        hd.a[1][...] = jnp.ones(hd.a[1].shape, F32)

    all_groups = [(0, cols // LANES)]
    late_groups = [((m * tq + tk) // LANES, (m + 1) * tq // LANES) for m in range(cols // tq)]

    def scores(hd, c, half, runs=all_groups):
        k = k_ref[pl.ds(pl.multiple_of(c * tk, tk), tk), hd.lanes]
        for g0, g1 in runs:
            s = _dot(k, hd.qt[:, g0 * LANES:g1 * LANES])
            for g in range(g0, g1):
                hd.s[half][g] = s[:, (g - g0) * LANES:(g - g0 + 1) * LANES]

    def softmax(hd, half, diagonal, runs=all_groups):
        for g in [g for g0, g1 in runs for g in range(g0, g1)]:
            sl = slice(g * LANES, (g + 1) * LANES)
            s = hd.s[half][g]
            if diagonal:
                k_pos = half * tk + lax.broadcasted_iota(jnp.int32, (tk, LANES), 0)
                q_pos = (g * LANES) % tq + lax.broadcasted_iota(jnp.int32, (tk, LANES), 1)
                s = jnp.where(k_pos <= q_pos, s, -jnp.inf)
            m_old = hd.m[:, sl]
            m_new = jnp.maximum(m_old, jnp.max(s, axis=0, keepdims=True))
            a = jnp.exp2(m_old - m_new)
            p = jnp.exp2(s - m_new)
            hd.m[:, sl] = m_new
            hd.a[half][:, sl] = a
            hd.p[half][g] = p.astype(BF16)

    def weighted_values(hd, pair, half, runs=all_groups):
        vt = jnp.concatenate([vt_ref[pair, hd.h][:, half * tk:(half + 1) * tk], ones], axis=0)
        for g0, g1 in runs:
            sl = slice(g0 * LANES, g1 * LANES)
            p = jnp.concatenate([hd.p[half][g] for g in range(g0, g1)], axis=1)
            hd.acc[:, sl] = hd.a[half][:, sl] * hd.acc[:, sl] + _dot(vt, p)

    def pair_step(i, diagonal):
        second = late_groups if diagonal else all_groups
        for hd in heads:
            weighted_values(hd, jnp.maximum(i - 1, 0), 1)
        for hd in heads:
            scores(hd, 2 * i + 1, 1, second)
        for hd in heads:
            softmax(hd, 0, diagonal)
        if not diagonal:
            for hd in heads:
                scores(hd, 2 * i + 2, 0)
        for hd in heads:
            weighted_values(hd, i, 0)
        for hd in heads:
            softmax(hd, 1, diagonal, second)

    for hd in heads:
        scores(hd, 0, 0)
    lax.fori_loop(0, q_tile, lambda i, c: (pair_step(i, False), c)[1], 0)
    pair_step(q_tile, True)
    for hd in heads:
        weighted_values(hd, q_tile, 1, late_groups)
    return [(hd.acc[:dv, :], hd.acc[dv:dv + 1, :]) for hd in heads]


def _flash_scratch(n_heads, dv, cols, tq):
    tk = tq // 2
    stat = pltpu.VMEM((1, cols), F32)
    group_major = lambda dtype: pltpu.VMEM((cols // LANES, tk, LANES), dtype)
    return n_heads * [group_major(F32), group_major(F32), group_major(BF16), group_major(BF16),
                      stat, stat, stat, pltpu.VMEM((dv + SUM_ROWS, cols), F32)]


def _mla_attn_kernel(qt_ref, k_ref, vt_ref, o_ref, *scratch, tq, n_heads):
    qts, scratch = scratch[:n_heads], scratch[n_heads:]
    for h in range(n_heads):
        qts[h][...] = qt_ref[h]
    for h, (acc, l) in enumerate(_flash_cols(qts, k_ref, vt_ref, scratch, pl.program_id(2), tq)):
        o_ref[:, h * MLA_V_DIM:(h + 1) * MLA_V_DIM] = (acc / l).T.astype(o_ref.dtype)


def _diff_attn_kernel(qt_ref, k_ref, vt_ref, lq1_ref, lk1_ref, lq2_ref, lk2_ref, g_ref, o_ref, *scratch,
                      tq, n_heads, lambda_init):
    qt2_refs, scratch = scratch[:n_heads], scratch[n_heads:]
    row = lax.broadcasted_iota(jnp.int32, (LANES, tq), 0)
    zero = jnp.zeros((LANES, tq), BF16)
    for h in range(n_heads):
        qt = qt_ref[h]
        qt2_refs[h][:, :tq] = jnp.where(row < DIFF_QK_DIM, qt, zero)
        qt2_refs[h][:, tq:] = jnp.where(row >= DIFF_QK_DIM, qt, zero)
    lam = (jnp.exp(jnp.sum(lq1_ref[...] * lk1_ref[...], axis=-1, keepdims=True))
           - jnp.exp(jnp.sum(lq2_ref[...] * lk2_ref[...], axis=-1, keepdims=True)) + lambda_init)
    for h, (acc, l) in enumerate(_flash_cols(qt2_refs, k_ref, vt_ref, scratch, pl.program_id(2), tq)):
        o = acc / l
        o = (o[:, :tq] - lam * o[:, tq:]).T
        o_ref[:, h * DIFF_V_DIM:(h + 1) * DIFF_V_DIM] = (
            _rms(o, g_ref[...], SUBLN_EPS) * (1.0 - lambda_init)).astype(o_ref.dtype)


def _mix_out_kernel(a_ref, b_ref, sa_ref, sb_ref, x_ref, wa_ref, wb_ref, wo_ref, g_ref, beta_ref, o_ref):
    ya = _dot(a_ref[...], wa_ref[...])
    yb = _dot(b_ref[...], wb_ref[...])
    m = (sa_ref[...].astype(F32) * ya + sb_ref[...].astype(F32) * yb).astype(BF16)
    mixed = _dot(m, wo_ref[...])
    o_ref[...] = _layer_norm(ALPHA * x_ref[...] + mixed, g_ref[...], beta_ref[...])


def _ffn_kernel(h_ref, wg_ref, wu_ref, wd_ref, g_ref, beta_ref, o_ref, hb_ref, acc_ref):
    f = pl.program_id(1)

    @pl.when(f == 0)
    def _():
        h = h_ref[...]
        hb_ref[...] = h.astype(BF16)
        acc_ref[...] = ALPHA * h

    hb = hb_ref[...]
    gate = _dot(hb, wg_ref[...])
    up = _dot(hb, wu_ref[...])
    act = (gate * _sigmoid(gate) * up).astype(BF16)
    acc_ref[...] += _dot(act, wd_ref[...])

    @pl.when(f == pl.num_programs(1) - 1)
    def _():
        o_ref[...] = _layer_norm(acc_ref[...], g_ref[...], beta_ref[...])


def _rope_angles(seq, dim):
    inv_freq = 1.0 / (ROPE_THETA ** (jnp.arange(0, dim, 2, dtype=F32) / dim))
    ang = jnp.arange(seq, dtype=F32)[:, None] * inv_freq[None, :]
    return jnp.cos(ang), jnp.sin(ang)


def kernel(x, w_in, mla_q_norm, mla_w_uq, mla_kv_norm, mla_w_ukv, diff_lambda_q1, diff_lambda_k1,
           diff_lambda_q2, diff_lambda_k2, diff_subln, w_branch_a, w_branch_b, w_out, ln1_g, ln1_b,
           w_ffn_in, w_ffn_down, ln2_g, ln2_b):
    B, S, D = x.shape
    T = B * S
    H = MLA_HEADS
    d_ff = w_ffn_down.shape[1]
    lambda_init = 0.8 - 0.6 * math.exp(-0.3 * 0)
    x2 = x.reshape(T, D)

    w = w_in[0]
    n_diff = DIFF_HEADS * 2 * DIFF_QK_DIM
    o_kr = MLA_Q_RANK + MLA_KV_RANK
    o_dq = o_kr + MLA_ROPE_DIM
    wt = jnp.swapaxes(w, 0, 1)
    w_rows = lambda start, n: pl.BlockSpec((pl.Element(n), pl.Element(D)), lambda *_: (start, 0),
                                           pipeline_mode=pl.Buffered(1))
    uq = mla_w_uq[0].reshape(MLA_Q_RANK, H, MLA_NOPE_DIM + MLA_ROPE_DIM)
    w_qnt = uq[:, :, :MLA_NOPE_DIM].reshape(MLA_Q_RANK, H * MLA_NOPE_DIM).T.astype(BF16)
    w_qpt = uq[:, :, MLA_NOPE_DIM:].reshape(MLA_Q_RANK, H * MLA_ROPE_DIM).T.astype(BF16)
    ukv = mla_w_ukv[0].reshape(MLA_KV_RANK, H, MLA_NOPE_DIM + MLA_V_DIM)
    w_kn = ukv[:, :, :MLA_NOPE_DIM].reshape(MLA_KV_RANK, H * LANES).astype(BF16)
    w_vt = ukv[:, :, MLA_NOPE_DIM:].reshape(MLA_KV_RANK, H * MLA_V_DIM).T.astype(BF16)
    row = lambda v: v.reshape(1, -1).astype(F32)

    assert MLA_ROPE_DIM == DIFF_QK_DIM
    cos, sin = _rope_angles(S, MLA_ROPE_DIM)
    cos_k, sin_k = jnp.tile(cos, (1, 4)), jnp.tile(jnp.concatenate([-sin, sin], axis=1), (1, 2))
    cos_t, sin_t = cos.T, sin.T

    bm = PROJ_ROWS
    n_pos = S // bm
    tok = lambda n: pl.BlockSpec((bm, n), lambda i: (i, 0))
    pos = pl.BlockSpec((bm, LANES), lambda i: (i % n_pos, 0))
    vt_spec = lambda n: pl.BlockSpec((None, n, bm), lambda i: (i, 0, 0))
    vt_shape = lambda n: jax.ShapeDtypeStruct((T // bm, n, bm), BF16)
    pos_t = pl.BlockSpec((cos_t.shape[0], bm), lambda i: (0, i % n_pos))
    n_steps = T // bm
    rest_rows = (wt.shape[0] - o_dq) // n_steps
    row_unit = math.gcd(o_dq, rest_rows)
    qt_a, k_a, vt_a, w_rest = pl.pallas_call(
        _with_casts(functools.partial(_mla_proj_kernel, scale=LOG2_E * (MLA_NOPE_DIM + MLA_ROPE_DIM) ** -0.5),
                    12, 3, 1),
        grid=(n_steps,),
        in_specs=[tok(D), w_rows(0, o_kr + LANES), _const_spec((1, MLA_Q_RANK)), _const_spec((1, MLA_KV_RANK)),
                  _const_spec(w_qnt.shape), _const_spec(w_qpt.shape), _const_spec(w_kn.shape),
                  _const_spec(w_vt.shape), pos, pos, pos_t, pos_t,
                  pl.BlockSpec((pl.Element(rest_rows), pl.Element(D)),
                               lambda i: ((o_dq // row_unit + rest_rows // row_unit * i) * row_unit, 0))],
        out_specs=[vt_spec(H * MLA_QK_PAD), tok(H * MLA_QK_PAD), vt_spec(H * MLA_V_DIM),
                   pl.BlockSpec((rest_rows, D), lambda i: (i, 0))],
        out_shape=[vt_shape(H * MLA_QK_PAD), jax.ShapeDtypeStruct((T, H * MLA_QK_PAD), BF16),
                   vt_shape(H * MLA_V_DIM), jax.ShapeDtypeStruct((wt.shape[0] - o_dq, D), BF16)],
        scratch_shapes=[pltpu.VMEM((o_kr + LANES, D), BF16)],
        compiler_params=_params("arbitrary"), name="mla_proj",
    )(x2, wt, row(mla_q_norm), row(mla_kv_norm), w_qnt, w_qpt, w_kn, w_vt, cos_k, sin_k, cos_t, sin_t, wt)

    c_in, c_out, c_shape = _slab_specs([w_ffn_down[0]], T // bm)
    qt_b, k_b, vt_b, w_fd = pl.pallas_call(
        _with_casts(functools.partial(_diff_proj_kernel, scale=LOG2_E * DIFF_QK_DIM ** -0.5), 8, 3, 1),
        grid=(T // bm,),
        in_specs=[tok(D), w_rows(0, n_diff), w_rows(n_diff, n_diff), w_rows(2 * n_diff, n_diff),
                  pos, pos, pos_t, pos_t] + c_in,
        out_specs=[vt_spec(n_diff), tok(n_diff), vt_spec(n_diff)] + c_out,
        out_shape=[vt_shape(n_diff), jax.ShapeDtypeStruct((T, n_diff), BF16), vt_shape(n_diff)] + c_shape,
        compiler_params=_params("parallel"), name="diff_proj",
    )(x2, w_rest, w_rest, w_rest, cos_k, sin_k, cos_t, sin_t, w_ffn_down[0])

    bmg = GATE_ROWS
    mix_weights = [w_branch_a[0], w_branch_b[0], w_out[0]]
    c_in, c_out, c_shape = _slab_specs(mix_weights, T // bmg)
    sig, w_a, w_b, w_o = pl.pallas_call(
        _with_casts(_gate_proj_kernel, 2, 1, len(mix_weights)),
        grid=(T // bmg,),
        in_specs=[pl.BlockSpec((bmg, D), lambda i: (i, 0)), w_rows(3 * n_diff, 2 * D)] + c_in,
        out_specs=[pl.BlockSpec((bmg, 2 * D), lambda i: (i, 0))] + c_out,
        out_shape=[jax.ShapeDtypeStruct((T, 2 * D), BF16)] + c_shape,
        compiler_params=_params("parallel"), name="gate_proj",
    )(x2, w_rest, *mix_weights)

    tq = bm
    hp = ATTN_HEADS_PER_STEP
    head_blk = lambda rows, width, full: pl.BlockSpec(
        (None, rows, hp * width), (lambda b, h, i: (b, 0, h)) if full else (lambda b, h, i: (b, i, h)))
    vt_blk = lambda dv: pl.BlockSpec((None, S // tq, hp, dv, tq), lambda b, h, i: (b, 0, h, 0, 0))
    qt_blk = lambda dk: pl.BlockSpec((None, None, hp, dk, tq), lambda b, h, i: (b, i, h, 0, 0))
    attn_a = pl.pallas_call(
        functools.partial(_mla_attn_kernel, tq=tq, n_heads=hp),
        grid=(B, H // hp, S // tq),
        in_specs=[qt_blk(MLA_QK_PAD), head_blk(S, MLA_QK_PAD, True), vt_blk(MLA_V_DIM)],
        out_specs=head_blk(tq, MLA_V_DIM, False),
        out_shape=jax.ShapeDtypeStruct((B, S, H * MLA_V_DIM), BF16),
        scratch_shapes=hp * [pltpu.VMEM((MLA_QK_PAD, tq), BF16)] + _flash_scratch(hp, MLA_V_DIM, tq, tq),
        compiler_params=_params("parallel", "parallel", "parallel"), name="mla_attn",
    )(qt_a.reshape(B, S // tq, H, MLA_QK_PAD, tq), k_a.reshape(B, S, -1),
      vt_a.reshape(B, S // tq, H, MLA_V_DIM, tq))

    lam_spec = pl.BlockSpec((1, DIFF_QK_DIM), lambda b, h, i: (0, 0))
    attn_b = pl.pallas_call(
        functools.partial(_diff_attn_kernel, tq=tq, n_heads=hp, lambda_init=lambda_init),
        grid=(B, DIFF_HEADS // hp, S // tq),
        in_specs=[qt_blk(LANES), head_blk(S, LANES, True), vt_blk(DIFF_V_DIM),
                  lam_spec, lam_spec, lam_spec, lam_spec, pl.BlockSpec((1, DIFF_V_DIM), lambda b, h, i: (0, 0))],
        out_specs=head_blk(tq, DIFF_V_DIM, False),
        out_shape=jax.ShapeDtypeStruct((B, S, DIFF_HEADS * DIFF_V_DIM), BF16),
        scratch_shapes=hp * [pltpu.VMEM((LANES, 2 * tq), BF16)] + _flash_scratch(hp, DIFF_V_DIM, 2 * tq, tq),
        compiler_params=_params("parallel", "parallel", "parallel"), name="diff_attn",
    )(qt_b.reshape(B, S // tq, DIFF_HEADS, LANES, tq), k_b.reshape(B, S, -1),
      vt_b.reshape(B, S // tq, DIFF_HEADS, DIFF_V_DIM, tq),
      row(diff_lambda_q1), row(diff_lambda_k1), row(diff_lambda_q2), row(diff_lambda_k2), row(diff_subln))

    bmo = MIX_ROWS
    tokm = lambda n, j=0: pl.BlockSpec((bmo, n), lambda i: (i, j))
    c_in, c_out, c_shape = _slab_specs([w_ffn_in[0]], T // bmo)
    h1, w_fi = pl.pallas_call(
        _with_casts(_mix_out_kernel, 10, 1, 1),
        grid=(T // bmo,),
        in_specs=[tokm(w_a.shape[0]), tokm(w_b.shape[0]), tokm(D, 0), tokm(D, 1), tokm(D),
                  _const_spec(w_a.shape), _const_spec(w_b.shape), _const_spec(w_o.shape),
                  _const_spec((1, D)), _const_spec((1, D))] + c_in,
        out_specs=[tokm(D)] + c_out,
        out_shape=[jax.ShapeDtypeStruct((T, D), F32)] + c_shape,
        compiler_params=_params("parallel"), name="mix_out",
    )(attn_a.reshape(T, -1), attn_b.reshape(T, -1), sig, sig, x2, w_a, w_b, w_o, row(ln1_g), row(ln1_b), w_ffn_in[0])

    bmf, tf = FFN_ROWS, FFN_COLS
    n_f = d_ff // tf
    out = pl.pallas_call(
        _ffn_kernel,
        grid=(T // bmf, n_f),
        in_specs=[pl.BlockSpec((bmf, D), lambda i, f: (i, 0)),
                  pl.BlockSpec((D, tf), lambda i, f: (0, f)),
                  pl.BlockSpec((D, tf), lambda i, f: (0, n_f + f)),
                  pl.BlockSpec((tf, D), lambda i, f: (f, 0)),
                  pl.BlockSpec((1, D), lambda i, f: (0, 0)), pl.BlockSpec((1, D), lambda i, f: (0, 0))],
        out_specs=pl.BlockSpec((bmf, D), lambda i, f: (i, 0)),
        out_shape=jax.ShapeDtypeStruct((T, D), F32),
        scratch_shapes=[pltpu.VMEM((bmf, D), BF16), pltpu.VMEM((bmf, D), F32)],
        compiler_params=_params("parallel", "arbitrary"), name="ffn",
    )(h1, w_fi, w_fi, w_fd, row(ln2_g), row(ln2_b))

    return out.reshape(B, S, D)
```

```python
import functools
import math

import jax
import jax.numpy as jnp
from jax import lax
from jax.experimental import pallas as pl
from jax.experimental.pallas import tpu as pltpu

MLA_HEADS = 8
MLA_NOPE_DIM = 128
MLA_ROPE_DIM = 64
MLA_V_DIM = 128
MLA_Q_RANK = 512
MLA_KV_RANK = 512
DIFF_HEADS = 8
DIFF_QK_DIM = 64
DIFF_V_DIM = 128
ROPE_THETA = 10000.0
DEPTH = 1
ALPHA = (2 * DEPTH) ** 0.25
RMS_EPS = 1e-6
SUBLN_EPS = 1e-5
LN_EPS = 1e-5

LOG2_E = math.log2(math.e)
LANES = 128
MLA_QK_PAD = 256
SUM_ROWS = 16
ROPE_HALF = MLA_ROPE_DIM // 2
VMEM_LIMIT = 56 * 1024 * 1024

PROJ_ROWS = 512
MIX_ROWS = 256
FFN_ROWS = 512
FFN_COLS = 512
ATTN_HEADS_PER_STEP = 4

BF16 = jnp.bfloat16
F32 = jnp.float32


def _dot(a, b):
    return jnp.dot(a, b, preferred_element_type=F32)


def _dot_nt(a, b):
    return lax.dot_general(a, b, (((1,), (1,)), ((), ())), preferred_element_type=F32)


def _rms(x, g, eps):
    return x * lax.rsqrt(jnp.mean(x * x, axis=-1, keepdims=True) + eps) * g


def _layer_norm(x, g, b):
    mu = jnp.mean(x, axis=-1, keepdims=True)
    xc = x - mu
    var = jnp.mean(xc * xc, axis=-1, keepdims=True)
    return xc * lax.rsqrt(var + LN_EPS) * g + b


def _rope(x, cos, sin_signed):
    up = pltpu.roll(x, LANES - ROPE_HALF, 1)
    dn = pltpu.roll(x, ROPE_HALF, 1)
    first_half = lax.broadcasted_iota(jnp.int32, x.shape, 1) % (2 * ROPE_HALF) < ROPE_HALF
    return x * cos + jnp.where(first_half, up, dn) * sin_signed


def _sigmoid(x):
    return 0.5 + 0.5 * jnp.tanh(0.5 * x)


def _const_spec(shape):
    return pl.BlockSpec(shape, lambda *_: (0,) * len(shape), pipeline_mode=pl.Buffered(1))


def _params(*sem):
    return pltpu.CompilerParams(dimension_semantics=sem, vmem_limit_bytes=VMEM_LIMIT)


def _with_casts(body, n_in, n_out, n_cast):
    def kernel(*refs):
        ins, refs = refs[:n_in], refs[n_in:]
        slabs_in, refs = refs[:n_cast], refs[n_cast:]
        outs, refs = refs[:n_out], refs[n_out:]
        slabs_out, scratch = refs[:n_cast], refs[n_cast:]
        for src, dst in zip(slabs_in, slabs_out):
            dst[...] = src[...].astype(BF16)
        body(*ins, *outs, *scratch)
    return kernel


def _slab_specs(arrays, steps):
    ins, outs, shapes = [], [], []
    for a in arrays:
        rows, width = a.shape
        spec = pl.BlockSpec((rows // steps, width), lambda i: (i, 0))
        ins.append(spec)
        outs.append(spec)
        shapes.append(jax.ShapeDtypeStruct(a.shape, BF16))
    return ins, outs, shapes


def _rope_rows(x, cos_t, sin_t):
    half = cos_t.shape[0]
    out = []
    for j in range(x.shape[0] // (2 * half)):
        x1 = x[2 * half * j:2 * half * j + half]
        x2 = x[2 * half * j + half:2 * half * (j + 1)]
        out += [x1 * cos_t - x2 * sin_t, x2 * cos_t + x1 * sin_t]
    return jnp.concatenate(out, axis=0)


def _mla_proj_kernel(x_ref, w_lat_ref, gq_ref, gkv_ref, w_qnt_ref, w_qpt_ref, w_kn_ref, w_vt_ref,
                     cos_ref, sin_ref, cost_ref, sint_ref, qt_ref, k_ref, vt_ref, w_lat_bf_ref, *, scale):
    @pl.when(pl.program_id(0) == 0)
    def _():
        w_lat_bf_ref[...] = w_lat_ref[...].astype(BF16)

    xb = x_ref[...].astype(BF16)
    lat = _dot_nt(xb, w_lat_bf_ref[...])
    cq = _rms(lat[:, :MLA_Q_RANK], gq_ref[...], RMS_EPS).astype(BF16)
    ckv = _rms(lat[:, MLA_Q_RANK:MLA_Q_RANK + MLA_KV_RANK], gkv_ref[...], RMS_EPS).astype(BF16)
    kr = lat[:, MLA_Q_RANK + MLA_KV_RANK:]
    kr = jnp.where(lax.broadcasted_iota(jnp.int32, kr.shape, 1) < MLA_ROPE_DIM, kr, 0.0)
    k_pe = _rope(kr, cos_ref[...], sin_ref[...]).astype(BF16)
    kn = _dot(ckv, w_kn_ref[...])
    vt_ref[...] = _dot_nt(w_vt_ref[...], ckv).astype(BF16)
    qnt = _dot_nt(w_qnt_ref[...], cq)
    qpt = _dot_nt(w_qpt_ref[...], cq)
    cos_t, sin_t = cost_ref[...], sint_ref[...]
    pad = jnp.zeros((MLA_QK_PAD - MLA_NOPE_DIM - MLA_ROPE_DIM, qnt.shape[1]), BF16)
    for h in range(MLA_HEADS):
        base = h * MLA_QK_PAD
        qt_ref[base:base + MLA_NOPE_DIM, :] = (qnt[h * MLA_NOPE_DIM:(h + 1) * MLA_NOPE_DIM] * scale).astype(BF16)
        q_pe = _rope_rows(qpt[h * MLA_ROPE_DIM:(h + 1) * MLA_ROPE_DIM], cos_t, sin_t)
        qt_ref[base + MLA_NOPE_DIM:base + MLA_NOPE_DIM + MLA_ROPE_DIM, :] = (q_pe * scale).astype(BF16)
        qt_ref[base + MLA_NOPE_DIM + MLA_ROPE_DIM:base + MLA_QK_PAD, :] = pad
        k_ref[:, base:base + LANES] = kn[:, h * LANES:(h + 1) * LANES].astype(BF16)
        k_ref[:, base + LANES:base + 2 * LANES] = k_pe


def _diff_proj_kernel(x_ref, w_q_ref, w_k_ref, w_vt_ref, cos_ref, sin_ref, cost_ref, sint_ref,
                      qt_ref, k_ref, vt_ref, *, scale):
    xb = x_ref[...].astype(BF16)
    yk = _dot_nt(xb, w_k_ref[...])
    vt_ref[...] = _dot_nt(w_vt_ref[...], xb).astype(BF16)
    qt = _dot_nt(w_q_ref[...], xb)
    qt_ref[...] = (_rope_rows(qt, cost_ref[...], sint_ref[...]) * scale).astype(BF16)
    cos, sin = cos_ref[...], sin_ref[...]
    for h in range(DIFF_HEADS):
        lo, hi = h * LANES, (h + 1) * LANES
        k_ref[:, lo:hi] = _rope(yk[:, lo:hi], cos, sin).astype(BF16)


def _flash_cols(qts, k_ref, vt_ref, scratch, q_tile, tq):
    n_heads = len(qts)
    per_head = len(scratch) // n_heads
    tk = tq // 2
    dk, cols = qts[0].shape
    dv = vt_ref.shape[2]
    ones = jnp.ones((SUM_ROWS, tk), BF16)

    class Head:
        def __init__(self, h):
            (s0, s1, p0, p1, self.m, a0, a1, self.acc) = scratch[h * per_head:(h + 1) * per_head]
            self.qt = qts[h]
            self.s, self.p, self.a = (s0, s1), (p0, p1), (a0, a1)
            self.lanes = slice(h * dk, (h + 1) * dk)
            self.h = h

    heads = [Head(h) for h in range(n_heads)]
    for hd in heads:
        hd.m[...] = jnp.full(hd.m.shape, -jnp.inf, F32)
        hd.acc[...] = jnp.zeros(hd.acc.shape, F32)
        hd.p[1][...] = jnp.zeros(hd.p[1].shape, BF16)
        hd.a[1][...] = jnp.ones(hd.a[1].shape, F32)

    all_groups = [(0, cols // LANES)]
    late_groups = [((m * tq + tk) // LANES, (m + 1) * tq // LANES) for m in range(cols // tq)]

    def scores(hd, c, half, runs=all_groups):
        k = k_ref[pl.ds(pl.multiple_of(c * tk, tk), tk), hd.lanes]
        for g0, g1 in runs:
            s = _dot(k, hd.qt[:, g0 * LANES:g1 * LANES])
            for g in range(g0, g1):
                hd.s[half][g] = s[:, (g - g0) * LANES:(g - g0 + 1) * LANES]

    def softmax(hd, half, diagonal, runs=all_groups):
        for g in [g for g0, g1 in runs for g in range(g0, g1)]:
            sl = slice(g * LANES, (g + 1) * LANES)
            s = hd.s[half][g]
            if diagonal:
                k_pos = half * tk + lax.broadcasted_iota(jnp.int32, (tk, LANES), 0)
                q_pos = (g * LANES) % tq + lax.broadcasted_iota(jnp.int32, (tk, LANES), 1)
                s = jnp.where(k_pos <= q_pos, s, -jnp.inf)
            m_old = hd.m[:, sl]
            m_new = jnp.maximum(m_old, jnp.max(s, axis=0, keepdims=True))
            a = jnp.exp2(m_old - m_new)
            p = jnp.exp2(s - m_new)
            hd.m[:, sl] = m_new
            hd.a[half][:, sl] = a
            hd.p[half][g] = p.astype(BF16)

    def weighted_values(hd, pair, half, runs=all_groups):
        vt = jnp.concatenate([vt_ref[pair, hd.h][:, half * tk:(half + 1) * tk], ones], axis=0)
        for g0, g1 in runs:
            sl = slice(g0 * LANES, g1 * LANES)
            p = jnp.concatenate([hd.p[half][g] for g in range(g0, g1)], axis=1)
            hd.acc[:, sl] = hd.a[half][:, sl] * hd.acc[:, sl] + _dot(vt, p)

    def pair_step(i, diagonal):
        second = late_groups if diagonal else all_groups
        for hd in heads:
            weighted_values(hd, jnp.maximum(i - 1, 0), 1)
        for hd in heads:
            scores(hd, 2 * i + 1, 1, second)
        for hd in heads:
            softmax(hd, 0, diagonal)
        if not diagonal:
            for hd in heads:
                scores(hd, 2 * i + 2, 0)
        for hd in heads:
            weighted_values(hd, i, 0)
        for hd in heads:
            softmax(hd, 1, diagonal, second)

    for hd in heads:
        scores(hd, 0, 0)
    lax.fori_loop(0, q_tile, lambda i, c: (pair_step(i, False), c)[1], 0)
    pair_step(q_tile, True)
    for hd in heads:
        weighted_values(hd, q_tile, 1, late_groups)
    return [(hd.acc[:dv, :], hd.acc[dv:dv + 1, :]) for hd in heads]


def _flash_scratch(n_heads, dv, cols, tq):
    tk = tq // 2
    stat = pltpu.VMEM((1, cols), F32)
    group_major = lambda dtype: pltpu.VMEM((cols // LANES, tk, LANES), dtype)
    return n_heads * [group_major(F32), group_major(F32), group_major(BF16), group_major(BF16),
                      stat, stat, stat, pltpu.VMEM((dv + SUM_ROWS, cols), F32)]


def _mla_attn_kernel(qt_ref, k_ref, vt_ref, o_ref, *scratch, tq, n_heads):
    qts, scratch = scratch[:n_heads], scratch[n_heads:]
    for h in range(n_heads):
        qts[h][...] = qt_ref[h]
    for h, (acc, l) in enumerate(_flash_cols(qts, k_ref, vt_ref, scratch, pl.program_id(2), tq)):
        o_ref[:, h * MLA_V_DIM:(h + 1) * MLA_V_DIM] = (acc / l).T.astype(o_ref.dtype)


def _diff_attn_kernel(qt_ref, k_ref, vt_ref, lq1_ref, lk1_ref, lq2_ref, lk2_ref, g_ref, o_ref, *scratch,
                      tq, n_heads, lambda_init):
    qt2_refs, scratch = scratch[:n_heads], scratch[n_heads:]
    row = lax.broadcasted_iota(jnp.int32, (LANES, tq), 0)
    zero = jnp.zeros((LANES, tq), BF16)
    for h in range(n_heads):
        qt = qt_ref[h]
        qt2_refs[h][:, :tq] = jnp.where(row < DIFF_QK_DIM, qt, zero)
        qt2_refs[h][:, tq:] = jnp.where(row >= DIFF_QK_DIM, qt, zero)
    lam = (jnp.exp(jnp.sum(lq1_ref[...] * lk1_ref[...], axis=-1, keepdims=True))
           - jnp.exp(jnp.sum(lq2_ref[...] * lk2_ref[...], axis=-1, keepdims=True)) + lambda_init)
    for h, (acc, l) in enumerate(_flash_cols(qt2_refs, k_ref, vt_ref, scratch, pl.program_id(2), tq)):
        o = acc / l
        o = (o[:, :tq] - lam * o[:, tq:]).T
        o_ref[:, h * DIFF_V_DIM:(h + 1) * DIFF_V_DIM] = (
            _rms(o, g_ref[...], SUBLN_EPS) * (1.0 - lambda_init)).astype(o_ref.dtype)


def _mix_out_kernel(a_ref, b_ref, x_ref, wg_ref, wa_ref, wb_ref, wo_ref, g_ref, beta_ref, o_ref):
    d = o_ref.shape[1]
    xb = x_ref[...].astype(BF16)
    ma = _sigmoid(_dot_nt(xb, wg_ref[:d, :])) * _dot(a_ref[...], wa_ref[...])
    mb = _sigmoid(_dot_nt(xb, wg_ref[d:, :])) * _dot(b_ref[...], wb_ref[...])
    mixed = _dot((ma + mb).astype(BF16), wo_ref[...])
    o_ref[...] = _layer_norm(ALPHA * x_ref[...] + mixed, g_ref[...], beta_ref[...])


def _ffn_kernel(h_ref, wg_ref, wu_ref, wd_ref, g_ref, beta_ref, o_ref, hb_ref, acc_ref):
    f = pl.program_id(1)

    @pl.when(f == 0)
    def _():
        h = h_ref[...]
        hb_ref[...] = h.astype(BF16)
        acc_ref[...] = ALPHA * h

    hb = hb_ref[...]
    gate = _dot(hb, wg_ref[...])
    up = _dot(hb, wu_ref[...])
    act = (gate * _sigmoid(gate) * up).astype(BF16)
    acc_ref[...] += _dot(act, wd_ref[...])

    @pl.when(f == pl.num_programs(1) - 1)
    def _():
        o_ref[...] = _layer_norm(acc_ref[...], g_ref[...], beta_ref[...])


def _rope_angles(seq, dim):
    inv_freq = 1.0 / (ROPE_THETA ** (jnp.arange(0, dim, 2, dtype=F32) / dim))
    ang = jnp.arange(seq, dtype=F32)[:, None] * inv_freq[None, :]
    return jnp.cos(ang), jnp.sin(ang)


def kernel(x, w_in, mla_q_norm, mla_w_uq, mla_kv_norm, mla_w_ukv, diff_lambda_q1, diff_lambda_k1,
           diff_lambda_q2, diff_lambda_k2, diff_subln, w_branch_a, w_branch_b, w_out, ln1_g, ln1_b,
           w_ffn_in, w_ffn_down, ln2_g, ln2_b):
    B, S, D = x.shape
    T = B * S
    H = MLA_HEADS
    d_ff = w_ffn_down.shape[1]
    lambda_init = 0.8 - 0.6 * math.exp(-0.3 * 0)
    x2 = x.reshape(T, D)

    w = w_in[0]
    n_diff = DIFF_HEADS * 2 * DIFF_QK_DIM
    o_kr = MLA_Q_RANK + MLA_KV_RANK
    o_dq = o_kr + MLA_ROPE_DIM
    wt = jnp.swapaxes(w, 0, 1)
    w_rows = lambda start, n: pl.BlockSpec((pl.Element(n), pl.Element(D)), lambda *_: (start, 0),
                                           pipeline_mode=pl.Buffered(1))
    uq = mla_w_uq[0].reshape(MLA_Q_RANK, H, MLA_NOPE_DIM + MLA_ROPE_DIM)
    w_qnt = uq[:, :, :MLA_NOPE_DIM].reshape(MLA_Q_RANK, H * MLA_NOPE_DIM).T.astype(BF16)
    w_qpt = uq[:, :, MLA_NOPE_DIM:].reshape(MLA_Q_RANK, H * MLA_ROPE_DIM).T.astype(BF16)
    ukv = mla_w_ukv[0].reshape(MLA_KV_RANK, H, MLA_NOPE_DIM + MLA_V_DIM)
    w_kn = ukv[:, :, :MLA_NOPE_DIM].reshape(MLA_KV_RANK, H * LANES).astype(BF16)
    w_vt = ukv[:, :, MLA_NOPE_DIM:].reshape(MLA_KV_RANK, H * MLA_V_DIM).T.astype(BF16)
    row = lambda v: v.reshape(1, -1).astype(F32)

    assert MLA_ROPE_DIM == DIFF_QK_DIM
    cos, sin = _rope_angles(S, MLA_ROPE_DIM)
    cos_k, sin_k = jnp.tile(cos, (1, 4)), jnp.tile(jnp.concatenate([-sin, sin], axis=1), (1, 2))
    cos_t, sin_t = cos.T, sin.T

    bm = PROJ_ROWS
    n_pos = S // bm
    tok = lambda n: pl.BlockSpec((bm, n), lambda i: (i, 0))
    pos = pl.BlockSpec((bm, LANES), lambda i: (i % n_pos, 0))
    vt_spec = lambda n: pl.BlockSpec((None, n, bm), lambda i: (i, 0, 0))
    vt_shape = lambda n: jax.ShapeDtypeStruct((T // bm, n, bm), BF16)
    pos_t = pl.BlockSpec((cos_t.shape[0], bm), lambda i: (0, i % n_pos))
    n_steps = T // bm
    rest_rows = (wt.shape[0] - o_dq) // n_steps
    row_unit = math.gcd(o_dq, rest_rows)
    qt_a, k_a, vt_a, w_rest = pl.pallas_call(
        _with_casts(functools.partial(_mla_proj_kernel, scale=LOG2_E * (MLA_NOPE_DIM + MLA_ROPE_DIM) ** -0.5),
                    12, 3, 1),
        grid=(n_steps,),
        in_specs=[tok(D), w_rows(0, o_kr + LANES), _const_spec((1, MLA_Q_RANK)), _const_spec((1, MLA_KV_RANK)),
                  _const_spec(w_qnt.shape), _const_spec(w_qpt.shape), _const_spec(w_kn.shape),
                  _const_spec(w_vt.shape), pos, pos, pos_t, pos_t,
                  pl.BlockSpec((pl.Element(rest_rows), pl.Element(D)),
                               lambda i: ((o_dq // row_unit + rest_rows // row_unit * i) * row_unit, 0))],
        out_specs=[vt_spec(H * MLA_QK_PAD), tok(H * MLA_QK_PAD), vt_spec(H * MLA_V_DIM),
                   pl.BlockSpec((rest_rows, D), lambda i: (i, 0))],
        out_shape=[vt_shape(H * MLA_QK_PAD), jax.ShapeDtypeStruct((T, H * MLA_QK_PAD), BF16),
                   vt_shape(H * MLA_V_DIM), jax.ShapeDtypeStruct((wt.shape[0] - o_dq, D), BF16)],
        scratch_shapes=[pltpu.VMEM((o_kr + LANES, D), BF16)],
        compiler_params=_params("arbitrary"), name="mla_proj",
    )(x2, wt, row(mla_q_norm), row(mla_kv_norm), w_qnt, w_qpt, w_kn, w_vt, cos_k, sin_k, cos_t, sin_t, wt)

    later_weights = [w_ffn_down[0], w_branch_a[0], w_branch_b[0], w_out[0]]
    c_in, c_out, c_shape = _slab_specs(later_weights, T // bm)
    qt_b, k_b, vt_b, w_fd, w_a, w_b, w_o = pl.pallas_call(
        _with_casts(functools.partial(_diff_proj_kernel, scale=LOG2_E * DIFF_QK_DIM ** -0.5), 8, 3,
                    len(later_weights)),
        grid=(T // bm,),
        in_specs=[tok(D), w_rows(0, n_diff), w_rows(n_diff, n_diff), w_rows(2 * n_diff, n_diff),
                  pos, pos, pos_t, pos_t] + c_in,
        out_specs=[vt_spec(n_diff), tok(n_diff), vt_spec(n_diff)] + c_out,
        out_shape=[vt_shape(n_diff), jax.ShapeDtypeStruct((T, n_diff), BF16), vt_shape(n_diff)] + c_shape,
        compiler_params=_params("parallel"), name="diff_proj",
    )(x2, w_rest, w_rest, w_rest, cos_k, sin_k, cos_t, sin_t, *later_weights)

    tq = bm
    hp = ATTN_HEADS_PER_STEP
    head_blk = lambda rows, width, full: pl.BlockSpec(
        (None, rows, hp * width), (lambda b, h, i: (b, 0, h)) if full else (lambda b, h, i: (b, i, h)))
    vt_blk = lambda dv: pl.BlockSpec((None, S // tq, hp, dv, tq), lambda b, h, i: (b, 0, h, 0, 0))
    qt_blk = lambda dk: pl.BlockSpec((None, None, hp, dk, tq), lambda b, h, i: (b, i, h, 0, 0))
    attn_a = pl.pallas_call(
        functools.partial(_mla_attn_kernel, tq=tq, n_heads=hp),
        grid=(B, H // hp, S // tq),
        in_specs=[qt_blk(MLA_QK_PAD), head_blk(S, MLA_QK_PAD, True), vt_blk(MLA_V_DIM)],
        out_specs=head_blk(tq, MLA_V_DIM, False),
        out_shape=jax.ShapeDtypeStruct((B, S, H * MLA_V_DIM), BF16),
        scratch_shapes=hp * [pltpu.VMEM((MLA_QK_PAD, tq), BF16)] + _flash_scratch(hp, MLA_V_DIM, tq, tq),
        compiler_params=_params("parallel", "parallel", "parallel"), name="mla_attn",
    )(qt_a.reshape(B, S // tq, H, MLA_QK_PAD, tq), k_a.reshape(B, S, -1),
      vt_a.reshape(B, S // tq, H, MLA_V_DIM, tq))

    lam_spec = pl.BlockSpec((1, DIFF_QK_DIM), lambda b, h, i: (0, 0))
    attn_b = pl.pallas_call(
        functools.partial(_diff_attn_kernel, tq=tq, n_heads=hp, lambda_init=lambda_init),
        grid=(B, DIFF_HEADS // hp, S // tq),
        in_specs=[qt_blk(LANES), head_blk(S, LANES, True), vt_blk(DIFF_V_DIM),
                  lam_spec, lam_spec, lam_spec, lam_spec, pl.BlockSpec((1, DIFF_V_DIM), lambda b, h, i: (0, 0))],
        out_specs=head_blk(tq, DIFF_V_DIM, False),
        out_shape=jax.ShapeDtypeStruct((B, S, DIFF_HEADS * DIFF_V_DIM), BF16),
        scratch_shapes=hp * [pltpu.VMEM((LANES, 2 * tq), BF16)] + _flash_scratch(hp, DIFF_V_DIM, 2 * tq, tq),
        compiler_params=_params("parallel", "parallel", "parallel"), name="diff_attn",
    )(qt_b.reshape(B, S // tq, DIFF_HEADS, LANES, tq), k_b.reshape(B, S, -1),
      vt_b.reshape(B, S // tq, DIFF_HEADS, DIFF_V_DIM, tq),
      row(diff_lambda_q1), row(diff_lambda_k1), row(diff_lambda_q2), row(diff_lambda_k2), row(diff_subln))

    bmo = MIX_ROWS
    tokm = lambda n: pl.BlockSpec((bmo, n), lambda i: (i, 0))
    c_in, c_out, c_shape = _slab_specs([w_ffn_in[0]], T // bmo)
    h1, w_fi = pl.pallas_call(
        _with_casts(_mix_out_kernel, 9, 1, 1),
        grid=(T // bmo,),
        in_specs=[tokm(w_a.shape[0]), tokm(w_b.shape[0]), tokm(D), w_rows(3 * n_diff, 2 * D),
                  _const_spec(w_a.shape), _const_spec(w_b.shape), _const_spec(w_o.shape),
                  _const_spec((1, D)), _const_spec((1, D))] + c_in,
        out_specs=[tokm(D)] + c_out,
        out_shape=[jax.ShapeDtypeStruct((T, D), F32)] + c_shape,
        compiler_params=_params("parallel"), name="mix_out",
    )(attn_a.reshape(T, -1), attn_b.reshape(T, -1), x2, w_rest, w_a, w_b, w_o, row(ln1_g), row(ln1_b), w_ffn_in[0])

    bmf, tf = FFN_ROWS, FFN_COLS
    n_f = d_ff // tf
    out = pl.pallas_call(
        _ffn_kernel,
        grid=(T // bmf, n_f),
        in_specs=[pl.BlockSpec((bmf, D), lambda i, f: (i, 0)),
                  pl.BlockSpec((D, tf), lambda i, f: (0, f)),
                  pl.BlockSpec((D, tf), lambda i, f: (0, n_f + f)),
                  pl.BlockSpec((tf, D), lambda i, f: (f, 0)),
                  pl.BlockSpec((1, D), lambda i, f: (0, 0)), pl.BlockSpec((1, D), lambda i, f: (0, 0))],
        out_specs=pl.BlockSpec((bmf, D), lambda i, f: (i, 0)),
        out_shape=jax.ShapeDtypeStruct((T, D), F32),
        scratch_shapes=[pltpu.VMEM((bmf, D), BF16), pltpu.VMEM((bmf, D), F32)],
        compiler_params=_params("parallel", "arbitrary"), name="ffn",
    )(h1, w_fi, w_fi, w_fd, row(ln2_g), row(ln2_b))

    return out.reshape(B, S, D)
```
